```python
import math
import jax
import jax.numpy as jnp
from jax import lax
import numpy as np


D_MODEL = 2048
BATCH = 1
SEQ = 16384
DEPTH = 1

CHUNK = 64
PLE_DIM = 256
EPS = 1e-6

RET_HEADS = 8
RET_HEAD_DIM = 128
RET_WIDTH = RET_HEADS * RET_HEAD_DIM
ROPE_THETA = 10000.0

SSM_WIDTH = 1024
SSM_GROUP_SIZE = 16
SSM_GROUPS = SSM_WIDTH // SSM_GROUP_SIZE
SSM_STATE = 64
DT_MIN = 1e-3
DT_MAX = 1e-1

IN_COLS = 4 * RET_WIDTH + SSM_WIDTH

N_EXPERT_GROUPS = 4
EXPERTS_PER_GROUP = 8
N_EXPERTS = N_EXPERT_GROUPS * EXPERTS_PER_GROUP
TOP_K_INNER = 2
D_EXPERT = 512
MOE_BLOCK = 128

kernel_name = 'hybrid_retention_s5_hmoe_block'


def _rmsnorm(x, w):
    xf = x.astype(jnp.float32)
    y = xf * lax.rsqrt(jnp.mean(xf * xf, axis=-1, keepdims=True) + EPS)
    return (y * w.astype(jnp.float32)).astype(x.dtype)


def _rope(t):
    seq = t.shape[1]
    half = t.shape[-1] // 2
    inv_freq = ROPE_THETA ** (-jnp.arange(half, dtype=jnp.float32) / half)
    ang = jnp.arange(seq, dtype=jnp.float32)[:, None] * inv_freq[None, :]
    cos = jnp.cos(ang)[None, :, None, :]
    sin = jnp.sin(ang)[None, :, None, :]
    t1, t2 = t[..., :half], t[..., half:]
    return jnp.concatenate([t1 * cos - t2 * sin, t1 * sin + t2 * cos], axis=-1)


def _retention(q, k, v, g, gn_w):
    bsz, seq, nh, dh = q.shape
    nc = seq // CHUNK
    q = _rope(q)
    k = _rope(k) * (dh ** -0.5)
    log_gamma = jnp.log1p(-jnp.exp2(-5.0 - jnp.arange(nh, dtype=jnp.float32)))
    pos = jnp.arange(CHUNK, dtype=jnp.float32)
    intra_decay = jnp.exp(log_gamma[:, None, None] * jnp.abs(pos[:, None] - pos[None, :]))
    key_decay = jnp.exp(log_gamma[:, None] * (CHUNK - 1.0 - pos)[None, :])
    query_decay = jnp.exp(log_gamma[:, None] * (pos + 1.0)[None, :])
    chunk_decay = jnp.exp(log_gamma * CHUNK)
    qc = q.reshape(bsz, nc, CHUNK, nh, dh)
    kc = k.reshape(bsz, nc, CHUNK, nh, dh)
    vc = v.reshape(bsz, nc, CHUNK, nh, dh)
    scores = jnp.einsum('bnqhd,bnkhd->bnhqk', qc, kc) * intra_decay
    y_intra = jnp.einsum('bnhqk,bnkhe->bnqhe', scores, vc)
    kv = jnp.einsum('bnkhd,bnkhe,hk->nbhde', kc, vc, key_decay)

    def step(state, kv_n):
        return state * chunk_decay[None, :, None, None] + kv_n, state

    _, state_prev = lax.scan(step, jnp.zeros((bsz, nh, dh, dh), jnp.float32), kv)
    y_cross = jnp.einsum('bnqhd,nbhde,hq->bnqhe', qc, state_prev, query_decay)
    y = (y_intra + y_cross).reshape(bsz, seq, nh, dh)
    mu = jnp.mean(y, axis=-1, keepdims=True)
    var = jnp.mean(jnp.square(y - mu), axis=-1, keepdims=True)
    y = ((y - mu) * lax.rsqrt(var + EPS)).reshape(bsz, seq, nh * dh) * gn_w
    return jax.nn.silu(g) * y


def _s5(u, lam_re, lam_im, log_dt, b_re, b_im, c_re, c_im, d, w_glu):
    bsz, seq, _ = u.shape
    f32 = jnp.float32
    ug = u.reshape(bsz, seq, SSM_GROUPS, SSM_GROUP_SIZE)
    lam = lax.complex(jnp.minimum(lam_re.astype(f32), -1e-4), lam_im.astype(f32))
    dt = jnp.exp(log_dt.astype(f32))[:, None]
    log_lam_bar = lam * dt
    lam_bar = jnp.exp(log_lam_bar)
    b_mat = lax.complex(b_re.astype(f32), b_im.astype(f32))
    b_bar = ((lam_bar - 1.0) / lam)[..., None] * b_mat
    bu = jnp.einsum('gpc,bsgc->bsgp', b_bar, ug.astype(jnp.complex64))
    steps = jnp.ones((1, seq, 1, 1), f32)

    def combine(left, right):
        n_l, s_l = left
        n_r, s_r = right
        return n_l + n_r, jnp.exp(n_r.astype(jnp.complex64) * log_lam_bar) * s_l + s_r

    _, states = lax.associative_scan(combine, (steps, bu), axis=1)
    c_mat = lax.complex(c_re.astype(f32), c_im.astype(f32))
    y = jnp.real(jnp.einsum('gcp,bsgp->bsgc', c_mat, states)).reshape(bsz, seq, SSM_WIDTH)
    y = y + d.astype(f32) * u
    y = jax.nn.gelu(y)
    return y * jax.nn.sigmoid(y @ w_glu.astype(f32))


def _hybrid_mixer(h, w_in, ret_gn_w, lam_re, lam_im, log_dt, b_re, b_im, c_re, c_im,
                  ssm_d, w_glu, w_branch_a, w_branch_b, w_merge, w_out):
    bsz, seq, _ = h.shape
    f32 = jnp.float32
    proj = h @ w_in
    q, k, v, g, u = jnp.split(proj, [RET_WIDTH, 2 * RET_WIDTH, 3 * RET_WIDTH, 4 * RET_WIDTH], axis=-1)

    def heads(t):
        return t.astype(f32).reshape(bsz, seq, RET_HEADS, RET_HEAD_DIM)

    y_ret = _retention(heads(q), heads(k), heads(v), g.astype(f32), ret_gn_w.astype(f32)).astype(h.dtype)
    y_ssm = _s5(u.astype(f32), lam_re, lam_im, log_dt, b_re, b_im, c_re, c_im, ssm_d, w_glu).astype(h.dtype)
    gates = jax.nn.sigmoid(h @ w_merge).reshape(bsz, seq, 2, D_MODEL)
    merged = gates[:, :, 0] * (y_ret @ w_branch_a) + gates[:, :, 1] * (y_ssm @ w_branch_b)
    return merged @ w_out


def _hier_moe(h, w_rg, b_rg, w_re, b_re, w_gate, w_up, w_down):
    bsz, seq, dm = h.shape
    n_tok = bsz * seq
    ht = h.reshape(n_tok, dm)
    g_prob = jax.nn.softmax((ht @ w_rg).astype(jnp.float32) + b_rg.astype(jnp.float32), axis=-1)
    g_w, g_idx = lax.top_k(g_prob, 1)
    e_logits = ((ht @ w_re).astype(jnp.float32) + b_re.astype(jnp.float32)).reshape(
        n_tok, N_EXPERT_GROUPS, EXPERTS_PER_GROUP)
    e_sel = e_logits[jnp.arange(n_tok), g_idx[:, 0]]
    top_logit, top_j = lax.top_k(e_sel, TOP_K_INNER)
    e_w = jax.nn.softmax(top_logit, axis=-1) * g_w
    expert = g_idx * EXPERTS_PER_GROUP + top_j
    n_assign = n_tok * TOP_K_INNER
    flat_e = expert.reshape(-1).astype(jnp.int32)
    flat_tok = jnp.repeat(jnp.arange(n_tok, dtype=jnp.int32), TOP_K_INNER)
    flat_w = e_w.reshape(-1)
    order = jnp.argsort(flat_e)
    se, stok, sw = flat_e[order], flat_tok[order], flat_w[order]
    counts = jnp.bincount(flat_e, length=N_EXPERTS).astype(jnp.int32)
    start = jnp.cumsum(counts) - counts
    padded = (counts + MOE_BLOCK - 1) // MOE_BLOCK * MOE_BLOCK
    pad_end = jnp.cumsum(padded)
    pad_start = pad_end - padded
    dest = pad_start[se] + (jnp.arange(n_assign, dtype=jnp.int32) - start[se])
    n_rows = n_assign + N_EXPERTS * MOE_BLOCK
    n_blocks = n_rows // MOE_BLOCK
    row_tok = jnp.zeros((n_rows,), jnp.int32).at[dest].set(stok)
    row_w = jnp.zeros((n_rows,), jnp.float32).at[dest].set(sw)
    block_start = jnp.arange(n_blocks, dtype=jnp.int32) * MOE_BLOCK
    block_expert = jnp.minimum(jnp.searchsorted(pad_end, block_start, side='right'), N_EXPERTS - 1)

    def expert_block(args):
        e, toks = args
        xb = ht[toks]
        a = jax.nn.silu(xb @ w_gate[e]) * (xb @ w_up[e])
        return a @ w_down[e]

    y_rows = lax.map(expert_block, (block_expert, row_tok.reshape(n_blocks, MOE_BLOCK)))
    y_rows = y_rows.reshape(n_rows, dm) * row_w[:, None].astype(h.dtype)
    out = jax.ops.segment_sum(y_rows, row_tok, num_segments=n_tok)
    return out.reshape(bsz, seq, dm)


def setup_inputs(seed: int = 0) -> dict:
    key = jax.random.key(seed)
    ks = jax.random.split(key, 32)
    f32 = jnp.float32

    def nrm(k, shape, scale):
        return jax.random.normal(k, shape, f32) * scale

    def gain(k, shape):
        return 1.0 + 0.02 * jax.random.normal(k, shape, f32)

    L, D = DEPTH, D_MODEL
    R, W, G, P, Cg = RET_WIDTH, SSM_WIDTH, SSM_GROUPS, SSM_STATE, SSM_GROUP_SIZE
    E, F = N_EXPERTS, D_EXPERT
    return {
        'x': nrm(ks[0], (BATCH, SEQ, D), 1.0),
        'p': nrm(ks[1], (L, BATCH, SEQ, PLE_DIM), 1.0),
        'norm_mix': gain(ks[2], (L, D)),
        'w_in': nrm(ks[3], (L, D, IN_COLS), D ** -0.5),
        'ret_gn_w': gain(ks[4], (L, R)),
        'ssm_lam_re': -0.5 + nrm(ks[5], (L, G, P), 0.01),
        'ssm_lam_im': math.pi * jnp.arange(P, dtype=f32) + nrm(ks[6], (L, G, P), 0.01),
        'ssm_log_dt': jax.random.uniform(ks[7], (L, G), f32, math.log(DT_MIN), math.log(DT_MAX)),
        'ssm_b_re': nrm(ks[8], (L, G, P, Cg), (2 * Cg) ** -0.5),
        'ssm_b_im': nrm(ks[9], (L, G, P, Cg), (2 * Cg) ** -0.5),
        'ssm_c_re': nrm(ks[10], (L, G, Cg, P), P ** -0.5),
        'ssm_c_im': nrm(ks[11], (L, G, Cg, P), P ** -0.5),
        'ssm_d': nrm(ks[12], (L, W), 1.0),
        'w_glu': nrm(ks[13], (L, W, W), W ** -0.5),
        'w_branch_a': nrm(ks[14], (L, R, D), R ** -0.5),
        'w_branch_b': nrm(ks[15], (L, W, D), W ** -0.5),
        'w_merge': nrm(ks[16], (L, D, 2 * D), D ** -0.5),
        'w_out': nrm(ks[17], (L, D, D), D ** -0.5),
        'norm_ffn': gain(ks[18], (L, D)),
        'w_router_group': nrm(ks[19], (L, D, N_EXPERT_GROUPS), D ** -0.5),
        'b_router_group': nrm(ks[20], (L, N_EXPERT_GROUPS), 0.01),
        'w_router_expert': nrm(ks[21], (L, D, E), D ** -0.5),
        'b_router_expert': nrm(ks[22], (L, E), 0.01),
        'w_exp_gate': nrm(ks[23], (L, E, D, F), D ** -0.5),
        'w_exp_up': nrm(ks[24], (L, E, D, F), D ** -0.5),
        'w_exp_down': nrm(ks[25], (L, E, F, D), F ** -0.5),
        'norm_ple': gain(ks[26], (L, D)),
        'w_ple_gate': nrm(ks[27], (L, D, D), D ** -0.5),
        'w_ple': nrm(ks[28], (L, PLE_DIM, D), PLE_DIM ** -0.5),
        'norm_f': gain(ks[29], (D,)),
    }


def reference(x, p, norm_mix, w_in, ret_gn_w, ssm_lam_re, ssm_lam_im, ssm_log_dt,
              ssm_b_re, ssm_b_im, ssm_c_re, ssm_c_im, ssm_d, w_glu, w_branch_a, w_branch_b,
              w_merge, w_out, norm_ffn, w_router_group, b_router_group, w_router_expert,
              b_router_expert, w_exp_gate, w_exp_up, w_exp_down, norm_ple, w_ple_gate,
              w_ple, norm_f):
    for i in range(DEPTH):
        h = _rmsnorm(x, norm_mix[i])
        x = x + _hybrid_mixer(h, w_in[i], ret_gn_w[i], ssm_lam_re[i], ssm_lam_im[i], ssm_log_dt[i],
                              ssm_b_re[i], ssm_b_im[i], ssm_c_re[i], ssm_c_im[i], ssm_d[i], w_glu[i],
                              w_branch_a[i], w_branch_b[i], w_merge[i], w_out[i])
        h = _rmsnorm(x, norm_ffn[i])
        x = x + _hier_moe(h, w_router_group[i], b_router_group[i], w_router_expert[i],
                          b_router_expert[i], w_exp_gate[i], w_exp_up[i], w_exp_down[i])
        h = _rmsnorm(x, norm_ple[i])
        x = x + jax.nn.sigmoid(h @ w_ple_gate[i]) * (p[i] @ w_ple[i])
    return _rmsnorm(x, norm_f)
```

```python
import functools
import math

import jax
import jax.numpy as jnp
from jax import lax
from jax.experimental import pallas as pl
from jax.experimental.pallas import tpu as pltpu

F32 = jnp.float32
BF16 = jnp.bfloat16

EPS = 1e-6
CHUNK = 64
CHUNK_LOG2 = 6
RET_HEADS = 8
HEAD_DIM = 128
ROPE_THETA = 10000.0
SSM_GROUPS = 64
SSM_GROUP_SIZE = 16
SSM_STATE = 64
SSM_PAIRS = SSM_GROUPS // 2
S5_CHUNK = 16
S5_SEGS = 8
N_GROUPS = 4
EXPERTS_PER_GROUP = 8
N_EXPERTS = 32
ROUTER_LANES = 128
MOE_ROWS = 256

VMEM_LIMIT_BYTES = 56 * 1024 * 1024


def _params(sem):
    return pltpu.CompilerParams(dimension_semantics=sem, vmem_limit_bytes=VMEM_LIMIT_BYTES)


def _resident(shape):
    n = len(shape)
    return pl.BlockSpec(shape, lambda *_: (0,) * n, pipeline_mode=pl.Buffered(1))


def _rms(x, w):
    ms = jnp.mean(x * x, axis=-1, keepdims=True)
    return x * lax.rsqrt(ms + EPS) * w


def _dot(a, b):
    return jnp.dot(a, b, preferred_element_type=F32)


def _proj_kernel(x_ref, nw_ref, w_ref, q_ref, k_ref, v_ref, g_ref, u_ref):
    h = _rms(x_ref[...], nw_ref[...]).astype(BF16)
    for c, o_ref in enumerate((q_ref, k_ref, v_ref, g_ref, u_ref)):
        width = o_ref.shape[1]
        o_ref[...] = _dot(h, w_ref[:, c * width:(c + 1) * width])


def _proj(x, norm_w, w_in_bf16, tm=256):
    t, d = x.shape
    width = w_in_bf16.shape[1] // 5
    out = jax.ShapeDtypeStruct((t, width), F32)
    return pl.pallas_call(
        _proj_kernel,
        grid=(t // tm,),
        in_specs=[pl.BlockSpec((tm, d), lambda i: (i, 0)),
                  _resident((1, d)),
                  _resident(w_in_bf16.shape)],
        out_specs=[pl.BlockSpec((tm, width), lambda i: (i, 0))] * 5,
        out_shape=[out] * 5,
        compiler_params=_params(("parallel",)),
        name="proj",
    )(x, norm_w, w_in_bf16)


def _log_gamma(h):
    return math.log1p(-(2.0 ** (-5.0 - h)))


def _retention_kernel(q_ref, k_ref, v_ref, g_ref, cos_ref, sin_ref, gn_ref, o_ref,
                      state_ref, decay_ref, qdec_ref, kdec_ref):
    rows = q_ref.shape[0]

    @pl.when(pl.program_id(0) == 0)
    def _init():
        ti = lax.broadcasted_iota(jnp.int32, (rows, rows), 0)
        si = lax.broadcasted_iota(jnp.int32, (rows, rows), 1)
        visible = lax.shift_right_logical(si, CHUNK_LOG2) <= lax.shift_right_logical(ti, CHUNK_LOG2)
        dist = jnp.abs(ti - si).astype(F32)
        pos = lax.broadcasted_iota(jnp.int32, (rows, HEAD_DIM), 0).astype(F32)
        for h in range(RET_HEADS):
            lg = _log_gamma(h)
            decay_ref[h] = jnp.where(visible, jnp.exp(lg * dist), 0.0)
            qdec_ref[h] = jnp.exp(lg * (pos + 1.0))
            kdec_ref[h] = jnp.exp(lg * (rows - 1.0 - pos))
        state_ref[...] = jnp.zeros_like(state_ref)

    cos = cos_ref[...]
    sin = sin_ref[...]
    scale = HEAD_DIM ** -0.5
    for h in range(RET_HEADS):
        sl = slice(h * HEAD_DIM, (h + 1) * HEAD_DIM)
        qh = q_ref[:, sl]
        kh = k_ref[:, sl]
        vb = v_ref[:, sl].astype(BF16)
        qr = qh * cos + pltpu.roll(qh, HEAD_DIM // 2, 1) * sin
        kr = (kh * cos + pltpu.roll(kh, HEAD_DIM // 2, 1) * sin) * scale
        scores = lax.dot_general(qr.astype(BF16), kr.astype(BF16), (((1,), (1,)), ((), ())),
                                 preferred_element_type=F32) * decay_ref[h]
        state = state_ref[h]
        y = _dot(scores.astype(BF16), vb)
        y = y + _dot((qr * qdec_ref[h]).astype(BF16), state.astype(BF16))
        kv = lax.dot_general((kr * kdec_ref[h]).astype(BF16), vb, (((0,), (0,)), ((), ())),
                             preferred_element_type=F32)
        state_ref[h] = math.exp(_log_gamma(h) * rows) * state + kv
        mu = jnp.mean(y, axis=-1, keepdims=True)
        yc = y - mu
        var = jnp.mean(yc * yc, axis=-1, keepdims=True)
        yn = yc * lax.rsqrt(var + EPS) * gn_ref[:, sl]
        gh = g_ref[:, sl]
        o_ref[:, sl] = (gh * jax.nn.sigmoid(gh) * yn).astype(o_ref.dtype)


def _retention(q, k, v, g, cos_tab, sin_tab, gn_w, rows=256):
    t, width = q.shape
    tok = pl.BlockSpec((rows, width), lambda i: (i, 0))
    tab = pl.BlockSpec((rows, HEAD_DIM), lambda i: (i, 0))
    return pl.pallas_call(
        _retention_kernel,
        grid=(t // rows,),
        in_specs=[tok, tok, tok, tok, tab, tab, _resident((1, width))],
        out_specs=tok,
        out_shape=jax.ShapeDtypeStruct((t, width), BF16),
        scratch_shapes=[pltpu.VMEM((RET_HEADS, HEAD_DIM, HEAD_DIM), F32),
                        pltpu.VMEM((RET_HEADS, rows, rows), F32),
                        pltpu.VMEM((RET_HEADS, rows, HEAD_DIM), F32),
                        pltpu.VMEM((RET_HEADS, rows, HEAD_DIM), F32)],
        compiler_params=_params(("arbitrary",)),
        name="retention",
    )(q, k, v, g, cos_tab, sin_tab, gn_w)


def _cmul(ar, ai, br, bi):
    return ar * br - ai * bi, ar * bi + ai * br


def _s5_kernel(u_ref, m_ref, bst_ref, cout_ref, a_ref, aseg_ref, pw_ref, y_ref,
               sr_ref, si_ref, pr_ref, pi_ref):
    nj = sr_ref.shape[0]
    lanes = sr_ref.shape[2]
    u = u_ref[...]
    y_ref[...] = _dot(u, m_ref[...])
    s_in = _dot(u, bst_ref[...])
    sr_ref[...] = s_in[:, :lanes].reshape(nj, S5_SEGS, lanes)
    si_ref[...] = s_in[:, lanes:].reshape(nj, S5_SEGS, lanes)

    ar = jnp.broadcast_to(a_ref[0:1, :], (S5_SEGS, lanes))
    ai = jnp.broadcast_to(a_ref[1:2, :], (S5_SEGS, lanes))

    def step(j, carry):
        cr, ci = carry
        pr_ref[j] = cr
        pi_ref[j] = ci
        mr, mi = _cmul(ar, ai, cr, ci)
        return mr + sr_ref[j], mi + si_ref[j]

    zero = jnp.zeros((S5_SEGS, lanes), F32)
    er, ei = lax.fori_loop(0, nj, step, (zero, zero))

    gr = jnp.broadcast_to(aseg_ref[0:1, :], (S5_SEGS, lanes))
    gi = jnp.broadcast_to(aseg_ref[1:2, :], (S5_SEGS, lanes))
    seg = lax.broadcasted_iota(jnp.int32, (S5_SEGS, lanes), 0)
    cr, ci = zero, zero
    for _ in range(S5_SEGS - 1):
        tr, ti = _cmul(gr, gi, cr, ci)
        cr = jnp.where(seg >= 1, pltpu.roll(tr + er, 1, 0), 0.0)
        ci = jnp.where(seg >= 1, pltpu.roll(ti + ei, 1, 0), 0.0)

    wr = pw_ref[0]
    wi = pw_ref[1]
    fr, fi = _cmul(wr, wi, cr[None], ci[None])
    pr = (pr_ref[...] + fr).reshape(nj * S5_SEGS, lanes)
    pi = (pi_ref[...] + fi).reshape(nj * S5_SEGS, lanes)
    p = jnp.concatenate([pr, pi], axis=-1).astype(BF16)
    y_ref[...] += _dot(p, cout_ref[...])


def _s5(u_pairs, m_pair, bst_pair, cout_pair, a_pair, aseg_pair, pw_pair):
    npair, rows, width = u_pairs.shape
    nj = rows // S5_SEGS
    lanes = 2 * SSM_STATE

    def pair_block(*shape):
        n = len(shape)
        return pl.BlockSpec((None,) + shape, lambda i: (i,) + (0,) * n)

    return pl.pallas_call(
        _s5_kernel,
        grid=(npair,),
        in_specs=[pair_block(rows, width), pair_block(width, width), pair_block(width, 2 * lanes),
                  pair_block(2 * lanes, width), pair_block(2, lanes), pair_block(2, lanes),
                  pair_block(2, nj, 1, lanes)],
        out_specs=pair_block(rows, width),
        out_shape=jax.ShapeDtypeStruct((npair, rows, width), F32),
        scratch_shapes=[pltpu.VMEM((nj, S5_SEGS, lanes), F32)] * 4,
        compiler_params=_params(("parallel",)),
        name="s5",
    )(u_pairs, m_pair, bst_pair, cout_pair, a_pair, aseg_pair, pw_pair)


def _s5_operators(lam_re, lam_im, log_dt, b_re, b_im, c_re, c_im, nj):
    g, p, cg = b_re.shape
    lc = S5_CHUNK
    lam = lax.complex(jnp.minimum(lam_re, -1e-4), lam_im)
    log_lam_bar = lam * jnp.exp(log_dt)[:, None]
    lam_bar = jnp.exp(log_lam_bar)
    b_bar = ((lam_bar - 1.0) / lam)[..., None] * lax.complex(b_re, b_im)
    c_mat = lax.complex(c_re, c_im)

    def power(n):
        n = jnp.asarray(n, F32)
        return jnp.exp(n.reshape(n.shape + (1, 1)).astype(jnp.complex64) * log_lam_bar)

    lags = jnp.arange(lc)
    taps = jnp.real(jnp.einsum('gap,dgp,gpc->dgac', c_mat, power(lags), b_bar,
                               precision=lax.Precision.HIGHEST))
    diff = lags[None, :] - lags[:, None]
    blocks = jnp.where((diff >= 0)[:, :, None, None, None],
                       taps[jnp.clip(diff, 0, lc - 1)], 0.0)
    m = blocks.transpose(2, 0, 4, 1, 3).reshape(g, lc * cg, lc * cg)

    bst = power(lc - 1 - lags)[:, :, :, None] * b_bar[None]
    bst = bst.transpose(1, 0, 3, 2).reshape(g, lc * cg, p)
    cout = c_mat[None] * power(lags + 1)[:, :, None, :]
    cout = cout.transpose(1, 3, 0, 2).reshape(g, p, lc * cg)

    def pair_lanes(z):
        z = z.reshape((g // 2, 2) + z.shape[1:])
        return jnp.concatenate([z[:, 0], z[:, 1]], axis=-1)

    zeros_m = jnp.zeros_like(m[0::2])
    m_pair = jnp.concatenate([jnp.concatenate([m[0::2], zeros_m], axis=2),
                              jnp.concatenate([zeros_m, m[1::2]], axis=2)], axis=1)
    zb = jnp.zeros((g // 2, lc * cg, p), F32)

    def bst_half(part):
        return jnp.concatenate([jnp.concatenate([part[0::2], zb], axis=2),
                                jnp.concatenate([zb, part[1::2]], axis=2)], axis=1)

    bst_pair = jnp.concatenate([bst_half(jnp.real(bst)), bst_half(jnp.imag(bst))], axis=2)
    zc = jnp.zeros((g // 2, p, lc * cg), F32)

    def cout_half(part):
        return jnp.concatenate([jnp.concatenate([part[0::2], zc], axis=2),
                                jnp.concatenate([zc, part[1::2]], axis=2)], axis=1)

    cout_pair = jnp.concatenate([cout_half(jnp.real(cout)), cout_half(-jnp.imag(cout))], axis=1)

    def planes(z):
        return jnp.stack([jnp.real(z), jnp.imag(z)], axis=1)

    a_pair = planes(pair_lanes(power(jnp.asarray(lc))))
    aseg_pair = planes(pair_lanes(power(jnp.asarray(lc * nj))))
    pw = power(lc * jnp.arange(nj)).transpose(1, 0, 2)
    pw_pair = planes(pair_lanes(pw))[:, :, :, None, :]
    return (m_pair.astype(BF16), bst_pair.astype(BF16), cout_pair.astype(BF16),
            a_pair.astype(F32), aseg_pair.astype(F32), pw_pair.astype(F32))


def _merge_kernel(x_ref, nw_ref, yret_ref, ys_ref, u_ref, d_ref, wglu_ref, wm_ref, wa_ref, wb_ref,
                  o_ref):
    d_model = x_ref.shape[1]
    h = _rms(x_ref[...], nw_ref[...]).astype(BF16)
    y = jax.nn.gelu(ys_ref[...] + d_ref[...] * u_ref[...])
    y_ssm = (y * jax.nn.sigmoid(_dot(y.astype(BF16), wglu_ref[...]))).astype(BF16)
    y_ret = yret_ref[...]
    cw = 512
    for c in range(d_model // cw):
        cs = slice(c * cw, (c + 1) * cw)
        gs = slice(d_model + c * cw, d_model + (c + 1) * cw)
        g0 = jax.nn.sigmoid(_dot(h, wm_ref[:, cs]))
        g1 = jax.nn.sigmoid(_dot(h, wm_ref[:, gs]))
        o_ref[:, cs] = (g0 * _dot(y_ret, wa_ref[:, cs]) + g1 * _dot(y_ssm, wb_ref[:, cs])).astype(o_ref.dtype)


def _merge(x, norm_w, y_ret, y_s5, u, ssm_d, w_glu, w_merge, w_a, w_b, tm=256):
    t, d = x.shape
    width = y_ret.shape[1]
    tok = lambda w: pl.BlockSpec((tm, w), lambda i: (i, 0))
    return pl.pallas_call(
        _merge_kernel,
        grid=(t // tm,),
        in_specs=[tok(d), _resident((1, d)), tok(width), tok(width), tok(width), _resident((1, width)),
                  _resident(w_glu.shape), _resident(w_merge.shape), _resident(w_a.shape),
                  _resident(w_b.shape)],
        out_specs=tok(d),
        out_shape=jax.ShapeDtypeStruct((t, d), BF16),
        compiler_params=_params(("parallel",)),
        name="merge",
    )(x, norm_w, y_ret, y_s5, u, ssm_d, w_glu, w_merge, w_a, w_b)


def _outproj_kernel(m_ref, x_ref, wo_ref, nw_ref, wr_ref, br_ref, x1_ref, lg_ref):
    x1 = x_ref[...] + _dot(m_ref[...], wo_ref[...])
    x1_ref[...] = x1
    h2 = _rms(x1, nw_ref[...])
    lg_ref[...] = jnp.dot(h2, wr_ref[...], preferred_element_type=F32,
                          precision=lax.Precision.HIGHEST) + br_ref[...]


def _outproj(merged, x, w_out, norm_w, w_router, b_router, tm=256):
    t, d = x.shape
    tok = lambda w: pl.BlockSpec((tm, w), lambda i: (i, 0))
    return pl.pallas_call(
        _outproj_kernel,
        grid=(t // tm,),
        in_specs=[tok(d), tok(d), _resident(w_out.shape), _resident((1, d)),
                  _resident(w_router.shape), _resident((1, ROUTER_LANES))],
        out_specs=[tok(d), tok(ROUTER_LANES)],
        out_shape=[jax.ShapeDtypeStruct((t, d), F32), jax.ShapeDtypeStruct((t, ROUTER_LANES), F32)],
        compiler_params=_params(("parallel",)),
        name="outproj",
    )(merged, x, w_out, norm_w, w_router, b_router)


def _moe_kernel(be_ref, nb_ref, nv_ref, tok_ref, dst_ref,
                x_hbm, nw_ref, rw_ref, wg_ref, wu_ref, wd_ref,
                out_hbm,
                xbuf, ybuf, wgu_bf, wd_bf, gsem, ssem):
    b = pl.program_id(0)
    n_used = nb_ref[0]
    rows = xbuf.shape[1]
    f = wg_ref.shape[1]

    def gather_row(blk, slot, r):
        tok = tok_ref[blk * rows + r]
        return pltpu.make_async_copy(x_hbm.at[pl.ds(tok, 1)], xbuf.at[slot, pl.ds(r, 1)], gsem.at[slot])

    def scatter_row(blk, slot, r):
        dst = dst_ref[blk * rows + r]
        return pltpu.make_async_copy(ybuf.at[slot, pl.ds(r, 1)], out_hbm.at[pl.ds(dst, 1)], ssem.at[slot])

    def start_gather(blk, slot):
        def body(r, _):
            gather_row(blk, slot, r).start()
            return 0
        lax.fori_loop(0, rows, body, 0)

    def wait_rows(buf, slot, sem, n):
        tile = 8
        if isinstance(n, int):
            assert n % tile == 0
            pltpu.make_async_copy(buf.at[slot, pl.ds(0, n)], buf.at[slot, pl.ds(0, n)], sem.at[slot]).wait()
            return
        n_tiled = pl.multiple_of((n // tile) * tile, tile)

        @pl.when(n_tiled > 0)
        def _():
            pltpu.make_async_copy(buf.at[slot, pl.ds(0, n_tiled)], buf.at[slot, pl.ds(0, n_tiled)],
                                  sem.at[slot]).wait()

        def body(r, _):
            pltpu.make_async_copy(buf.at[slot, pl.ds(0, 1)], buf.at[slot, pl.ds(0, 1)], sem.at[slot]).wait()
            return 0
        lax.fori_loop(0, n - n_tiled, body, 0)

    @pl.when(b == 0)
    def _prologue():
        start_gather(0, 0)

    @pl.when(b < n_used)
    def _block():
        slot = lax.rem(b, 2)

        @pl.when(b + 1 < n_used)
        def _prefetch():
            start_gather(b + 1, 1 - slot)

        @pl.when(jnp.logical_or(b == 0, be_ref[b] != be_ref[jnp.maximum(b - 1, 0)]))
        def _cast_weights():
            wgu_bf[:, :f] = wg_ref[...].astype(BF16)
            wgu_bf[:, f:] = wu_ref[...].astype(BF16)
            wd_bf[...] = wd_ref[...].astype(BF16)

        wait_rows(xbuf, slot, gsem, rows)
        h = _rms(xbuf[slot], nw_ref[...]).astype(BF16)
        gu = _dot(h, wgu_bf[...])
        gate = gu[:, :f]
        act = (gate * jax.nn.sigmoid(gate) * gu[:, f:]).astype(BF16)
        y = _dot(act, wd_bf[...]) * rw_ref[...]

        @pl.when(b >= 2)
        def _free_slot():
            wait_rows(ybuf, slot, ssem, nv_ref[jnp.maximum(b - 2, 0)])

        ybuf[slot] = y

        def body(r, _):
            scatter_row(b, slot, r).start()
            return 0
        lax.fori_loop(0, nv_ref[b], body, 0)

    @pl.when(b == pl.num_programs(0) - 1)
    def _drain():
        last = n_used - 1
        wait_rows(ybuf, lax.rem(last, 2), ssem, nv_ref[last])

        @pl.when(n_used >= 2)
        def _():
            wait_rows(ybuf, lax.rem(last + 1, 2), ssem, nv_ref[jnp.maximum(last - 1, 0)])


def _moe(block_expert, n_used, n_valid, row_tok, row_dst, x1, norm_w, row_w, w_gate, w_up, w_down):
    t, d = x1.shape
    n_rows = row_tok.shape[0]
    n_blocks = n_rows // MOE_ROWS
    e, _, f = w_gate.shape

    def expert_block(shape):
        return pl.BlockSpec((None,) + shape, lambda b, be, *_: (be[b], 0, 0))

    grid_spec = pltpu.PrefetchScalarGridSpec(
        num_scalar_prefetch=5,
        grid=(n_blocks,),
        in_specs=[pl.BlockSpec(memory_space=pl.ANY),
                  pl.BlockSpec((1, d), lambda b, *_: (0, 0)),
                  pl.BlockSpec((MOE_ROWS, 1), lambda b, *_: (b, 0)),
                  expert_block((d, f)), expert_block((d, f)), expert_block((f, d))],
        out_specs=pl.BlockSpec(memory_space=pl.ANY),
        scratch_shapes=[pltpu.VMEM((2, MOE_ROWS, d), F32),
                        pltpu.VMEM((2, MOE_ROWS, d), F32),
                        pltpu.VMEM((d, 2 * f), BF16),
                        pltpu.VMEM((f, d), BF16),
                        pltpu.SemaphoreType.DMA((2,)),
                        pltpu.SemaphoreType.DMA((2,))],
    )
    return pl.pallas_call(
        _moe_kernel,
        grid_spec=grid_spec,
        out_shape=jax.ShapeDtypeStruct((2 * t, d), F32),
        compiler_params=pltpu.CompilerParams(dimension_semantics=("arbitrary",),
                                             vmem_limit_bytes=VMEM_LIMIT_BYTES,
                                             has_side_effects=True),
        name="moe",
    )(block_expert, n_used, n_valid, row_tok, row_dst, x1, norm_w, row_w, w_gate, w_up, w_down)


def _route(logits):
    t = logits.shape[0]
    g_prob = jax.nn.softmax(logits[:, :N_GROUPS], axis=-1)
    g_w, g_idx = lax.top_k(g_prob, 1)
    e_logits = logits[:, N_GROUPS:N_GROUPS + N_EXPERTS].reshape(t, N_GROUPS, EXPERTS_PER_GROUP)
    e_sel = jnp.take_along_axis(e_logits, g_idx[:, :, None], axis=1)[:, 0]
    top_logit, top_j = lax.top_k(e_sel, 2)
    e_w = jax.nn.softmax(top_logit, axis=-1) * g_w
    expert = (g_idx * EXPERTS_PER_GROUP + top_j).astype(jnp.int32)

    n_assign = 2 * t
    n_rows = n_assign + N_EXPERTS * MOE_ROWS
    n_blocks = n_rows // MOE_ROWS
    flat_e = expert.reshape(-1)
    flat_w = e_w.reshape(-1)
    order = jnp.argsort(flat_e).astype(jnp.int32)
    se = flat_e[order]
    counts = jnp.bincount(flat_e, length=N_EXPERTS).astype(jnp.int32)
    start = jnp.cumsum(counts) - counts
    padded = (counts + MOE_ROWS - 1) // MOE_ROWS * MOE_ROWS
    pad_end = jnp.cumsum(padded)
    pad_start = pad_end - padded
    dest = pad_start[se] + (jnp.arange(n_assign, dtype=jnp.int32) - start[se])
    row_tok = jnp.zeros((n_rows,), jnp.int32).at[dest].set(order // 2)
    row_w = jnp.zeros((n_rows,), F32).at[dest].set(flat_w[order])
    row_dst = jnp.zeros((n_rows,), jnp.int32).at[dest].set((order % 2) * t + order // 2)
    block_start = jnp.arange(n_blocks, dtype=jnp.int32) * MOE_ROWS
    block_expert = jnp.minimum(jnp.searchsorted(pad_end, block_start, side='right'),
                               N_EXPERTS - 1).astype(jnp.int32)
    n_used = (pad_end[-1] // MOE_ROWS).astype(jnp.int32).reshape(1)
    block_expert = jnp.where(block_start < pad_end[-1], block_expert, block_expert[jnp.maximum(n_used[0] - 1, 0)])
    n_valid = jnp.clip(counts[block_expert] - (block_start - pad_start[block_expert]), 0, MOE_ROWS)
    n_valid = jnp.where(block_start < pad_end[-1], n_valid, 0).astype(jnp.int32)
    return block_expert, n_used, n_valid, row_tok, row_dst, row_w[:, None]


def _final_kernel(x1_ref, r0_ref, r1_ref, p_ref, nple_ref, wg_ref, wp_ref, nf_ref, o_ref):
    x2 = x1_ref[...] + (r0_ref[...] + r1_ref[...])
    h3 = _rms(x2, nple_ref[...]).astype(BF16)
    gate = jax.nn.sigmoid(_dot(h3, wg_ref[...]))
    x3 = x2 + gate * _dot(p_ref[...].astype(BF16), wp_ref[...])
    o_ref[...] = _rms(x3, nf_ref[...])


def _final(x1, moe_rows, p, norm_ple, w_ple_gate, w_ple, norm_f, tm=256):
    t, d = x1.shape
    nt = t // tm
    tok = lambda w: pl.BlockSpec((tm, w), lambda i: (i, 0))
    return pl.pallas_call(
        _final_kernel,
        grid=(nt,),
        in_specs=[tok(d), tok(d), pl.BlockSpec((tm, d), lambda i: (i + nt, 0)), tok(p.shape[1]),
                  _resident((1, d)), _resident(w_ple_gate.shape), _resident(w_ple.shape),
                  _resident((1, d))],
        out_specs=tok(d),
        out_shape=jax.ShapeDtypeStruct((t, d), F32),
        compiler_params=_params(("parallel",)),
        name="final",
    )(x1, moe_rows, moe_rows, p, norm_ple, w_ple_gate, w_ple, norm_f)


def _rope_tables(t):
    half = HEAD_DIM // 2
    inv_freq = ROPE_THETA ** (-jnp.arange(half, dtype=F32) / half)
    ang = jnp.arange(t, dtype=F32)[:, None] * inv_freq[None, :]
    cos, sin = jnp.cos(ang), jnp.sin(ang)
    return jnp.concatenate([cos, cos], axis=-1), jnp.concatenate([-sin, sin], axis=-1)


def _layer(x, p, norm_mix, w_in, ret_gn_w, lam_re, lam_im, log_dt, b_re, b_im, c_re, c_im, ssm_d,
           w_glu, w_branch_a, w_branch_b, w_merge, w_out, norm_ffn, w_rg, b_rg, w_re, b_re_router,
           w_exp_gate, w_exp_up, w_exp_down, norm_ple, w_ple_gate, w_ple, norm_f):
    t, d = x.shape
    row = lambda v: v.reshape(1, -1).astype(F32)

    q, k, v, g, u = _proj(x, row(norm_mix), w_in.astype(BF16))
    cos_tab, sin_tab = _rope_tables(t)
    y_ret = _retention(q, k, v, g, cos_tab, sin_tab, row(ret_gn_w))

    n_chunks = t // S5_CHUNK
    nj = n_chunks // S5_SEGS
    ops = _s5_operators(lam_re, lam_im, log_dt, b_re, b_im, c_re, c_im, nj)
    u_pairs = (u.astype(BF16).reshape(S5_SEGS, nj, S5_CHUNK, SSM_PAIRS, 2, SSM_GROUP_SIZE)
               .transpose(3, 1, 0, 4, 2, 5).reshape(SSM_PAIRS, n_chunks, 2 * S5_CHUNK * SSM_GROUP_SIZE))
    y_pairs = _s5(u_pairs, *ops)
    y_s5 = (y_pairs.reshape(SSM_PAIRS, nj, S5_SEGS, 2, S5_CHUNK, SSM_GROUP_SIZE)
            .transpose(2, 1, 4, 0, 3, 5).reshape(t, SSM_GROUPS * SSM_GROUP_SIZE))

    merged = _merge(x, row(norm_mix), y_ret, y_s5, u, row(ssm_d), w_glu.astype(BF16),
                    w_merge.astype(BF16), w_branch_a.astype(BF16), w_branch_b.astype(BF16))

    pad = ROUTER_LANES - N_GROUPS - N_EXPERTS
    w_router = jnp.concatenate([w_rg, w_re, jnp.zeros((d, pad), F32)], axis=1).astype(F32)
    b_router = jnp.concatenate([b_rg, b_re_router, jnp.zeros((pad,), F32)]).reshape(1, ROUTER_LANES).astype(F32)
    x1, logits = _outproj(merged, x, w_out.astype(BF16), row(norm_ffn), w_router, b_router)

    block_expert, n_used, n_valid, row_tok, row_dst, row_w = _route(logits)
    moe_rows = _moe(block_expert, n_used, n_valid, row_tok, row_dst, x1, row(norm_ffn), row_w,
                    w_exp_gate, w_exp_up, w_exp_down)

    return _final(x1, moe_rows, p, row(norm_ple), w_ple_gate.astype(BF16), w_ple.astype(BF16), row(norm_f))


def kernel(x, p, norm_mix, w_in, ret_gn_w, ssm_lam_re, ssm_lam_im, ssm_log_dt, ssm_b_re, ssm_b_im, ssm_c_re, ssm_c_im, ssm_d, w_glu, w_branch_a, w_branch_b, w_merge, w_out, norm_ffn, w_router_group, b_router_group, w_router_expert, b_router_expert, w_exp_gate, w_exp_up, w_exp_down, norm_ple, w_ple_gate, w_ple, norm_f):
    depth, bsz, seq, _ = p.shape
    assert depth == 1 and bsz == 1, "single layer, single sequence"
    out = _layer(x[0], p[0, 0], norm_mix[0], w_in[0], ret_gn_w[0], ssm_lam_re[0], ssm_lam_im[0],
                 ssm_log_dt[0], ssm_b_re[0], ssm_b_im[0], ssm_c_re[0], ssm_c_im[0], ssm_d[0], w_glu[0],
                 w_branch_a[0], w_branch_b[0], w_merge[0], w_out[0], norm_ffn[0], w_router_group[0],
                 b_router_group[0], w_router_expert[0], b_router_expert[0], w_exp_gate[0], w_exp_up[0],
                 w_exp_down[0], norm_ple[0], w_ple_gate[0], w_ple[0], norm_f)
    return out[None]
```

```python
import math

import jax
import jax.numpy as jnp
from jax import lax
from jax.experimental import pallas as pl
from jax.experimental.pallas import tpu as pltpu

F32 = jnp.float32
BF16 = jnp.bfloat16
I32 = jnp.int32

EPS = 1e-6
LANES = 128
SUBLANES = 8
CHUNK_LOG2 = 6
RET_HEADS = 8
HEAD_DIM = 128
ROPE_THETA = 10000.0
SSM_GROUPS = 64
SSM_GROUP_SIZE = 16
SSM_STATE = 64
S5_CHUNK = 16
S5_TILE_GROUPS = LANES // SSM_GROUP_SIZE
S5_TILES = SSM_GROUPS // S5_TILE_GROUPS
S5_HALF = S5_TILE_GROUPS * SSM_STATE
N_GROUPS = 4
EXPERTS_PER_GROUP = 8
N_EXPERTS = 32
ROUTER_LANES = LANES
MOE_ROWS = 256
DMA_UNROLL = 8

VMEM_LIMIT_BYTES = 56 * 1024 * 1024


def _params(sem):
    return pltpu.CompilerParams(dimension_semantics=sem, vmem_limit_bytes=VMEM_LIMIT_BYTES)


def _resident(shape):
    n = len(shape)
    return pl.BlockSpec(shape, lambda *_: (0,) * n, pipeline_mode=pl.Buffered(1))


def _rms(x, w):
    ms = jnp.mean(x * x, axis=-1, keepdims=True)
    return x * lax.rsqrt(ms + EPS) * w


def _dot(a, b):
    return jnp.dot(a, b, preferred_element_type=F32)


def _proj_kernel(x_ref, nw_ref, w_ref, q_ref, k_ref, v_ref, g_ref, u_ref):
    h = _rms(x_ref[...], nw_ref[...]).astype(BF16)
    for c, o_ref in enumerate((q_ref, k_ref, v_ref, g_ref, u_ref)):
        width = o_ref.shape[1]
        o_ref[...] = _dot(h, w_ref[:, c * width:(c + 1) * width])


def _proj(x, norm_w, w_in_bf16, tm=256):
    t, d = x.shape
    width = w_in_bf16.shape[1] // 5
    out = jax.ShapeDtypeStruct((t, width), F32)
    return pl.pallas_call(
        _proj_kernel,
        grid=(t // tm,),
        in_specs=[pl.BlockSpec((tm, d), lambda i: (i, 0)),
                  _resident((1, d)),
                  _resident(w_in_bf16.shape)],
        out_specs=[pl.BlockSpec((tm, width), lambda i: (i, 0))] * 5,
        out_shape=[out] * 5,
        compiler_params=_params(("parallel",)),
        name="proj",
    )(x, norm_w, w_in_bf16)


def _log_gamma(h):
    return math.log1p(-(2.0 ** (-5.0 - h)))


def _retention_kernel(q_ref, k_ref, v_ref, g_ref, cos_ref, sin_ref, gn_ref, o_ref,
                      state_ref, decay_ref, qdec_ref, kdec_ref):
    rows = q_ref.shape[0]

    @pl.when(pl.program_id(0) == 0)
    def _init():
        ti = lax.broadcasted_iota(I32, (rows, rows), 0)
        si = lax.broadcasted_iota(I32, (rows, rows), 1)
        visible = lax.shift_right_logical(si, CHUNK_LOG2) <= lax.shift_right_logical(ti, CHUNK_LOG2)
        dist = jnp.abs(ti - si).astype(F32)
        pos = lax.broadcasted_iota(I32, (rows, HEAD_DIM), 0).astype(F32)
        for h in range(RET_HEADS):
            lg = _log_gamma(h)
            decay_ref[h] = jnp.where(visible, jnp.exp(lg * dist), 0.0)
            qdec_ref[h] = jnp.exp(lg * (pos + 1.0))
            kdec_ref[h] = jnp.exp(lg * (rows - 1.0 - pos))
        state_ref[...] = jnp.zeros_like(state_ref)

    cos = cos_ref[...]
    sin = sin_ref[...]
    scale = HEAD_DIM ** -0.5
    for h in range(RET_HEADS):
        sl = slice(h * HEAD_DIM, (h + 1) * HEAD_DIM)
        qh = q_ref[:, sl]
        kh = k_ref[:, sl]
        vb = v_ref[:, sl].astype(BF16)
        qr = qh * cos + pltpu.roll(qh, HEAD_DIM // 2, 1) * sin
        kr = (kh * cos + pltpu.roll(kh, HEAD_DIM // 2, 1) * sin) * scale
        scores = lax.dot_general(qr.astype(BF16), kr.astype(BF16), (((1,), (1,)), ((), ())),
                                 preferred_element_type=F32) * decay_ref[h]
        state = state_ref[h]
        y = _dot(scores.astype(BF16), vb)
        y = y + _dot((qr * qdec_ref[h]).astype(BF16), state.astype(BF16))
        kv = lax.dot_general((kr * kdec_ref[h]).astype(BF16), vb, (((0,), (0,)), ((), ())),
                             preferred_element_type=F32)
        state_ref[h] = math.exp(_log_gamma(h) * rows) * state + kv
        mu = jnp.mean(y, axis=-1, keepdims=True)
        yc = y - mu
        var = jnp.mean(yc * yc, axis=-1, keepdims=True)
        yn = yc * lax.rsqrt(var + EPS) * gn_ref[:, sl]
        gh = g_ref[:, sl]
        o_ref[:, sl] = (gh * jax.nn.sigmoid(gh) * yn).astype(o_ref.dtype)


def _retention(q, k, v, g, cos_tab, sin_tab, gn_w, rows=256):
    t, width = q.shape
    tok = pl.BlockSpec((rows, width), lambda i: (i, 0))
    tab = pl.BlockSpec((rows, HEAD_DIM), lambda i: (i, 0))
    return pl.pallas_call(
        _retention_kernel,
        grid=(t // rows,),
        in_specs=[tok, tok, tok, tok, tab, tab, _resident((1, width))],
        out_specs=tok,
        out_shape=jax.ShapeDtypeStruct((t, width), BF16),
        scratch_shapes=[pltpu.VMEM((RET_HEADS, HEAD_DIM, HEAD_DIM), F32),
                        pltpu.VMEM((RET_HEADS, rows, rows), F32),
                        pltpu.VMEM((RET_HEADS, rows, HEAD_DIM), F32),
                        pltpu.VMEM((RET_HEADS, rows, HEAD_DIM), F32)],
        compiler_params=_params(("arbitrary",)),
        name="retention",
    )(q, k, v, g, cos_tab, sin_tab, gn_w)


def _cmul(ar, ai, br, bi):
    return ar * br - ai * bi, ar * bi + ai * br


def _s5_kernel(u_ref, m_ref, bst_ref, cout_ref, pw_ref, y_ref, a_ref, pr_ref, pi_ref, carry_ref):
    tq = u_ref.shape[0]
    nn = tq // S5_CHUNK
    nb = nn // SUBLANES
    half = S5_HALF

    @pl.when(pl.program_id(1) == 0)
    def _():
        carry_ref[...] = jnp.zeros_like(carry_ref)

    for l in range(S5_CHUNK):
        a_ref[:, l * LANES:(l + 1) * LANES] = u_ref[pl.ds(l, nn, stride=S5_CHUNK), :].astype(BF16)
    a = a_ref[...]
    yc = _dot(a, m_ref[...])
    s_in = _dot(a, bst_ref[...])
    sr, si = s_in[:, :half], s_in[:, half:]

    row = lax.broadcasted_iota(I32, (nn, half), 0)
    zr = jnp.where(row == 0, carry_ref[0:1, :], pltpu.roll(sr, 1, 0))
    zi = jnp.where(row == 0, carry_ref[1:2, :], pltpu.roll(si, 1, 0))
    in_block = jnp.bitwise_and(row, SUBLANES - 1)
    for d in (1, 2, 4):
        mr, mi = _cmul(pw_ref[0, d - 1:d, :], pw_ref[1, d - 1:d, :],
                       pltpu.roll(zr, d, 0), pltpu.roll(zi, d, 0))
        keep = in_block >= d
        zr = zr + jnp.where(keep, mr, 0.0)
        zi = zi + jnp.where(keep, mi, 0.0)
    pr_ref[...] = zr.reshape(nb, SUBLANES, half)
    pi_ref[...] = zi.reshape(nb, SUBLANES, half)

    wr, wi = pw_ref[0], pw_ref[1]

    def block_step(r, c):
        fr, fi = _cmul(wr, wi, c[0], c[1])
        nr = pr_ref[r] + fr
        ni = pi_ref[r] + fi
        pr_ref[r] = nr
        pi_ref[r] = ni
        last = slice(SUBLANES - 1, SUBLANES)
        return (jnp.broadcast_to(nr[last, :], (SUBLANES, half)),
                jnp.broadcast_to(ni[last, :], (SUBLANES, half)))

    zero = jnp.zeros((SUBLANES, half), F32)
    lr, li = lax.fori_loop(0, nb, block_step, (zero, zero))
    er, ei = _cmul(wr[0:1, :], wi[0:1, :], lr[0:1, :], li[0:1, :])
    carry_ref[0:1, :] = er + sr[nn - 1:nn, :]
    carry_ref[1:2, :] = ei + si[nn - 1:nn, :]

    p = jnp.concatenate([pr_ref[...].reshape(nn, half), pi_ref[...].reshape(nn, half)], axis=-1)
    yc = yc + _dot(p.astype(BF16), cout_ref[...])
    for l in range(S5_CHUNK):
        y_ref[pl.ds(l, nn, stride=S5_CHUNK), :] = yc[:, l * LANES:(l + 1) * LANES]


def _s5(u, m_op, bst_op, cout_op, pw_op, tq):
    t, width = u.shape
    nn = tq // S5_CHUNK
    kdim = S5_CHUNK * LANES

    def tile_block(*shape):
        n = len(shape)
        return pl.BlockSpec((None,) + shape, lambda o, i: (o,) + (0,) * n)

    tok = pl.BlockSpec((tq, LANES), lambda o, i: (i, o))
    return pl.pallas_call(
        _s5_kernel,
        grid=(width // LANES, t // tq),
        in_specs=[tok, tile_block(kdim, kdim), tile_block(kdim, 2 * S5_HALF),
                  tile_block(2 * S5_HALF, kdim), tile_block(2, SUBLANES, S5_HALF)],
        out_specs=tok,
        out_shape=jax.ShapeDtypeStruct((t, width), F32),
        scratch_shapes=[pltpu.VMEM((nn, kdim), BF16),
                        pltpu.VMEM((nn // SUBLANES, SUBLANES, S5_HALF), F32),
                        pltpu.VMEM((nn // SUBLANES, SUBLANES, S5_HALF), F32),
                        pltpu.VMEM((2, S5_HALF), F32)],
        compiler_params=_params(("parallel", "arbitrary")),
        name="s5",
    )(u, m_op, bst_op, cout_op, pw_op)


def _s5_operators(lam_re, lam_im, log_dt, b_re, b_im, c_re, c_im):
    g, p, cg = b_re.shape
    lc, nt, tg = S5_CHUNK, S5_TILES, S5_TILE_GROUPS
    lam = lax.complex(jnp.minimum(lam_re, -1e-4), lam_im)
    log_lam_bar = lam * jnp.exp(log_dt)[:, None]
    lam_bar = jnp.exp(log_lam_bar)
    b_bar = ((lam_bar - 1.0) / lam)[..., None] * lax.complex(b_re, b_im)
    c_mat = lax.complex(c_re, c_im)

    def power(n):
        n = jnp.asarray(n, F32)
        return jnp.exp(n[:, None, None].astype(jnp.complex64) * log_lam_bar)

    lags = jnp.arange(lc)
    eye = jnp.eye(tg, dtype=F32)
    taps = jnp.real(jnp.einsum('gap,dgp,gpc->dgac', c_mat, power(lags), b_bar,
                               precision=lax.Precision.HIGHEST))
    diff = lags[None, :] - lags[:, None]
    blocks = jnp.where((diff >= 0)[:, :, None, None, None],
                       taps[jnp.clip(diff, 0, lc - 1)], 0.0)
    blocks = blocks.reshape(lc, lc, nt, tg, cg, cg).transpose(2, 0, 3, 5, 1, 4)
    m_op = (blocks[:, :, :, :, :, None, :] * eye[None, None, :, None, None, :, None]).astype(BF16)
    m_op = m_op.reshape(nt, lc * tg * cg, lc * tg * cg)

    bst = power(lc - 1 - lags)[:, :, :, None] * b_bar[None]
    bst = bst.reshape(lc, nt, tg, p, cg).transpose(1, 0, 2, 4, 3)

    def bst_plane(z):
        z = z[:, :, :, :, None, :] * eye[None, None, :, None, :, None]
        return z.astype(BF16).reshape(nt, lc * tg * cg, tg * p)

    bst_op = jnp.concatenate([bst_plane(jnp.real(bst)), bst_plane(jnp.imag(bst))], axis=2)

    cout = c_mat[None] * power(lags + 1)[:, :, None, :]
    cout = cout.reshape(lc, nt, tg, cg, p).transpose(1, 2, 4, 0, 3)

    def cout_plane(z):
        z = z[:, :, :, :, None, :] * eye[None, :, None, None, :, None]
        return z.astype(BF16).reshape(nt, tg * p, lc * tg * cg)

    cout_op = jnp.concatenate([cout_plane(jnp.real(cout)), cout_plane(-jnp.imag(cout))], axis=1)

    pw = power(lc * (1 + jnp.arange(SUBLANES)))
    pw = pw.reshape(SUBLANES, nt, tg * p).transpose(1, 0, 2)
    pw_op = jnp.stack([jnp.real(pw), jnp.imag(pw)], axis=1).astype(F32)
    return m_op, bst_op, cout_op, pw_op


def _merge_kernel(x_ref, nw_ref, yret_ref, ys_ref, u_ref, d_ref, wglu_ref, wm_ref, wa_ref, wb_ref,
                  o_ref):
    d_model = x_ref.shape[1]
    h = _rms(x_ref[...], nw_ref[...]).astype(BF16)
    y = jax.nn.gelu(ys_ref[...] + d_ref[...] * u_ref[...])
    y_ssm = (y * jax.nn.sigmoid(_dot(y.astype(BF16), wglu_ref[...]))).astype(BF16)
    y_ret = yret_ref[...]
    cw = 512
    for c in range(d_model // cw):
        cs = slice(c * cw, (c + 1) * cw)
        gs = slice(d_model + c * cw, d_model + (c + 1) * cw)
        g0 = jax.nn.sigmoid(_dot(h, wm_ref[:, cs]))
        g1 = jax.nn.sigmoid(_dot(h, wm_ref[:, gs]))
        o_ref[:, cs] = (g0 * _dot(y_ret, wa_ref[:, cs]) + g1 * _dot(y_ssm, wb_ref[:, cs])).astype(o_ref.dtype)


def _merge(x, norm_w, y_ret, y_s5, u, ssm_d, w_glu, w_merge, w_a, w_b, tm=256):
    t, d = x.shape
    width = y_ret.shape[1]
    tok = lambda w: pl.BlockSpec((tm, w), lambda i: (i, 0))
    return pl.pallas_call(
        _merge_kernel,
        grid=(t // tm,),
        in_specs=[tok(d), _resident((1, d)), tok(width), tok(width), tok(width), _resident((1, width)),
                  _resident(w_glu.shape), _resident(w_merge.shape), _resident(w_a.shape),
                  _resident(w_b.shape)],
        out_specs=tok(d),
        out_shape=jax.ShapeDtypeStruct((t, d), BF16),
        compiler_params=_params(("parallel",)),
        name="merge",
    )(x, norm_w, y_ret, y_s5, u, ssm_d, w_glu, w_merge, w_a, w_b)


def _first_argmax(vals, lane):
    top = jnp.max(vals, axis=-1, keepdims=True)
    idx = jnp.min(jnp.where(vals == top, lane, ROUTER_LANES), axis=-1, keepdims=True)
    return top, idx


def _outproj_kernel(m_ref, x_ref, wo_ref, nw_ref, wr_ref, br_ref, x1_ref, ri_ref, rw_ref):
    x1 = x_ref[...] + _dot(m_ref[...], wo_ref[...])
    x1_ref[...] = x1
    h2 = _rms(x1, nw_ref[...])
    h_hi = h2.astype(BF16)
    h_lo = (h2 - h_hi.astype(F32)).astype(BF16)
    part = _dot(h_hi, wr_ref[...])
    logits = (part[:, :ROUTER_LANES] + part[:, ROUTER_LANES:]
              + _dot(h_lo, wr_ref[:, :ROUTER_LANES]) + br_ref[...])

    lane = lax.broadcasted_iota(I32, logits.shape, 1)
    neg = -jnp.inf
    is_group = lane < N_GROUPS
    g_top, g_idx = _first_argmax(jnp.where(is_group, logits, neg), lane)
    g_w = 1.0 / jnp.sum(jnp.where(is_group, jnp.exp(logits - g_top), 0.0), axis=-1, keepdims=True)
    first = N_GROUPS + EXPERTS_PER_GROUP * g_idx
    e_logits = jnp.where(lane >= first, jnp.where(lane < first + EXPERTS_PER_GROUP, logits, neg), neg)
    t1, i1 = _first_argmax(e_logits, lane)
    t2, i2 = _first_argmax(jnp.where(lane == i1, neg, e_logits), lane)
    ratio = jnp.exp(t2 - t1)
    w1 = g_w / (1.0 + ratio)
    w2 = w1 * ratio
    ri_ref[...] = jnp.where(lane == 0, i1 - N_GROUPS, jnp.where(lane == 1, i2 - N_GROUPS, 0))
    rw_ref[...] = jnp.where(lane == 0, w1, jnp.where(lane == 1, w2, 0.0))


def _outproj(merged, x, w_out, norm_w, w_router, b_router, tm=256):
    t, d = x.shape
    tok = lambda w: pl.BlockSpec((tm, w), lambda i: (i, 0))
    return pl.pallas_call(
        _outproj_kernel,
        grid=(t // tm,),
        in_specs=[tok(d), tok(d), _resident(w_out.shape), _resident((1, d)),
                  _resident(w_router.shape), _resident((1, ROUTER_LANES))],
        out_specs=[tok(d), tok(ROUTER_LANES), tok(ROUTER_LANES)],
        out_shape=[jax.ShapeDtypeStruct((t, d), F32),
                   jax.ShapeDtypeStruct((t, ROUTER_LANES), I32),
                   jax.ShapeDtypeStruct((t, ROUTER_LANES), F32)],
        compiler_params=_params(("parallel",)),
        name="outproj",
    )(merged, x, w_out, norm_w, w_router, b_router)


def _moe_kernel(be_ref, nb_ref, nv_ref, tok_ref, dst_ref,
                x_hbm, nw_ref, wg_ref, wu_ref, wd_ref,
                out_hbm,
                xbuf, ybuf, wgu_bf, wd_bf, gsem, ssem):
    b = pl.program_id(0)
    n_used = nb_ref[0]
    rows = xbuf.shape[1]
    f = wg_ref.shape[1]

    def gather_row(blk, slot, r):
        tok = tok_ref[blk * rows + r]
        return pltpu.make_async_copy(x_hbm.at[pl.ds(tok, 1)], xbuf.at[slot, pl.ds(r, 1)], gsem.at[slot])

    def scatter_row(blk, slot, r):
        dst = dst_ref[blk * rows + r]
        return pltpu.make_async_copy(ybuf.at[slot, pl.ds(r, 1)], out_hbm.at[pl.ds(dst, 1)], ssem.at[slot])

    def start_rows(make_copy, blk, slot, n):
        def group(i, _):
            for j in range(DMA_UNROLL):
                make_copy(blk, slot, i * DMA_UNROLL + j).start()
            return 0
        n_groups = n // DMA_UNROLL
        lax.fori_loop(0, n_groups, group, 0)
        if not isinstance(n, int):
            def single(r, _):
                make_copy(blk, slot, r).start()
                return 0
            lax.fori_loop(n_groups * DMA_UNROLL, n, single, 0)

    def wait_rows(buf, slot, sem, n):
        if isinstance(n, int):
            pltpu.make_async_copy(buf.at[slot, pl.ds(0, n)], buf.at[slot, pl.ds(0, n)], sem.at[slot]).wait()
            return
        n_tiled = pl.multiple_of((n // SUBLANES) * SUBLANES, SUBLANES)

        @pl.when(n_tiled > 0)
        def _():
            pltpu.make_async_copy(buf.at[slot, pl.ds(0, n_tiled)], buf.at[slot, pl.ds(0, n_tiled)],
                                  sem.at[slot]).wait()

        def body(r, _):
            pltpu.make_async_copy(buf.at[slot, pl.ds(0, 1)], buf.at[slot, pl.ds(0, 1)], sem.at[slot]).wait()
            return 0
        lax.fori_loop(0, n - n_tiled, body, 0)

    @pl.when(b == 0)
    def _prologue():
        start_rows(gather_row, 0, 0, rows)

    @pl.when(b < n_used)
    def _block():
        slot = lax.rem(b, 2)

        @pl.when(b + 1 < n_used)
        def _prefetch():
            start_rows(gather_row, b + 1, 1 - slot, rows)

        @pl.when(jnp.logical_or(b == 0, be_ref[b] != be_ref[jnp.maximum(b - 1, 0)]))
        def _cast_weights():
            wgu_bf[:, :f] = wg_ref[...].astype(BF16)
            wgu_bf[:, f:] = wu_ref[...].astype(BF16)
            wd_bf[...] = wd_ref[...].astype(BF16)

        wait_rows(xbuf, slot, gsem, rows)
        h = _rms(xbuf[slot], nw_ref[...]).astype(BF16)
        gu = _dot(h, wgu_bf[...])
        gate = gu[:, :f]
        act = (gate * jax.nn.sigmoid(gate) * gu[:, f:]).astype(BF16)
        y = _dot(act, wd_bf[...])

        @pl.when(b >= 2)
        def _free_slot():
            wait_rows(ybuf, slot, ssem, nv_ref[jnp.maximum(b - 2, 0)])

        ybuf[slot] = y
        start_rows(scatter_row, b, slot, nv_ref[b])

    @pl.when(b == pl.num_programs(0) - 1)
    def _drain():
        last = n_used - 1
        wait_rows(ybuf, lax.rem(last, 2), ssem, nv_ref[last])

        @pl.when(n_used >= 2)
        def _():
            wait_rows(ybuf, lax.rem(last + 1, 2), ssem, nv_ref[jnp.maximum(last - 1, 0)])


def _moe(block_expert, n_used, n_valid, row_tok, row_dst, x1, norm_w, w_gate, w_up, w_down):
    t, d = x1.shape
    n_blocks = row_tok.shape[0] // MOE_ROWS
    _, _, f = w_gate.shape

    def expert_block(shape):
        return pl.BlockSpec((None,) + shape, lambda b, be, *_: (be[b], 0, 0))

    grid_spec = pltpu.PrefetchScalarGridSpec(
        num_scalar_prefetch=5,
        grid=(n_blocks,),
        in_specs=[pl.BlockSpec(memory_space=pl.ANY),
                  pl.BlockSpec((1, d), lambda b, *_: (0, 0)),
                  expert_block((d, f)), expert_block((d, f)), expert_block((f, d))],
        out_specs=pl.BlockSpec(memory_space=pl.ANY),
        scratch_shapes=[pltpu.VMEM((2, MOE_ROWS, d), F32),
                        pltpu.VMEM((2, MOE_ROWS, d), F32),
                        pltpu.VMEM((d, 2 * f), BF16),
                        pltpu.VMEM((f, d), BF16),
                        pltpu.SemaphoreType.DMA((2,)),
                        pltpu.SemaphoreType.DMA((2,))],
    )
    return pl.pallas_call(
        _moe_kernel,
        grid_spec=grid_spec,
        out_shape=jax.ShapeDtypeStruct((2 * t, d), F32),
        compiler_params=pltpu.CompilerParams(dimension_semantics=("arbitrary",),
                                             vmem_limit_bytes=VMEM_LIMIT_BYTES,
                                             has_side_effects=True),
        name="moe",
    )(block_expert, n_used, n_valid, row_tok, row_dst, x1, norm_w, w_gate, w_up, w_down)


def _row_layout(expert):
    t = expert.shape[0]
    n_assign = 2 * t
    n_blocks = n_assign // MOE_ROWS + N_EXPERTS
    flat_e = expert.reshape(-1)
    order = jnp.argsort(flat_e).astype(I32)
    counts = jnp.sum(flat_e[:, None] == jnp.arange(N_EXPERTS, dtype=I32)[None, :], axis=0, dtype=I32)
    start = jnp.cumsum(counts) - counts
    padded = (counts + MOE_ROWS - 1) // MOE_ROWS * MOE_ROWS
    pad_end = jnp.cumsum(padded)
    pad_start = pad_end - padded
    block_start = jnp.arange(n_blocks, dtype=I32) * MOE_ROWS
    used = block_start < pad_end[-1]
    n_used = (pad_end[-1] // MOE_ROWS).astype(I32)
    block_expert = jnp.minimum(jnp.sum(pad_end[None, :] <= block_start[:, None], axis=1, dtype=I32),
                               N_EXPERTS - 1)
    block_expert = jnp.where(used, block_expert, block_expert[jnp.maximum(n_used - 1, 0)])
    rank0 = block_start - pad_start[block_expert]
    n_valid = jnp.where(used, jnp.clip(counts[block_expert] - rank0, 0, MOE_ROWS), 0).astype(I32)
    within = jnp.arange(MOE_ROWS, dtype=I32)[None, :]
    valid = within < n_valid[:, None]
    src = order[jnp.clip(start[block_expert][:, None] + rank0[:, None] + within, 0, n_assign - 1)]
    tok = src // 2
    row_tok = jnp.where(valid, tok, 0).reshape(-1)
    row_dst = jnp.where(valid, (src % 2) * t + tok, 0).reshape(-1)
    return block_expert, n_used.reshape(1), n_valid, row_tok, row_dst


def _final_kernel(x1_ref, r0_ref, r1_ref, rw_ref, p_ref, nple_ref, wg_ref, wp_ref, nf_ref, o_ref):
    x2 = x1_ref[...] + (rw_ref[:, 0:1] * r0_ref[...] + rw_ref[:, 1:2] * r1_ref[...])
    h3 = _rms(x2, nple_ref[...]).astype(BF16)
    gate = jax.nn.sigmoid(_dot(h3, wg_ref[...]))
    x3 = x2 + gate * _dot(p_ref[...].astype(BF16), wp_ref[...])
    o_ref[...] = _rms(x3, nf_ref[...])


def _final(x1, moe_rows, route_w, p, norm_ple, w_ple_gate, w_ple, norm_f, tm=256):
    t, d = x1.shape
    nt = t // tm
    tok = lambda w: pl.BlockSpec((tm, w), lambda i: (i, 0))
    return pl.pallas_call(
        _final_kernel,
        grid=(nt,),
        in_specs=[tok(d), tok(d), pl.BlockSpec((tm, d), lambda i: (i + nt, 0)), tok(ROUTER_LANES),
                  tok(p.shape[1]), _resident((1, d)), _resident(w_ple_gate.shape),
                  _resident(w_ple.shape), _resident((1, d))],
        out_specs=tok(d),
        out_shape=jax.ShapeDtypeStruct((t, d), F32),
        compiler_params=_params(("parallel",)),
        name="final",
    )(x1, moe_rows, moe_rows, route_w, p, norm_ple, w_ple_gate, w_ple, norm_f)


def _rope_tables(t):
    half = HEAD_DIM // 2
    inv_freq = ROPE_THETA ** (-jnp.arange(half, dtype=F32) / half)
    ang = jnp.arange(t, dtype=F32)[:, None] * inv_freq[None, :]
    cos, sin = jnp.cos(ang), jnp.sin(ang)
    return jnp.concatenate([cos, cos], axis=-1), jnp.concatenate([-sin, sin], axis=-1)


def _layer(x, p, norm_mix, w_in, ret_gn_w, lam_re, lam_im, log_dt, b_re, b_im, c_re, c_im, ssm_d,
           w_glu, w_branch_a, w_branch_b, w_merge, w_out, norm_ffn, w_rg, b_rg, w_re, b_re_router,
           w_exp_gate, w_exp_up, w_exp_down, norm_ple, w_ple_gate, w_ple, norm_f):
    t, d = x.shape
    row = lambda v: v.reshape(1, -1).astype(F32)

    q, k, v, g, u = _proj(x, row(norm_mix), w_in.astype(BF16))
    cos_tab, sin_tab = _rope_tables(t)
    y_ret = _retention(q, k, v, g, cos_tab, sin_tab, row(ret_gn_w))

    ops = _s5_operators(lam_re, lam_im, log_dt, b_re, b_im, c_re, c_im)
    y_s5 = _s5(u, *ops, tq=min(t, 4096))

    merged = _merge(x, row(norm_mix), y_ret, y_s5, u, row(ssm_d), w_glu.astype(BF16),
                    w_merge.astype(BF16), w_branch_a.astype(BF16), w_branch_b.astype(BF16))

    pad = ROUTER_LANES - N_GROUPS - N_EXPERTS
    w_router = jnp.concatenate([w_rg, w_re, jnp.zeros((d, pad), F32)], axis=1).astype(F32)
    w_router_hi = w_router.astype(BF16)
    w_router_lo = (w_router - w_router_hi.astype(F32)).astype(BF16)
    b_router = jnp.concatenate([b_rg, b_re_router, jnp.zeros((pad,), F32)]).reshape(1, ROUTER_LANES).astype(F32)
    x1, route_i, route_w = _outproj(merged, x, w_out.astype(BF16), row(norm_ffn),
                                    jnp.concatenate([w_router_hi, w_router_lo], axis=1), b_router)

    block_expert, n_used, n_valid, row_tok, row_dst = _row_layout(route_i[:, :2])
    moe_rows = _moe(block_expert, n_used, n_valid, row_tok, row_dst, x1, row(norm_ffn),
                    w_exp_gate, w_exp_up, w_exp_down)

    return _final(x1, moe_rows, route_w, p, row(norm_ple), w_ple_gate.astype(BF16), w_ple.astype(BF16),
                  row(norm_f))


def kernel(x, p, norm_mix, w_in, ret_gn_w, ssm_lam_re, ssm_lam_im, ssm_log_dt, ssm_b_re, ssm_b_im, ssm_c_re, ssm_c_im, ssm_d, w_glu, w_branch_a, w_branch_b, w_merge, w_out, norm_ffn, w_router_group, b_router_group, w_router_expert, b_router_expert, w_exp_gate, w_exp_up, w_exp_down, norm_ple, w_ple_gate, w_ple, norm_f):
    depth, bsz, seq, _ = p.shape
    assert depth == 1 and bsz == 1, "single layer, single sequence"
    out = _layer(x[0], p[0, 0], norm_mix[0], w_in[0], ret_gn_w[0], ssm_lam_re[0], ssm_lam_im[0],
                 ssm_log_dt[0], ssm_b_re[0], ssm_b_im[0], ssm_c_re[0], ssm_c_im[0], ssm_d[0], w_glu[0],
                 w_branch_a[0], w_branch_b[0], w_merge[0], w_out[0], norm_ffn[0], w_router_group[0],
                 b_router_group[0], w_router_expert[0], b_router_expert[0], w_exp_gate[0], w_exp_up[0],
                 w_exp_down[0], norm_ple[0], w_ple_gate[0], w_ple[0], norm_f)
    return out[None]
```

```python
import math

import jax
import jax.numpy as jnp
from jax import lax
from jax.experimental import pallas as pl
from jax.experimental.pallas import tpu as pltpu

F32 = jnp.float32
BF16 = jnp.bfloat16
I32 = jnp.int32

EPS = 1e-6
LANES = 128
SUBLANES = 8
CHUNK_LOG2 = 6
RET_HEADS = 8
HEAD_DIM = 128
ROPE_THETA = 10000.0
SSM_GROUPS = 64
SSM_GROUP_SIZE = 16
SSM_STATE = 64
S5_CHUNK = 16
S5_TILE_GROUPS = LANES // SSM_GROUP_SIZE
S5_TILES = SSM_GROUPS // S5_TILE_GROUPS
S5_HALF = S5_TILE_GROUPS * SSM_STATE
N_GROUPS = 4
EXPERTS_PER_GROUP = 8
N_EXPERTS = 32
ROUTER_LANES = LANES
MOE_ROWS = 256

VMEM_LIMIT_BYTES = 56 * 1024 * 1024


def _params(sem):
    return pltpu.CompilerParams(dimension_semantics=sem, vmem_limit_bytes=VMEM_LIMIT_BYTES)


def _resident(shape):
    n = len(shape)
    return pl.BlockSpec(shape, lambda *_: (0,) * n, pipeline_mode=pl.Buffered(1))


def _rms(x, w):
    ms = jnp.mean(x * x, axis=-1, keepdims=True)
    return x * lax.rsqrt(ms + EPS) * w


def _dot(a, b):
    return jnp.dot(a, b, preferred_element_type=F32)


def _proj_kernel(x_ref, nw_ref, w_ref, q_ref, k_ref, v_ref, g_ref, u_ref):
    h = _rms(x_ref[...], nw_ref[...]).astype(BF16)
    for c, o_ref in enumerate((q_ref, k_ref, v_ref, g_ref, u_ref)):
        width = o_ref.shape[1]
        o_ref[...] = _dot(h, w_ref[:, c * width:(c + 1) * width])


def _proj(x, norm_w, w_in_bf16, tm=256):
    t, d = x.shape
    width = w_in_bf16.shape[1] // 5
    out = jax.ShapeDtypeStruct((t, width), F32)
    return pl.pallas_call(
        _proj_kernel,
        grid=(t // tm,),
        in_specs=[pl.BlockSpec((tm, d), lambda i: (i, 0)),
                  _resident((1, d)),
                  _resident(w_in_bf16.shape)],
        out_specs=[pl.BlockSpec((tm, width), lambda i: (i, 0))] * 5,
        out_shape=[out] * 5,
        compiler_params=_params(("parallel",)),
        name="proj",
    )(x, norm_w, w_in_bf16)


def _log_gamma(h):
    return math.log1p(-(2.0 ** (-5.0 - h)))


def _retention_kernel(q_ref, k_ref, v_ref, g_ref, cos_ref, sin_ref, gn_ref, o_ref,
                      state_ref, decay_ref, qdec_ref, kdec_ref):
    rows = q_ref.shape[0]

    @pl.when(pl.program_id(0) == 0)
    def _init():
        ti = lax.broadcasted_iota(I32, (rows, rows), 0)
        si = lax.broadcasted_iota(I32, (rows, rows), 1)
        visible = lax.shift_right_logical(si, CHUNK_LOG2) <= lax.shift_right_logical(ti, CHUNK_LOG2)
        dist = jnp.abs(ti - si).astype(F32)
        pos = lax.broadcasted_iota(I32, (rows, HEAD_DIM), 0).astype(F32)
        for h in range(RET_HEADS):
            lg = _log_gamma(h)
            decay_ref[h] = jnp.where(visible, jnp.exp(lg * dist), 0.0)
            qdec_ref[h] = jnp.exp(lg * (pos + 1.0))
            kdec_ref[h] = jnp.exp(lg * (rows - 1.0 - pos))
        state_ref[...] = jnp.zeros_like(state_ref)

    cos = cos_ref[...]
    sin = sin_ref[...]
    scale = HEAD_DIM ** -0.5
    for h in range(RET_HEADS):
        sl = slice(h * HEAD_DIM, (h + 1) * HEAD_DIM)
        qh = q_ref[:, sl]
        kh = k_ref[:, sl]
        vb = v_ref[:, sl].astype(BF16)
        qr = qh * cos + pltpu.roll(qh, HEAD_DIM // 2, 1) * sin
        kr = (kh * cos + pltpu.roll(kh, HEAD_DIM // 2, 1) * sin) * scale
        scores = lax.dot_general(qr.astype(BF16), kr.astype(BF16), (((1,), (1,)), ((), ())),
                                 preferred_element_type=F32) * decay_ref[h]
        state = state_ref[h]
        y = _dot(scores.astype(BF16), vb)
        y = y + _dot((qr * qdec_ref[h]).astype(BF16), state.astype(BF16))
        kv = lax.dot_general((kr * kdec_ref[h]).astype(BF16), vb, (((0,), (0,)), ((), ())),
                             preferred_element_type=F32)
        state_ref[h] = math.exp(_log_gamma(h) * rows) * state + kv
        mu = jnp.mean(y, axis=-1, keepdims=True)
        yc = y - mu
        var = jnp.mean(yc * yc, axis=-1, keepdims=True)
        yn = yc * lax.rsqrt(var + EPS) * gn_ref[:, sl]
        gh = g_ref[:, sl]
        o_ref[:, sl] = (gh * jax.nn.sigmoid(gh) * yn).astype(o_ref.dtype)


def _retention(q, k, v, g, cos_tab, sin_tab, gn_w, rows=256):
    t, width = q.shape
    tok = pl.BlockSpec((rows, width), lambda i: (i, 0))
    tab = pl.BlockSpec((rows, HEAD_DIM), lambda i: (i, 0))
    return pl.pallas_call(
        _retention_kernel,
        grid=(t // rows,),
        in_specs=[tok, tok, tok, tok, tab, tab, _resident((1, width))],
        out_specs=tok,
        out_shape=jax.ShapeDtypeStruct((t, width), BF16),
        scratch_shapes=[pltpu.VMEM((RET_HEADS, HEAD_DIM, HEAD_DIM), F32),
                        pltpu.VMEM((RET_HEADS, rows, rows), F32),
                        pltpu.VMEM((RET_HEADS, rows, HEAD_DIM), F32),
                        pltpu.VMEM((RET_HEADS, rows, HEAD_DIM), F32)],
        compiler_params=_params(("arbitrary",)),
        name="retention",
    )(q, k, v, g, cos_tab, sin_tab, gn_w)


def _cmul(ar, ai, br, bi):
    return ar * br - ai * bi, ar * bi + ai * br


def _s5_kernel(u_ref, m_ref, bst_ref, cout_ref, pw_ref, y_ref, a_ref, pr_ref, pi_ref, carry_ref):
    tq = u_ref.shape[0]
    nn = tq // S5_CHUNK
    nb = nn // SUBLANES
    half = S5_HALF

    @pl.when(pl.program_id(1) == 0)
    def _():
        carry_ref[...] = jnp.zeros_like(carry_ref)

    for l in range(S5_CHUNK):
        a_ref[:, l * LANES:(l + 1) * LANES] = u_ref[pl.ds(l, nn, stride=S5_CHUNK), :].astype(BF16)
    a = a_ref[...]
    yc = _dot(a, m_ref[...])
    s_in = _dot(a, bst_ref[...])
    sr, si = s_in[:, :half], s_in[:, half:]

    row = lax.broadcasted_iota(I32, (nn, half), 0)
    zr = jnp.where(row == 0, carry_ref[0:1, :], pltpu.roll(sr, 1, 0))
    zi = jnp.where(row == 0, carry_ref[1:2, :], pltpu.roll(si, 1, 0))
    in_block = jnp.bitwise_and(row, SUBLANES - 1)
    for d in (1, 2, 4):
        mr, mi = _cmul(pw_ref[0, d - 1:d, :], pw_ref[1, d - 1:d, :],
                       pltpu.roll(zr, d, 0), pltpu.roll(zi, d, 0))
        keep = in_block >= d
        zr = zr + jnp.where(keep, mr, 0.0)
        zi = zi + jnp.where(keep, mi, 0.0)
    pr_ref[...] = zr.reshape(nb, SUBLANES, half)
    pi_ref[...] = zi.reshape(nb, SUBLANES, half)

    wr, wi = pw_ref[0], pw_ref[1]

    def block_step(r, c):
        fr, fi = _cmul(wr, wi, c[0], c[1])
        nr = pr_ref[r] + fr
        ni = pi_ref[r] + fi
        pr_ref[r] = nr
        pi_ref[r] = ni
        last = slice(SUBLANES - 1, SUBLANES)
        return (jnp.broadcast_to(nr[last, :], (SUBLANES, half)),
                jnp.broadcast_to(ni[last, :], (SUBLANES, half)))

    zero = jnp.zeros((SUBLANES, half), F32)
    lr, li = lax.fori_loop(0, nb, block_step, (zero, zero))
    er, ei = _cmul(wr[0:1, :], wi[0:1, :], lr[0:1, :], li[0:1, :])
    carry_ref[0:1, :] = er + sr[nn - 1:nn, :]
    carry_ref[1:2, :] = ei + si[nn - 1:nn, :]

    p = jnp.concatenate([pr_ref[...].reshape(nn, half), pi_ref[...].reshape(nn, half)], axis=-1)
    yc = yc + _dot(p.astype(BF16), cout_ref[...])
    for l in range(S5_CHUNK):
        y_ref[pl.ds(l, nn, stride=S5_CHUNK), :] = yc[:, l * LANES:(l + 1) * LANES]


def _s5(u, m_op, bst_op, cout_op, pw_op, tq):
    t, width = u.shape
    nn = tq // S5_CHUNK
    kdim = S5_CHUNK * LANES

    def tile_block(*shape):
        n = len(shape)
        return pl.BlockSpec((None,) + shape, lambda o, i: (o,) + (0,) * n)

    tok = pl.BlockSpec((tq, LANES), lambda o, i: (i, o))
    return pl.pallas_call(
        _s5_kernel,
        grid=(width // LANES, t // tq),
        in_specs=[tok, tile_block(kdim, kdim), tile_block(kdim, 2 * S5_HALF),
                  tile_block(2 * S5_HALF, kdim), tile_block(2, SUBLANES, S5_HALF)],
        out_specs=tok,
        out_shape=jax.ShapeDtypeStruct((t, width), F32),
        scratch_shapes=[pltpu.VMEM((nn, kdim), BF16),
                        pltpu.VMEM((nn // SUBLANES, SUBLANES, S5_HALF), F32),
                        pltpu.VMEM((nn // SUBLANES, SUBLANES, S5_HALF), F32),
                        pltpu.VMEM((2, S5_HALF), F32)],
        compiler_params=_params(("parallel", "arbitrary")),
        name="s5",
    )(u, m_op, bst_op, cout_op, pw_op)


def _s5_operators(lam_re, lam_im, log_dt, b_re, b_im, c_re, c_im):
    g, p, cg = b_re.shape
    lc, nt, tg = S5_CHUNK, S5_TILES, S5_TILE_GROUPS
    lam = lax.complex(jnp.minimum(lam_re, -1e-4), lam_im)
    log_lam_bar = lam * jnp.exp(log_dt)[:, None]
    lam_bar = jnp.exp(log_lam_bar)
    b_bar = ((lam_bar - 1.0) / lam)[..., None] * lax.complex(b_re, b_im)
    c_mat = lax.complex(c_re, c_im)

    def power(n):
        n = jnp.asarray(n, F32)
        return jnp.exp(n[:, None, None].astype(jnp.complex64) * log_lam_bar)

    lags = jnp.arange(lc)
    taps = jnp.real(jnp.einsum('gap,dgp,gpc->dgac', c_mat, power(lags), b_bar,
                               precision=lax.Precision.HIGHEST))
    taps = taps.reshape(lc, nt, tg, cg, cg).transpose(1, 2, 4, 0, 3).reshape(nt, tg * cg, lc * cg)
    taps = jnp.pad(taps, ((0, 0), (0, 0), ((lc - 1) * cg, 0)))
    kc = jnp.stack([taps[:, :, (lc - 1 - l) * cg:(2 * lc - 1 - l) * cg] for l in range(lc)], axis=1)
    kc = kc.reshape(nt, lc * tg * cg, lc * cg)

    bst = power(lc - 1 - lags)[:, :, None, :] * b_bar.transpose(0, 2, 1)[None]
    bst = bst.reshape(lc, nt, tg * cg, p).transpose(1, 0, 2, 3).reshape(nt, lc * tg * cg, p)
    bc = jnp.concatenate([jnp.real(bst), jnp.imag(bst)], axis=-1)

    pw_out = jnp.exp((lags + 1).astype(jnp.complex64)[None, None, :] * log_lam_bar[:, :, None])
    cout = pw_out[:, :, :, None] * c_mat.transpose(0, 2, 1)[:, :, None, :]
    cout = cout.reshape(nt, tg * p, lc * cg)
    cc = jnp.concatenate([jnp.real(cout), -jnp.imag(cout)], axis=1)

    def replicate(n_in, size):
        i = jnp.arange(n_in)[:, None]
        j = jnp.arange(n_in * tg)[None, :]
        return jnp.logical_and(i // size == j // (tg * size), i % size == j % size).astype(BF16)

    def spread(compact, size_in, row_size, col_size):
        full = jnp.einsum('trk,kc->trc', compact.astype(BF16), replicate(compact.shape[2], size_in),
                          preferred_element_type=F32)
        row_group = (jnp.arange(full.shape[1]) // row_size) % tg
        col_group = (jnp.arange(full.shape[2]) // col_size) % tg
        return jnp.where(row_group[:, None] == col_group[None, :], full, 0.0).astype(BF16)

    m_op = spread(kc, cg, cg, cg)
    bst_op = spread(bc, p, cg, p)
    cout_op = spread(cc, cg, p, cg)

    pw = power(lc * (1 + jnp.arange(SUBLANES)))
    pw = pw.reshape(SUBLANES, nt, tg * p).transpose(1, 0, 2)
    pw_op = jnp.stack([jnp.real(pw), jnp.imag(pw)], axis=1).astype(F32)
    return m_op, bst_op, cout_op, pw_op


def _merge_kernel(x_ref, nw_ref, yret_ref, ys_ref, u_ref, d_ref, wglu_ref, wm_ref, wa_ref, wb_ref,
                  o_ref):
    d_model = x_ref.shape[1]
    h = _rms(x_ref[...], nw_ref[...]).astype(BF16)
    y = jax.nn.gelu(ys_ref[...] + d_ref[...] * u_ref[...])
    y_ssm = (y * jax.nn.sigmoid(_dot(y.astype(BF16), wglu_ref[...]))).astype(BF16)
    y_ret = yret_ref[...]
    cw = 512
    for c in range(d_model // cw):
        cs = slice(c * cw, (c + 1) * cw)
        gs = slice(d_model + c * cw, d_model + (c + 1) * cw)
        g0 = jax.nn.sigmoid(_dot(h, wm_ref[:, cs]))
        g1 = jax.nn.sigmoid(_dot(h, wm_ref[:, gs]))
        o_ref[:, cs] = (g0 * _dot(y_ret, wa_ref[:, cs]) + g1 * _dot(y_ssm, wb_ref[:, cs])).astype(o_ref.dtype)


def _merge(x, norm_w, y_ret, y_s5, u, ssm_d, w_glu, w_merge, w_a, w_b, tm=256):
    t, d = x.shape
    width = y_ret.shape[1]
    tok = lambda w: pl.BlockSpec((tm, w), lambda i: (i, 0))
    return pl.pallas_call(
        _merge_kernel,
        grid=(t // tm,),
        in_specs=[tok(d), _resident((1, d)), tok(width), tok(width), tok(width), _resident((1, width)),
                  _resident(w_glu.shape), _resident(w_merge.shape), _resident(w_a.shape),
                  _resident(w_b.shape)],
        out_specs=tok(d),
        out_shape=jax.ShapeDtypeStruct((t, d), BF16),
        compiler_params=_params(("parallel",)),
        name="merge",
    )(x, norm_w, y_ret, y_s5, u, ssm_d, w_glu, w_merge, w_a, w_b)


def _first_argmax(vals, lane):
    top = jnp.max(vals, axis=-1, keepdims=True)
    idx = jnp.min(jnp.where(vals == top, lane, ROUTER_LANES), axis=-1, keepdims=True)
    return top, idx


def _outproj_kernel(m_ref, x_ref, wo_ref, nw_ref, wr_ref, br_ref, x1_ref, ri_ref, rw_ref):
    x1 = x_ref[...] + _dot(m_ref[...], wo_ref[...])
    x1_ref[...] = x1
    h2 = _rms(x1, nw_ref[...])
    h_hi = h2.astype(BF16)
    h_lo = (h2 - h_hi.astype(F32)).astype(BF16)
    part = _dot(h_hi, wr_ref[...])
    logits = (part[:, :ROUTER_LANES] + part[:, ROUTER_LANES:]
              + _dot(h_lo, wr_ref[:, :ROUTER_LANES]) + br_ref[...])

    lane = lax.broadcasted_iota(I32, logits.shape, 1)
    neg = -jnp.inf
    is_group = lane < N_GROUPS
    g_top, g_idx = _first_argmax(jnp.where(is_group, logits, neg), lane)
    g_w = 1.0 / jnp.sum(jnp.where(is_group, jnp.exp(logits - g_top), 0.0), axis=-1, keepdims=True)
    first = N_GROUPS + EXPERTS_PER_GROUP * g_idx
    e_logits = jnp.where(lane >= first, jnp.where(lane < first + EXPERTS_PER_GROUP, logits, neg), neg)
    t1, i1 = _first_argmax(e_logits, lane)
    t2, i2 = _first_argmax(jnp.where(lane == i1, neg, e_logits), lane)
    ratio = jnp.exp(t2 - t1)
    w1 = g_w / (1.0 + ratio)
    w2 = w1 * ratio
    ri_ref[...] = jnp.where(lane == 0, i1 - N_GROUPS, jnp.where(lane == 1, i2 - N_GROUPS, 0))
    rw_ref[...] = jnp.where(lane == 0, w1, jnp.where(lane == 1, w2, 0.0))


def _outproj(merged, x, w_out, norm_w, w_router, b_router, tm=256):
    t, d = x.shape
    tok = lambda w: pl.BlockSpec((tm, w), lambda i: (i, 0))
    return pl.pallas_call(
        _outproj_kernel,
        grid=(t // tm,),
        in_specs=[tok(d), tok(d), _resident(w_out.shape), _resident((1, d)),
                  _resident(w_router.shape), _resident((1, ROUTER_LANES))],
        out_specs=[tok(d), tok(ROUTER_LANES), tok(ROUTER_LANES)],
        out_shape=[jax.ShapeDtypeStruct((t, d), F32),
                   jax.ShapeDtypeStruct((t, ROUTER_LANES), I32),
                   jax.ShapeDtypeStruct((t, ROUTER_LANES), F32)],
        compiler_params=_params(("parallel",)),
        name="outproj",
    )(merged, x, w_out, norm_w, w_router, b_router)


def _moe_kernel(be_ref, tok_ref, dst_ref,
                x_hbm, nw_ref, wg_ref, wu_ref, wd_ref,
                out_hbm,
                xbuf, ybuf, wgu_bf, wd_bf, gsem, ssem):
    s = pl.program_id(0)
    rows = xbuf.shape[1]
    f = wg_ref.shape[1]
    slot = lax.rem(s, 2)
    other = 1 - slot

    def gather_row(blk, buf_slot, r):
        tok = tok_ref[blk * rows + r]
        return pltpu.make_async_copy(x_hbm.at[pl.ds(tok, 1)], xbuf.at[buf_slot, pl.ds(r, 1)],
                                     gsem.at[buf_slot])

    def scatter_row(blk, buf_slot, r):
        dst = dst_ref[(blk + 1) * rows + r]
        return pltpu.make_async_copy(ybuf.at[buf_slot, pl.ds(r, 1)], out_hbm.at[pl.ds(dst, 1)],
                                     ssem.at[buf_slot])

    def wait_slot(buf, sem, buf_slot):
        pltpu.make_async_copy(buf.at[buf_slot], buf.at[buf_slot], sem.at[buf_slot]).wait()

    @pl.when(s == 0)
    def _prologue():
        ybuf[...] = jnp.zeros_like(ybuf)
        for r in range(rows):
            gather_row(0, 0, r).start()

    @pl.when(jnp.logical_or(s == 0, be_ref[s] != be_ref[jnp.maximum(s - 1, 0)]))
    def _cast_weights():
        wgu_bf[:, :f] = wg_ref[...].astype(BF16)
        wgu_bf[:, f:] = wu_ref[...].astype(BF16)
        wd_bf[...] = wd_ref[...].astype(BF16)

    wait_slot(xbuf, gsem, slot)
    for r in range(rows):
        scatter_row(s - 1, other, r).start()
    h = _rms(xbuf[slot], nw_ref[...]).astype(BF16)
    for r in range(rows):
        gather_row(s + 1, other, r).start()
    gu = _dot(h, wgu_bf[...])
    gate = gu[:, :f]
    act = (gate * jax.nn.sigmoid(gate) * gu[:, f:]).astype(BF16)
    y = _dot(act, wd_bf[...])

    @pl.when(s >= 1)
    def _free_slot():
        wait_slot(ybuf, ssem, slot)

    ybuf[slot] = y

    @pl.when(s == pl.num_programs(0) - 1)
    def _drain():
        wait_slot(ybuf, ssem, other)
        wait_slot(xbuf, gsem, other)


def _moe(block_expert, row_tok, row_dst, x1, norm_w, w_gate, w_up, w_down):
    t, d = x1.shape
    n_steps = block_expert.shape[0]
    _, _, f = w_gate.shape

    def expert_block(shape):
        return pl.BlockSpec((None,) + shape, lambda b, be, *_: (be[b], 0, 0))

    grid_spec = pltpu.PrefetchScalarGridSpec(
        num_scalar_prefetch=3,
        grid=(n_steps,),
        in_specs=[pl.BlockSpec(memory_space=pl.ANY),
                  pl.BlockSpec((1, d), lambda b, *_: (0, 0)),
                  expert_block((d, f)), expert_block((d, f)), expert_block((f, d))],
        out_specs=pl.BlockSpec(memory_space=pl.ANY),
        scratch_shapes=[pltpu.VMEM((2, MOE_ROWS, d), F32),
                        pltpu.VMEM((2, MOE_ROWS, d), F32),
                        pltpu.VMEM((d, 2 * f), BF16),
                        pltpu.VMEM((f, d), BF16),
                        pltpu.SemaphoreType.DMA((2,)),
                        pltpu.SemaphoreType.DMA((2,))],
    )
    return pl.pallas_call(
        _moe_kernel,
        grid_spec=grid_spec,
        out_shape=jax.ShapeDtypeStruct((row_dst.shape[0], d), F32),
        compiler_params=pltpu.CompilerParams(dimension_semantics=("arbitrary",),
                                             vmem_limit_bytes=VMEM_LIMIT_BYTES,
                                             has_side_effects=True),
        name="moe",
    )(block_expert, row_tok, row_dst, x1, norm_w, w_gate, w_up, w_down)


def _row_layout(expert):
    t = expert.shape[0]
    n_assign = 2 * t
    n_blocks = n_assign // MOE_ROWS + N_EXPERTS
    flat_e = expert.reshape(-1)
    order = jnp.argsort(flat_e).astype(I32)
    counts = jnp.sum(flat_e[:, None] == jnp.arange(N_EXPERTS, dtype=I32)[None, :], axis=0, dtype=I32)
    start = jnp.cumsum(counts) - counts
    padded = (counts + MOE_ROWS - 1) // MOE_ROWS * MOE_ROWS
    pad_end = jnp.cumsum(padded)
    pad_start = pad_end - padded
    block_start = jnp.arange(n_blocks, dtype=I32) * MOE_ROWS
    used = block_start < pad_end[-1]
    block_expert = jnp.minimum(jnp.sum(pad_end[None, :] <= block_start[:, None], axis=1, dtype=I32),
                               N_EXPERTS - 1)
    last_expert = block_expert[jnp.maximum(pad_end[-1] // MOE_ROWS - 1, 0)]
    block_expert = jnp.where(used, block_expert, last_expert)
    rank0 = block_start - pad_start[block_expert]
    n_valid = jnp.where(used, jnp.clip(counts[block_expert] - rank0, 0, MOE_ROWS), 0).astype(I32)
    within = jnp.arange(MOE_ROWS, dtype=I32)[None, :]
    valid = within < n_valid[:, None]
    src = order[jnp.clip(start[block_expert][:, None] + rank0[:, None] + within, 0, n_assign - 1)]
    tok = src // 2
    n_pad = MOE_ROWS - n_valid
    spare = n_assign + MOE_ROWS + (jnp.cumsum(n_pad) - n_pad)[:, None] + (within - n_valid[:, None])
    row_dst = jnp.where(valid, (src % 2) * t + tok, spare).astype(I32)
    row_dst = jnp.concatenate([n_assign + within, row_dst], axis=0).reshape(-1)
    row_tok = jnp.concatenate([jnp.where(valid, tok, 0), jnp.zeros((2, MOE_ROWS), I32)], axis=0).reshape(-1)
    block_expert = jnp.concatenate([block_expert, last_expert[None]])
    return block_expert, row_tok, row_dst


def _final_kernel(x1_ref, r0_ref, r1_ref, rw_ref, p_ref, nple_ref, wg_ref, wp_ref, nf_ref, o_ref):
    x2 = x1_ref[...] + (rw_ref[:, 0:1] * r0_ref[...] + rw_ref[:, 1:2] * r1_ref[...])
    h3 = _rms(x2, nple_ref[...]).astype(BF16)
    gate = jax.nn.sigmoid(_dot(h3, wg_ref[...]))
    x3 = x2 + gate * _dot(p_ref[...].astype(BF16), wp_ref[...])
    o_ref[...] = _rms(x3, nf_ref[...])


def _final(x1, moe_rows, route_w, p, norm_ple, w_ple_gate, w_ple, norm_f, tm=256):
    t, d = x1.shape
    nt = t // tm
    tok = lambda w: pl.BlockSpec((tm, w), lambda i: (i, 0))
    return pl.pallas_call(
        _final_kernel,
        grid=(nt,),
        in_specs=[tok(d), tok(d), pl.BlockSpec((tm, d), lambda i: (i + nt, 0)), tok(ROUTER_LANES),
                  tok(p.shape[1]), _resident((1, d)), _resident(w_ple_gate.shape),
                  _resident(w_ple.shape), _resident((1, d))],
        out_specs=tok(d),
        out_shape=jax.ShapeDtypeStruct((t, d), F32),
        compiler_params=_params(("parallel",)),
        name="final",
    )(x1, moe_rows, moe_rows, route_w, p, norm_ple, w_ple_gate, w_ple, norm_f)


def _rope_tables(t):
    half = HEAD_DIM // 2
    inv_freq = ROPE_THETA ** (-jnp.arange(half, dtype=F32) / half)
    ang = jnp.arange(t, dtype=F32)[:, None] * inv_freq[None, :]
    cos, sin = jnp.cos(ang), jnp.sin(ang)
    return jnp.concatenate([cos, cos], axis=-1), jnp.concatenate([-sin, sin], axis=-1)


def _layer(x, p, norm_mix, w_in, ret_gn_w, lam_re, lam_im, log_dt, b_re, b_im, c_re, c_im, ssm_d,
           w_glu, w_branch_a, w_branch_b, w_merge, w_out, norm_ffn, w_rg, b_rg, w_re, b_re_router,
           w_exp_gate, w_exp_up, w_exp_down, norm_ple, w_ple_gate, w_ple, norm_f):
    t, d = x.shape
    row = lambda v: v.reshape(1, -1).astype(F32)

    q, k, v, g, u = _proj(x, row(norm_mix), w_in.astype(BF16))
    cos_tab, sin_tab = _rope_tables(t)
    y_ret = _retention(q, k, v, g, cos_tab, sin_tab, row(ret_gn_w))

    ops = _s5_operators(lam_re, lam_im, log_dt, b_re, b_im, c_re, c_im)
    y_s5 = _s5(u, *ops, tq=min(t, 4096))

    merged = _merge(x, row(norm_mix), y_ret, y_s5, u, row(ssm_d), w_glu.astype(BF16),
                    w_merge.astype(BF16), w_branch_a.astype(BF16), w_branch_b.astype(BF16))

    pad = ROUTER_LANES - N_GROUPS - N_EXPERTS
    w_router = jnp.concatenate([w_rg, w_re, jnp.zeros((d, pad), F32)], axis=1).astype(F32)
    w_router_hi = w_router.astype(BF16)
    w_router_lo = (w_router - w_router_hi.astype(F32)).astype(BF16)
    b_router = jnp.concatenate([b_rg, b_re_router, jnp.zeros((pad,), F32)]).reshape(1, ROUTER_LANES).astype(F32)
    x1, route_i, route_w = _outproj(merged, x, w_out.astype(BF16), row(norm_ffn),
                                    jnp.concatenate([w_router_hi, w_router_lo], axis=1), b_router)

    block_expert, row_tok, row_dst = _row_layout(route_i[:, :2])
    moe_rows = _moe(block_expert, row_tok, row_dst, x1, row(norm_ffn), w_exp_gate, w_exp_up, w_exp_down)

    return _final(x1, moe_rows, route_w, p, row(norm_ple), w_ple_gate.astype(BF16), w_ple.astype(BF16),
                  row(norm_f))


def kernel(x, p, norm_mix, w_in, ret_gn_w, ssm_lam_re, ssm_lam_im, ssm_log_dt, ssm_b_re, ssm_b_im, ssm_c_re, ssm_c_im, ssm_d, w_glu, w_branch_a, w_branch_b, w_merge, w_out, norm_ffn, w_router_group, b_router_group, w_router_expert, b_router_expert, w_exp_gate, w_exp_up, w_exp_down, norm_ple, w_ple_gate, w_ple, norm_f):
    depth, bsz, seq, _ = p.shape
    assert depth == 1 and bsz == 1, "single layer, single sequence"
    out = _layer(x[0], p[0, 0], norm_mix[0], w_in[0], ret_gn_w[0], ssm_lam_re[0], ssm_lam_im[0],
                 ssm_log_dt[0], ssm_b_re[0], ssm_b_im[0], ssm_c_re[0], ssm_c_im[0], ssm_d[0], w_glu[0],
                 w_branch_a[0], w_branch_b[0], w_merge[0], w_out[0], norm_ffn[0], w_router_group[0],
                 b_router_group[0], w_router_expert[0], b_router_expert[0], w_exp_gate[0], w_exp_up[0],
                 w_exp_down[0], norm_ple[0], w_ple_gate[0], w_ple[0], norm_f)
    return out[None]
```

```python
import math

import jax
import jax.numpy as jnp
from jax import lax
from jax.experimental import pallas as pl
from jax.experimental.pallas import tpu as pltpu

F32 = jnp.float32
BF16 = jnp.bfloat16
I32 = jnp.int32

EPS = 1e-6
LANES = 128
SUBLANES = 8
CHUNK_LOG2 = 6
RET_HEADS = 8
HEAD_DIM = 128
ROPE_THETA = 10000.0
SSM_GROUPS = 64
SSM_GROUP_SIZE = 16
SSM_STATE = 64
S5_CHUNK = 16
S5_TILE_GROUPS = LANES // SSM_GROUP_SIZE
S5_TILES = SSM_GROUPS // S5_TILE_GROUPS
S5_HALF = S5_TILE_GROUPS * SSM_STATE
N_GROUPS = 4
EXPERTS_PER_GROUP = 8
N_EXPERTS = 32
ROUTER_LANES = LANES
MOE_ROWS = 256
SLAB = 16

VMEM_LIMIT_BYTES = 56 * 1024 * 1024


def _params(sem):
    return pltpu.CompilerParams(dimension_semantics=sem, vmem_limit_bytes=VMEM_LIMIT_BYTES)


def _resident(shape):
    n = len(shape)
    return pl.BlockSpec(shape, lambda *_: (0,) * n, pipeline_mode=pl.Buffered(1))


def _rms(x, w):
    ms = jnp.mean(x * x, axis=-1, keepdims=True)
    return x * lax.rsqrt(ms + EPS) * w


def _dot(a, b):
    return jnp.dot(a, b, preferred_element_type=F32)


def _proj_kernel(x_ref, nw_ref, w_ref, q_ref, k_ref, v_ref, g_ref, u_ref):
    h = _rms(x_ref[...], nw_ref[...]).astype(BF16)
    for c, o_ref in enumerate((q_ref, k_ref, v_ref, g_ref, u_ref)):
        width = o_ref.shape[1]
        o_ref[...] = _dot(h, w_ref[:, c * width:(c + 1) * width])


def _proj(x, norm_w, w_in_bf16, tm=256):
    t, d = x.shape
    width = w_in_bf16.shape[1] // 5
    out = jax.ShapeDtypeStruct((t, width), F32)
    return pl.pallas_call(
        _proj_kernel,
        grid=(t // tm,),
        in_specs=[pl.BlockSpec((tm, d), lambda i: (i, 0)),
                  _resident((1, d)),
                  _resident(w_in_bf16.shape)],
        out_specs=[pl.BlockSpec((tm, width), lambda i: (i, 0))] * 5,
        out_shape=[out] * 5,
        compiler_params=_params(("parallel",)),
        name="proj",
    )(x, norm_w, w_in_bf16)


def _log_gamma(h):
    return math.log1p(-(2.0 ** (-5.0 - h)))


def _retention_kernel(q_ref, k_ref, v_ref, g_ref, cos_ref, sin_ref, gn_ref, o_ref,
                      state_ref, decay_ref, qdec_ref, kdec_ref):
    rows = q_ref.shape[0]

    @pl.when(pl.program_id(0) == 0)
    def _init():
        ti = lax.broadcasted_iota(I32, (rows, rows), 0)
        si = lax.broadcasted_iota(I32, (rows, rows), 1)
        visible = lax.shift_right_logical(si, CHUNK_LOG2) <= lax.shift_right_logical(ti, CHUNK_LOG2)
        dist = jnp.abs(ti - si).astype(F32)
        pos = lax.broadcasted_iota(I32, (rows, HEAD_DIM), 0).astype(F32)
        for h in range(RET_HEADS):
            lg = _log_gamma(h)
            decay_ref[h] = jnp.where(visible, jnp.exp(lg * dist), 0.0)
            qdec_ref[h] = jnp.exp(lg * (pos + 1.0))
            kdec_ref[h] = jnp.exp(lg * (rows - 1.0 - pos))
        state_ref[...] = jnp.zeros_like(state_ref)

    cos = cos_ref[...]
    sin = sin_ref[...]
    scale = HEAD_DIM ** -0.5
    for h in range(RET_HEADS):
        sl = slice(h * HEAD_DIM, (h + 1) * HEAD_DIM)
        qh = q_ref[:, sl]
        kh = k_ref[:, sl]
        vb = v_ref[:, sl].astype(BF16)
        qr = qh * cos + pltpu.roll(qh, HEAD_DIM // 2, 1) * sin
        kr = (kh * cos + pltpu.roll(kh, HEAD_DIM // 2, 1) * sin) * scale
        scores = lax.dot_general(qr.astype(BF16), kr.astype(BF16), (((1,), (1,)), ((), ())),
                                 preferred_element_type=F32) * decay_ref[h]
        state = state_ref[h]
        y = _dot(scores.astype(BF16), vb)
        y = y + _dot((qr * qdec_ref[h]).astype(BF16), state.astype(BF16))
        kv = lax.dot_general((kr * kdec_ref[h]).astype(BF16), vb, (((0,), (0,)), ((), ())),
                             preferred_element_type=F32)
        state_ref[h] = math.exp(_log_gamma(h) * rows) * state + kv
        mu = jnp.mean(y, axis=-1, keepdims=True)
        yc = y - mu
        var = jnp.mean(yc * yc, axis=-1, keepdims=True)
        yn = yc * lax.rsqrt(var + EPS) * gn_ref[:, sl]
        gh = g_ref[:, sl]
        o_ref[:, sl] = (gh * jax.nn.sigmoid(gh) * yn).astype(o_ref.dtype)


def _retention(q, k, v, g, cos_tab, sin_tab, gn_w, rows=256):
    t, width = q.shape
    tok = pl.BlockSpec((rows, width), lambda i: (i, 0))
    tab = pl.BlockSpec((rows, HEAD_DIM), lambda i: (i, 0))
    return pl.pallas_call(
        _retention_kernel,
        grid=(t // rows,),
        in_specs=[tok, tok, tok, tok, tab, tab, _resident((1, width))],
        out_specs=tok,
        out_shape=jax.ShapeDtypeStruct((t, width), BF16),
        scratch_shapes=[pltpu.VMEM((RET_HEADS, HEAD_DIM, HEAD_DIM), F32),
                        pltpu.VMEM((RET_HEADS, rows, rows), F32),
                        pltpu.VMEM((RET_HEADS, rows, HEAD_DIM), F32),
                        pltpu.VMEM((RET_HEADS, rows, HEAD_DIM), F32)],
        compiler_params=_params(("arbitrary",)),
        name="retention",
    )(q, k, v, g, cos_tab, sin_tab, gn_w)


def _cmul(ar, ai, br, bi):
    return ar * br - ai * bi, ar * bi + ai * br


def _s5_kernel(u_ref, m_ref, bst_ref, cout_ref, pw_ref, y_ref, a_ref, pr_ref, pi_ref, carry_ref):
    tq = u_ref.shape[0]
    nn = tq // S5_CHUNK
    nb = nn // SUBLANES
    half = S5_HALF

    @pl.when(pl.program_id(1) == 0)
    def _():
        carry_ref[...] = jnp.zeros_like(carry_ref)

    for l in range(S5_CHUNK):
        a_ref[:, l * LANES:(l + 1) * LANES] = u_ref[pl.ds(l, nn, stride=S5_CHUNK), :].astype(BF16)
    a = a_ref[...]
    yc = _dot(a, m_ref[...])
    s_in = _dot(a, bst_ref[...])
    sr, si = s_in[:, :half], s_in[:, half:]

    row = lax.broadcasted_iota(I32, (nn, half), 0)
    zr = jnp.where(row == 0, carry_ref[0:1, :], pltpu.roll(sr, 1, 0))
    zi = jnp.where(row == 0, carry_ref[1:2, :], pltpu.roll(si, 1, 0))
    in_block = jnp.bitwise_and(row, SUBLANES - 1)
    for d in (1, 2, 4):
        mr, mi = _cmul(pw_ref[0, d - 1:d, :], pw_ref[1, d - 1:d, :],
                       pltpu.roll(zr, d, 0), pltpu.roll(zi, d, 0))
        keep = in_block >= d
        zr = zr + jnp.where(keep, mr, 0.0)
        zi = zi + jnp.where(keep, mi, 0.0)
    pr_ref[...] = zr.reshape(nb, SUBLANES, half)
    pi_ref[...] = zi.reshape(nb, SUBLANES, half)

    wr, wi = pw_ref[0], pw_ref[1]

    def block_step(r, c):
        fr, fi = _cmul(wr, wi, c[0], c[1])
        nr = pr_ref[r] + fr
        ni = pi_ref[r] + fi
        pr_ref[r] = nr
        pi_ref[r] = ni
        last = slice(SUBLANES - 1, SUBLANES)
        return (jnp.broadcast_to(nr[last, :], (SUBLANES, half)),
                jnp.broadcast_to(ni[last, :], (SUBLANES, half)))

    zero = jnp.zeros((SUBLANES, half), F32)
    lr, li = lax.fori_loop(0, nb, block_step, (zero, zero))
    er, ei = _cmul(wr[0:1, :], wi[0:1, :], lr[0:1, :], li[0:1, :])
    carry_ref[0:1, :] = er + sr[nn - 1:nn, :]
    carry_ref[1:2, :] = ei + si[nn - 1:nn, :]

    p = jnp.concatenate([pr_ref[...].reshape(nn, half), pi_ref[...].reshape(nn, half)], axis=-1)
    yc = yc + _dot(p.astype(BF16), cout_ref[...])
    for l in range(S5_CHUNK):
        y_ref[pl.ds(l, nn, stride=S5_CHUNK), :] = yc[:, l * LANES:(l + 1) * LANES]


def _s5(u, m_op, bst_op, cout_op, pw_op, tq):
    t, width = u.shape
    nn = tq // S5_CHUNK
    kdim = S5_CHUNK * LANES

    def tile_block(*shape):
        n = len(shape)
        return pl.BlockSpec((None,) + shape, lambda o, i: (o,) + (0,) * n)

    tok = pl.BlockSpec((tq, LANES), lambda o, i: (i, o))
    return pl.pallas_call(
        _s5_kernel,
        grid=(width // LANES, t // tq),
        in_specs=[tok, tile_block(kdim, kdim), tile_block(kdim, 2 * S5_HALF),
                  tile_block(2 * S5_HALF, kdim), tile_block(2, SUBLANES, S5_HALF)],
        out_specs=tok,
        out_shape=jax.ShapeDtypeStruct((t, width), F32),
        scratch_shapes=[pltpu.VMEM((nn, kdim), BF16),
                        pltpu.VMEM((nn // SUBLANES, SUBLANES, S5_HALF), F32),
                        pltpu.VMEM((nn // SUBLANES, SUBLANES, S5_HALF), F32),
                        pltpu.VMEM((2, S5_HALF), F32)],
        compiler_params=_params(("parallel", "arbitrary")),
        name="s5",
    )(u, m_op, bst_op, cout_op, pw_op)


def _s5_operators(lam_re, lam_im, log_dt, b_re, b_im, c_re, c_im):
    g, p, cg = b_re.shape
    lc, nt, tg = S5_CHUNK, S5_TILES, S5_TILE_GROUPS
    lam = lax.complex(jnp.minimum(lam_re, -1e-4), lam_im)
    log_lam_bar = lam * jnp.exp(log_dt)[:, None]
    lam_bar = jnp.exp(log_lam_bar)
    b_bar = ((lam_bar - 1.0) / lam)[..., None] * lax.complex(b_re, b_im)
    c_mat = lax.complex(c_re, c_im)

    def power(n):
        n = jnp.asarray(n, F32)
        return jnp.exp(n[:, None, None].astype(jnp.complex64) * log_lam_bar)

    lags = jnp.arange(lc)
    taps = jnp.real(jnp.einsum('gap,dgp,gpc->dgac', c_mat, power(lags), b_bar,
                               precision=lax.Precision.HIGHEST))
    taps = taps.reshape(lc, nt, tg, cg, cg).transpose(1, 2, 4, 0, 3).reshape(nt, tg * cg, lc * cg)
    taps = jnp.pad(taps, ((0, 0), (0, 0), ((lc - 1) * cg, 0)))
    kc = jnp.stack([taps[:, :, (lc - 1 - l) * cg:(2 * lc - 1 - l) * cg] for l in range(lc)], axis=1)
    kc = kc.reshape(nt, lc * tg * cg, lc * cg)

    bst = power(lc - 1 - lags)[:, :, None, :] * b_bar.transpose(0, 2, 1)[None]
    bst = bst.reshape(lc, nt, tg * cg, p).transpose(1, 0, 2, 3).reshape(nt, lc * tg * cg, p)
    bc = jnp.concatenate([jnp.real(bst), jnp.imag(bst)], axis=-1)

    pw_out = jnp.exp((lags + 1).astype(jnp.complex64)[None, None, :] * log_lam_bar[:, :, None])
    cout = pw_out[:, :, :, None] * c_mat.transpose(0, 2, 1)[:, :, None, :]
    cout = cout.reshape(nt, tg * p, lc * cg)
    cc = jnp.concatenate([jnp.real(cout), -jnp.imag(cout)], axis=1)

    def replicate(n_in, size):
        i = jnp.arange(n_in)[:, None]
        j = jnp.arange(n_in * tg)[None, :]
        return jnp.logical_and(i // size == j // (tg * size), i % size == j % size).astype(BF16)

    def spread(compact, size_in, row_size, col_size):
        full = jnp.einsum('trk,kc->trc', compact.astype(BF16), replicate(compact.shape[2], size_in),
                          preferred_element_type=F32)
        row_group = (jnp.arange(full.shape[1]) // row_size) % tg
        col_group = (jnp.arange(full.shape[2]) // col_size) % tg
        return jnp.where(row_group[:, None] == col_group[None, :], full, 0.0).astype(BF16)

    m_op = spread(kc, cg, cg, cg)
    bst_op = spread(bc, p, cg, p)
    cout_op = spread(cc, cg, p, cg)

    pw = power(lc * (1 + jnp.arange(SUBLANES)))
    pw = pw.reshape(SUBLANES, nt, tg * p).transpose(1, 0, 2)
    pw_op = jnp.stack([jnp.real(pw), jnp.imag(pw)], axis=1).astype(F32)
    return m_op, bst_op, cout_op, pw_op


def _merge_kernel(x_ref, nw_ref, yret_ref, ys_ref, u_ref, d_ref, wglu_ref, wm_ref, wa_ref, wb_ref,
                  o_ref):
    d_model = x_ref.shape[1]
    h = _rms(x_ref[...], nw_ref[...]).astype(BF16)
    y = jax.nn.gelu(ys_ref[...] + d_ref[...] * u_ref[...])
    y_ssm = (y * jax.nn.sigmoid(_dot(y.astype(BF16), wglu_ref[...]))).astype(BF16)
    y_ret = yret_ref[...]
    cw = 512
    for c in range(d_model // cw):
        cs = slice(c * cw, (c + 1) * cw)
        gs = slice(d_model + c * cw, d_model + (c + 1) * cw)
        g0 = jax.nn.sigmoid(_dot(h, wm_ref[:, cs]))
        g1 = jax.nn.sigmoid(_dot(h, wm_ref[:, gs]))
        o_ref[:, cs] = (g0 * _dot(y_ret, wa_ref[:, cs]) + g1 * _dot(y_ssm, wb_ref[:, cs])).astype(o_ref.dtype)


def _merge(x, norm_w, y_ret, y_s5, u, ssm_d, w_glu, w_merge, w_a, w_b, tm=256):
    t, d = x.shape
    width = y_ret.shape[1]
    tok = lambda w: pl.BlockSpec((tm, w), lambda i: (i, 0))
    return pl.pallas_call(
        _merge_kernel,
        grid=(t // tm,),
        in_specs=[tok(d), _resident((1, d)), tok(width), tok(width), tok(width), _resident((1, width)),
                  _resident(w_glu.shape), _resident(w_merge.shape), _resident(w_a.shape),
                  _resident(w_b.shape)],
        out_specs=tok(d),
        out_shape=jax.ShapeDtypeStruct((t, d), BF16),
        compiler_params=_params(("parallel",)),
        name="merge",
    )(x, norm_w, y_ret, y_s5, u, ssm_d, w_glu, w_merge, w_a, w_b)


def _first_argmax(vals, lane):
    top = jnp.max(vals, axis=-1, keepdims=True)
    idx = jnp.min(jnp.where(vals == top, lane, ROUTER_LANES), axis=-1, keepdims=True)
    return top, idx


def _slab_store(ref, value, lead=()):
    rows = value.shape[0]
    for j in range(value.shape[1] // LANES):
        ref[lead + (pl.ds(j, rows, stride=SLAB), slice(None))] = value[:, j * LANES:(j + 1) * LANES]


def _slab_piece(ref, j, rows, lead=()):
    return ref[lead + (pl.ds(j, rows, stride=SLAB), slice(None))]


def _outproj_kernel(m_ref, x_ref, wo_ref, nw_ref, wr_ref, br_ref, x1_ref, ri_ref, rw_ref):
    x1 = x_ref[...] + _dot(m_ref[...], wo_ref[...])
    _slab_store(x1_ref, x1)
    h2 = _rms(x1, nw_ref[...])
    h_hi = h2.astype(BF16)
    h_lo = (h2 - h_hi.astype(F32)).astype(BF16)
    part = _dot(h_hi, wr_ref[...])
    logits = (part[:, :ROUTER_LANES] + part[:, ROUTER_LANES:]
              + _dot(h_lo, wr_ref[:, :ROUTER_LANES]) + br_ref[...])

    lane = lax.broadcasted_iota(I32, logits.shape, 1)
    neg = -jnp.inf
    is_group = lane < N_GROUPS
    g_top, g_idx = _first_argmax(jnp.where(is_group, logits, neg), lane)
    g_w = 1.0 / jnp.sum(jnp.where(is_group, jnp.exp(logits - g_top), 0.0), axis=-1, keepdims=True)
    first = N_GROUPS + EXPERTS_PER_GROUP * g_idx
    e_logits = jnp.where(lane >= first, jnp.where(lane < first + EXPERTS_PER_GROUP, logits, neg), neg)
    t1, i1 = _first_argmax(e_logits, lane)
    t2, i2 = _first_argmax(jnp.where(lane == i1, neg, e_logits), lane)
    ratio = jnp.exp(t2 - t1)
    w1 = g_w / (1.0 + ratio)
    w2 = w1 * ratio
    ri_ref[...] = jnp.where(lane == 0, i1 - N_GROUPS, jnp.where(lane == 1, i2 - N_GROUPS, 0))
    rw_ref[...] = jnp.where(lane == 0, w1, jnp.where(lane == 1, w2, 0.0))


def _outproj(merged, x, w_out, norm_w, w_router, b_router, tm=256):
    t, d = x.shape
    tok = lambda w: pl.BlockSpec((tm, w), lambda i: (i, 0))
    return pl.pallas_call(
        _outproj_kernel,
        grid=(t // tm,),
        in_specs=[tok(d), tok(d), _resident(w_out.shape), _resident((1, d)),
                  _resident(w_router.shape), _resident((1, ROUTER_LANES))],
        out_specs=[pl.BlockSpec((tm * SLAB, LANES), lambda i: (i, 0)), tok(ROUTER_LANES), tok(ROUTER_LANES)],
        out_shape=[jax.ShapeDtypeStruct((t * SLAB, LANES), F32),
                   jax.ShapeDtypeStruct((t, ROUTER_LANES), I32),
                   jax.ShapeDtypeStruct((t, ROUTER_LANES), F32)],
        compiler_params=_params(("parallel",)),
        name="outproj",
    )(merged, x, w_out, norm_w, w_router, b_router)


def _moe_kernel(be_ref, tok_ref, dst_ref,
                x_hbm, nw_ref, wg_ref, wu_ref, wd_ref,
                out_hbm,
                xbuf, ybuf, h_ref, wgu_bf, wd_bf, gsem, ssem):
    s = pl.program_id(0)
    rows, d_model = h_ref.shape
    f = wg_ref.shape[1]
    slot = lax.rem(s, 2)
    other = 1 - slot

    def gather_row(blk, buf_slot, r):
        tok = pl.multiple_of(tok_ref[blk * rows + r], SLAB)
        return pltpu.make_async_copy(x_hbm.at[pl.ds(tok, SLAB)], xbuf.at[buf_slot, pl.ds(r * SLAB, SLAB)],
                                     gsem.at[buf_slot])

    def scatter_row(blk, buf_slot, r):
        dst = pl.multiple_of(dst_ref[(blk + 1) * rows + r], SLAB)
        return pltpu.make_async_copy(ybuf.at[buf_slot, pl.ds(r * SLAB, SLAB)], out_hbm.at[pl.ds(dst, SLAB)],
                                     ssem.at[buf_slot])

    def wait_slot(buf, sem, buf_slot):
        pltpu.make_async_copy(buf.at[buf_slot], buf.at[buf_slot], sem.at[buf_slot]).wait()

    @pl.when(s == 0)
    def _prologue():
        ybuf[...] = jnp.zeros_like(ybuf)
        for r in range(rows):
            gather_row(0, 0, r).start()

    @pl.when(jnp.logical_or(s == 0, be_ref[s] != be_ref[jnp.maximum(s - 1, 0)]))
    def _cast_weights():
        wgu_bf[:, :f] = wg_ref[...].astype(BF16)
        wgu_bf[:, f:] = wu_ref[...].astype(BF16)
        wd_bf[...] = wd_ref[...].astype(BF16)

    wait_slot(xbuf, gsem, slot)
    for r in range(rows):
        scatter_row(s - 1, other, r).start()
    ss = jnp.zeros((rows, 1), F32)
    for j in range(SLAB):
        xj = _slab_piece(xbuf, j, rows, (slot,))
        ss = ss + jnp.sum(xj * xj, axis=-1, keepdims=True)
    inv = lax.rsqrt(ss * (1.0 / d_model) + EPS)
    for j in range(SLAB):
        cols = slice(j * LANES, (j + 1) * LANES)
        h_ref[:, cols] = (_slab_piece(xbuf, j, rows, (slot,)) * inv * nw_ref[:, cols]).astype(BF16)
    for r in range(rows):
        gather_row(s + 1, other, r).start()
    gu = _dot(h_ref[...], wgu_bf[...])
    gate = gu[:, :f]
    act = (gate * jax.nn.sigmoid(gate) * gu[:, f:]).astype(BF16)
    y = _dot(act, wd_bf[...])

    @pl.when(s >= 1)
    def _free_slot():
        wait_slot(ybuf, ssem, slot)

    _slab_store(ybuf, y, (slot,))

    @pl.when(s == pl.num_programs(0) - 1)
    def _drain():
        wait_slot(ybuf, ssem, other)
        wait_slot(xbuf, gsem, other)


def _moe(block_expert, row_tok, row_dst, x1_slabs, norm_w, w_gate, w_up, w_down):
    n_steps = block_expert.shape[0]
    _, d, f = w_gate.shape
    assert d == SLAB * LANES

    def expert_block(shape):
        return pl.BlockSpec((None,) + shape, lambda b, be, *_: (be[b], 0, 0))

    grid_spec = pltpu.PrefetchScalarGridSpec(
        num_scalar_prefetch=3,
        grid=(n_steps,),
        in_specs=[pl.BlockSpec(memory_space=pl.ANY),
                  pl.BlockSpec((1, d), lambda b, *_: (0, 0)),
                  expert_block((d, f)), expert_block((d, f)), expert_block((f, d))],
        out_specs=pl.BlockSpec(memory_space=pl.ANY),
        scratch_shapes=[pltpu.VMEM((2, MOE_ROWS * SLAB, LANES), F32),
                        pltpu.VMEM((2, MOE_ROWS * SLAB, LANES), F32),
                        pltpu.VMEM((MOE_ROWS, d), BF16),
                        pltpu.VMEM((d, 2 * f), BF16),
                        pltpu.VMEM((f, d), BF16),
                        pltpu.SemaphoreType.DMA((2,)),
                        pltpu.SemaphoreType.DMA((2,))],
    )
    return pl.pallas_call(
        _moe_kernel,
        grid_spec=grid_spec,
        out_shape=jax.ShapeDtypeStruct((row_dst.shape[0] * SLAB, LANES), F32),
        compiler_params=pltpu.CompilerParams(dimension_semantics=("arbitrary",),
                                             vmem_limit_bytes=VMEM_LIMIT_BYTES,
                                             has_side_effects=True),
        name="moe",
    )(block_expert, row_tok * SLAB, row_dst * SLAB, x1_slabs, norm_w, w_gate, w_up, w_down)


def _row_layout(expert):
    t = expert.shape[0]
    n_assign = 2 * t
    n_blocks = n_assign // MOE_ROWS + N_EXPERTS
    flat_e = expert.reshape(-1)
    order = jnp.argsort(flat_e).astype(I32)
    counts = jnp.sum(flat_e[:, None] == jnp.arange(N_EXPERTS, dtype=I32)[None, :], axis=0, dtype=I32)
    start = jnp.cumsum(counts) - counts
    padded = (counts + MOE_ROWS - 1) // MOE_ROWS * MOE_ROWS
    pad_end = jnp.cumsum(padded)
    pad_start = pad_end - padded
    block_start = jnp.arange(n_blocks, dtype=I32) * MOE_ROWS
    used = block_start < pad_end[-1]
    block_expert = jnp.minimum(jnp.sum(pad_end[None, :] <= block_start[:, None], axis=1, dtype=I32),
                               N_EXPERTS - 1)
    last_expert = block_expert[jnp.maximum(pad_end[-1] // MOE_ROWS - 1, 0)]
    block_expert = jnp.where(used, block_expert, last_expert)
    rank0 = block_start - pad_start[block_expert]
    n_valid = jnp.where(used, jnp.clip(counts[block_expert] - rank0, 0, MOE_ROWS), 0).astype(I32)
    within = jnp.arange(MOE_ROWS, dtype=I32)[None, :]
    valid = within < n_valid[:, None]
    src = order[jnp.clip(start[block_expert][:, None] + rank0[:, None] + within, 0, n_assign - 1)]
    tok = src // 2
    n_pad = MOE_ROWS - n_valid
    spare = n_assign + MOE_ROWS + (jnp.cumsum(n_pad) - n_pad)[:, None] + (within - n_valid[:, None])
    row_dst = jnp.where(valid, (src % 2) * t + tok, spare).astype(I32)
    row_dst = jnp.concatenate([n_assign + within, row_dst], axis=0).reshape(-1)
    row_tok = jnp.concatenate([jnp.where(valid, tok, 0), jnp.zeros((2, MOE_ROWS), I32)], axis=0).reshape(-1)
    block_expert = jnp.concatenate([block_expert, last_expert[None]])
    return block_expert, row_tok, row_dst


def _final_kernel(x1_ref, r0_ref, r1_ref, rw_ref, p_ref, nple_ref, wg_ref, wp_ref, nf_ref, o_ref, x2_ref):
    tm = x2_ref.shape[0]
    w0, w1 = rw_ref[:, 0:1], rw_ref[:, 1:2]
    for j in range(SLAB):
        x2_ref[:, j * LANES:(j + 1) * LANES] = _slab_piece(x1_ref, j, tm) + (
            w0 * _slab_piece(r0_ref, j, tm) + w1 * _slab_piece(r1_ref, j, tm))
    x2 = x2_ref[...]
    h3 = _rms(x2, nple_ref[...]).astype(BF16)
    gate = jax.nn.sigmoid(_dot(h3, wg_ref[...]))
    x3 = x2 + gate * _dot(p_ref[...].astype(BF16), wp_ref[...])
    o_ref[...] = _rms(x3, nf_ref[...])


def _final(x1_slabs, moe_slabs, route_w, p, norm_ple, w_ple_gate, w_ple, norm_f, tm=256):
    t = route_w.shape[0]
    d = SLAB * LANES
    nt = t // tm
    tok = lambda w: pl.BlockSpec((tm, w), lambda i: (i, 0))
    slabs = lambda first: pl.BlockSpec((tm * SLAB, LANES), lambda i: (i + first, 0))
    return pl.pallas_call(
        _final_kernel,
        grid=(nt,),
        in_specs=[slabs(0), slabs(0), slabs(nt), tok(ROUTER_LANES),
                  tok(p.shape[1]), _resident((1, d)), _resident(w_ple_gate.shape),
                  _resident(w_ple.shape), _resident((1, d))],
        out_specs=tok(d),
        out_shape=jax.ShapeDtypeStruct((t, d), F32),
        scratch_shapes=[pltpu.VMEM((tm, d), F32)],
        compiler_params=_params(("parallel",)),
        name="final",
    )(x1_slabs, moe_slabs, moe_slabs, route_w, p, norm_ple, w_ple_gate, w_ple, norm_f)


def _rope_tables(t):
    half = HEAD_DIM // 2
    inv_freq = ROPE_THETA ** (-jnp.arange(half, dtype=F32) / half)
    ang = jnp.arange(t, dtype=F32)[:, None] * inv_freq[None, :]
    cos, sin = jnp.cos(ang), jnp.sin(ang)
    return jnp.concatenate([cos, cos], axis=-1), jnp.concatenate([-sin, sin], axis=-1)


def _layer(x, p, norm_mix, w_in, ret_gn_w, lam_re, lam_im, log_dt, b_re, b_im, c_re, c_im, ssm_d,
           w_glu, w_branch_a, w_branch_b, w_merge, w_out, norm_ffn, w_rg, b_rg, w_re, b_re_router,
           w_exp_gate, w_exp_up, w_exp_down, norm_ple, w_ple_gate, w_ple, norm_f):
    t, d = x.shape
    row = lambda v: v.reshape(1, -1).astype(F32)

    q, k, v, g, u = _proj(x, row(norm_mix), w_in.astype(BF16))
    cos_tab, sin_tab = _rope_tables(t)
    y_ret = _retention(q, k, v, g, cos_tab, sin_tab, row(ret_gn_w))

    ops = _s5_operators(lam_re, lam_im, log_dt, b_re, b_im, c_re, c_im)
    y_s5 = _s5(u, *ops, tq=min(t, 4096))

    merged = _merge(x, row(norm_mix), y_ret, y_s5, u, row(ssm_d), w_glu.astype(BF16),
                    w_merge.astype(BF16), w_branch_a.astype(BF16), w_branch_b.astype(BF16))

    pad = ROUTER_LANES - N_GROUPS - N_EXPERTS
    w_router = jnp.concatenate([w_rg, w_re, jnp.zeros((d, pad), F32)], axis=1).astype(F32)
    w_router_hi = w_router.astype(BF16)
    w_router_lo = (w_router - w_router_hi.astype(F32)).astype(BF16)
    b_router = jnp.concatenate([b_rg, b_re_router, jnp.zeros((pad,), F32)]).reshape(1, ROUTER_LANES).astype(F32)
    x1, route_i, route_w = _outproj(merged, x, w_out.astype(BF16), row(norm_ffn),
                                    jnp.concatenate([w_router_hi, w_router_lo], axis=1), b_router)

    block_expert, row_tok, row_dst = _row_layout(route_i[:, :2])
    moe_rows = _moe(block_expert, row_tok, row_dst, x1, row(norm_ffn), w_exp_gate, w_exp_up, w_exp_down)

    return _final(x1, moe_rows, route_w, p, row(norm_ple), w_ple_gate.astype(BF16), w_ple.astype(BF16),
                  row(norm_f))


def kernel(x, p, norm_mix, w_in, ret_gn_w, ssm_lam_re, ssm_lam_im, ssm_log_dt, ssm_b_re, ssm_b_im, ssm_c_re, ssm_c_im, ssm_d, w_glu, w_branch_a, w_branch_b, w_merge, w_out, norm_ffn, w_router_group, b_router_group, w_router_expert, b_router_expert, w_exp_gate, w_exp_up, w_exp_down, norm_ple, w_ple_gate, w_ple, norm_f):
    depth, bsz, seq, _ = p.shape
    assert depth == 1 and bsz == 1, "single layer, single sequence"
    out = _layer(x[0], p[0, 0], norm_mix[0], w_in[0], ret_gn_w[0], ssm_lam_re[0], ssm_lam_im[0],
                 ssm_log_dt[0], ssm_b_re[0], ssm_b_im[0], ssm_c_re[0], ssm_c_im[0], ssm_d[0], w_glu[0],
                 w_branch_a[0], w_branch_b[0], w_merge[0], w_out[0], norm_ffn[0], w_router_group[0],
                 b_router_group[0], w_router_expert[0], b_router_expert[0], w_exp_gate[0], w_exp_up[0],
                 w_exp_down[0], norm_ple[0], w_ple_gate[0], w_ple[0], norm_f)
    return out[None]
```

```python
import math

import jax
import jax.numpy as jnp
from jax import lax
from jax.experimental import pallas as pl
from jax.experimental.pallas import tpu as pltpu

F32 = jnp.float32
BF16 = jnp.bfloat16
I32 = jnp.int32

EPS = 1e-6
LANES = 128
SUBLANES = 8
CHUNK_LOG2 = 6
RET_HEADS = 8
HEAD_DIM = 128
ROPE_THETA = 10000.0
SSM_GROUPS = 64
SSM_GROUP_SIZE = 16
SSM_STATE = 64
S5_CHUNK = 16
S5_TILE_GROUPS = LANES // SSM_GROUP_SIZE
S5_TILES = SSM_GROUPS // S5_TILE_GROUPS
S5_HALF = S5_TILE_GROUPS * SSM_STATE
N_GROUPS = 4
EXPERTS_PER_GROUP = 8
N_EXPERTS = 32
ROUTER_LANES = LANES
MOE_ROWS = 256
SLAB = 16

VMEM_LIMIT_BYTES = 56 * 1024 * 1024


def _params(sem):
    return pltpu.CompilerParams(dimension_semantics=sem, vmem_limit_bytes=VMEM_LIMIT_BYTES)


def _resident(shape):
    n = len(shape)
    return pl.BlockSpec(shape, lambda *_: (0,) * n, pipeline_mode=pl.Buffered(1))


def _rms(x, w):
    ms = jnp.mean(x * x, axis=-1, keepdims=True)
    return x * lax.rsqrt(ms + EPS) * w


def _dot(a, b):
    return jnp.dot(a, b, preferred_element_type=F32)


def _proj_kernel(x_ref, nw_ref, w_ref, q_ref, k_ref, v_ref, g_ref, u_ref):
    h = _rms(x_ref[...], nw_ref[...]).astype(BF16)
    for c, o_ref in enumerate((q_ref, k_ref, v_ref, g_ref, u_ref)):
        width = o_ref.shape[1]
        o_ref[...] = _dot(h, w_ref[:, c * width:(c + 1) * width])


def _proj(x, norm_w, w_in_bf16, tm=256):
    t, d = x.shape
    width = w_in_bf16.shape[1] // 5
    out = jax.ShapeDtypeStruct((t, width), F32)
    return pl.pallas_call(
        _proj_kernel,
        grid=(t // tm,),
        in_specs=[pl.BlockSpec((tm, d), lambda i: (i, 0)),
                  _resident((1, d)),
                  _resident(w_in_bf16.shape)],
        out_specs=[pl.BlockSpec((tm, width), lambda i: (i, 0))] * 5,
        out_shape=[out] * 5,
        compiler_params=_params(("parallel",)),
        name="proj",
    )(x, norm_w, w_in_bf16)


def _log_gamma(h):
    return math.log1p(-(2.0 ** (-5.0 - h)))


def _retention_kernel(q_ref, k_ref, v_ref, g_ref, cos_ref, sin_ref, gn_ref, o_ref,
                      state_ref, decay_ref, qdec_ref, kdec_ref):
    rows = q_ref.shape[0]

    @pl.when(pl.program_id(0) == 0)
    def _init():
        ti = lax.broadcasted_iota(I32, (rows, rows), 0)
        si = lax.broadcasted_iota(I32, (rows, rows), 1)
        visible = lax.shift_right_logical(si, CHUNK_LOG2) <= lax.shift_right_logical(ti, CHUNK_LOG2)
        dist = jnp.abs(ti - si).astype(F32)
        pos = lax.broadcasted_iota(I32, (rows, HEAD_DIM), 0).astype(F32)
        for h in range(RET_HEADS):
            lg = _log_gamma(h)
            decay_ref[h] = jnp.where(visible, jnp.exp(lg * dist), 0.0)
            qdec_ref[h] = jnp.exp(lg * (pos + 1.0))
            kdec_ref[h] = jnp.exp(lg * (rows - 1.0 - pos))
        state_ref[...] = jnp.zeros_like(state_ref)

    cos = cos_ref[...]
    sin = sin_ref[...]
    scale = HEAD_DIM ** -0.5
    for h in range(RET_HEADS):
        sl = slice(h * HEAD_DIM, (h + 1) * HEAD_DIM)
        qh = q_ref[:, sl]
        kh = k_ref[:, sl]
        vb = v_ref[:, sl].astype(BF16)
        qr = qh * cos + pltpu.roll(qh, HEAD_DIM // 2, 1) * sin
        kr = (kh * cos + pltpu.roll(kh, HEAD_DIM // 2, 1) * sin) * scale
        scores = lax.dot_general(qr.astype(BF16), kr.astype(BF16), (((1,), (1,)), ((), ())),
                                 preferred_element_type=F32) * decay_ref[h]
        state = state_ref[h]
        y = _dot(scores.astype(BF16), vb)
        y = y + _dot((qr * qdec_ref[h]).astype(BF16), state.astype(BF16))
        kv = lax.dot_general((kr * kdec_ref[h]).astype(BF16), vb, (((0,), (0,)), ((), ())),
                             preferred_element_type=F32)
        state_ref[h] = math.exp(_log_gamma(h) * rows) * state + kv
        mu = jnp.mean(y, axis=-1, keepdims=True)
        yc = y - mu
        var = jnp.mean(yc * yc, axis=-1, keepdims=True)
        yn = yc * lax.rsqrt(var + EPS) * gn_ref[:, sl]
        gh = g_ref[:, sl]
        o_ref[:, sl] = (gh * jax.nn.sigmoid(gh) * yn).astype(o_ref.dtype)


def _retention(q, k, v, g, cos_tab, sin_tab, gn_w, rows=256):
    t, width = q.shape
    tok = pl.BlockSpec((rows, width), lambda i: (i, 0))
    tab = pl.BlockSpec((rows, HEAD_DIM), lambda i: (i, 0))
    return pl.pallas_call(
        _retention_kernel,
        grid=(t // rows,),
        in_specs=[tok, tok, tok, tok, tab, tab, _resident((1, width))],
        out_specs=tok,
        out_shape=jax.ShapeDtypeStruct((t, width), BF16),
        scratch_shapes=[pltpu.VMEM((RET_HEADS, HEAD_DIM, HEAD_DIM), F32),
                        pltpu.VMEM((RET_HEADS, rows, rows), F32),
                        pltpu.VMEM((RET_HEADS, rows, HEAD_DIM), F32),
                        pltpu.VMEM((RET_HEADS, rows, HEAD_DIM), F32)],
        compiler_params=_params(("arbitrary",)),
        name="retention",
    )(q, k, v, g, cos_tab, sin_tab, gn_w)


def _cmul(ar, ai, br, bi):
    return ar * br - ai * bi, ar * bi + ai * br


def _s5_kernel(u_ref, m_ref, bst_ref, cout_ref, pw_ref, y_ref, a_ref, pr_ref, pi_ref, carry_ref):
    tq = u_ref.shape[0]
    nn = tq // S5_CHUNK
    nb = nn // SUBLANES
    half = S5_HALF

    @pl.when(pl.program_id(1) == 0)
    def _():
        carry_ref[...] = jnp.zeros_like(carry_ref)

    for l in range(S5_CHUNK):
        a_ref[:, l * LANES:(l + 1) * LANES] = u_ref[pl.ds(l, nn, stride=S5_CHUNK), :].astype(BF16)
    a = a_ref[...]
    yc = _dot(a, m_ref[...])
    s_in = _dot(a, bst_ref[...])
    sr, si = s_in[:, :half], s_in[:, half:]

    row = lax.broadcasted_iota(I32, (nn, half), 0)
    zr = jnp.where(row == 0, carry_ref[0:1, :], pltpu.roll(sr, 1, 0))
    zi = jnp.where(row == 0, carry_ref[1:2, :], pltpu.roll(si, 1, 0))
    in_block = jnp.bitwise_and(row, SUBLANES - 1)
    for d in (1, 2, 4):
        mr, mi = _cmul(pw_ref[0, d - 1:d, :], pw_ref[1, d - 1:d, :],
                       pltpu.roll(zr, d, 0), pltpu.roll(zi, d, 0))
        keep = in_block >= d
        zr = zr + jnp.where(keep, mr, 0.0)
        zi = zi + jnp.where(keep, mi, 0.0)
    pr_ref[...] = zr.reshape(nb, SUBLANES, half)
    pi_ref[...] = zi.reshape(nb, SUBLANES, half)

    wr, wi = pw_ref[0], pw_ref[1]

    def block_step(r, c):
        fr, fi = _cmul(wr, wi, c[0], c[1])
        nr = pr_ref[r] + fr
        ni = pi_ref[r] + fi
        pr_ref[r] = nr
        pi_ref[r] = ni
        last = slice(SUBLANES - 1, SUBLANES)
        return (jnp.broadcast_to(nr[last, :], (SUBLANES, half)),
                jnp.broadcast_to(ni[last, :], (SUBLANES, half)))

    zero = jnp.zeros((SUBLANES, half), F32)
    lr, li = lax.fori_loop(0, nb, block_step, (zero, zero))
    er, ei = _cmul(wr[0:1, :], wi[0:1, :], lr[0:1, :], li[0:1, :])
    carry_ref[0:1, :] = er + sr[nn - 1:nn, :]
    carry_ref[1:2, :] = ei + si[nn - 1:nn, :]

    p = jnp.concatenate([pr_ref[...].reshape(nn, half), pi_ref[...].reshape(nn, half)], axis=-1)
    yc = yc + _dot(p.astype(BF16), cout_ref[...])
    for l in range(S5_CHUNK):
        y_ref[pl.ds(l, nn, stride=S5_CHUNK), :] = yc[:, l * LANES:(l + 1) * LANES]


def _s5(u, m_op, bst_op, cout_op, pw_op, tq):
    t, width = u.shape
    nn = tq // S5_CHUNK
    kdim = S5_CHUNK * LANES

    def tile_block(*shape):
        n = len(shape)
        return pl.BlockSpec((None,) + shape, lambda o, i: (o,) + (0,) * n)

    tok = pl.BlockSpec((tq, LANES), lambda o, i: (i, o))
    return pl.pallas_call(
        _s5_kernel,
        grid=(width // LANES, t // tq),
        in_specs=[tok, tile_block(kdim, kdim), tile_block(kdim, 2 * S5_HALF),
                  tile_block(2 * S5_HALF, kdim), tile_block(2, SUBLANES, S5_HALF)],
        out_specs=tok,
        out_shape=jax.ShapeDtypeStruct((t, width), F32),
        scratch_shapes=[pltpu.VMEM((nn, kdim), BF16),
                        pltpu.VMEM((nn // SUBLANES, SUBLANES, S5_HALF), F32),
                        pltpu.VMEM((nn // SUBLANES, SUBLANES, S5_HALF), F32),
                        pltpu.VMEM((2, S5_HALF), F32)],
        compiler_params=_params(("parallel", "arbitrary")),
        name="s5",
    )(u, m_op, bst_op, cout_op, pw_op)


def _s5_operators(lam_re, lam_im, log_dt, b_re, b_im, c_re, c_im):
    g, p, cg = b_re.shape
    lc, nt, tg = S5_CHUNK, S5_TILES, S5_TILE_GROUPS
    lam = lax.complex(jnp.minimum(lam_re, -1e-4), lam_im)
    log_lam_bar = lam * jnp.exp(log_dt)[:, None]
    lam_bar = jnp.exp(log_lam_bar)
    b_bar = ((lam_bar - 1.0) / lam)[..., None] * lax.complex(b_re, b_im)
    c_mat = lax.complex(c_re, c_im)

    def power(n):
        n = jnp.asarray(n, F32)
        return jnp.exp(n[:, None, None].astype(jnp.complex64) * log_lam_bar)

    lags = jnp.arange(lc)
    taps = jnp.real(jnp.einsum('gap,dgp,gpc->dgac', c_mat, power(lags), b_bar,
                               precision=lax.Precision.HIGHEST))
    taps = taps.reshape(lc, nt, tg, cg, cg).transpose(1, 2, 4, 0, 3).reshape(nt, tg * cg, lc * cg)
    taps = jnp.pad(taps, ((0, 0), (0, 0), ((lc - 1) * cg, 0)))
    kc = jnp.stack([taps[:, :, (lc - 1 - l) * cg:(2 * lc - 1 - l) * cg] for l in range(lc)], axis=1)
    kc = kc.reshape(nt, lc * tg * cg, lc * cg)

    bst = power(lc - 1 - lags)[:, :, None, :] * b_bar.transpose(0, 2, 1)[None]
    bst = bst.reshape(lc, nt, tg * cg, p).transpose(1, 0, 2, 3).reshape(nt, lc * tg * cg, p)
    bc = jnp.concatenate([jnp.real(bst), jnp.imag(bst)], axis=-1)

    pw_out = jnp.exp((lags + 1).astype(jnp.complex64)[None, None, :] * log_lam_bar[:, :, None])
    cout = pw_out[:, :, :, None] * c_mat.transpose(0, 2, 1)[:, :, None, :]
    cout = cout.reshape(nt, tg * p, lc * cg)
    cc = jnp.concatenate([jnp.real(cout), -jnp.imag(cout)], axis=1)

    def replicate(n_in, size):
        i = jnp.arange(n_in)[:, None]
        j = jnp.arange(n_in * tg)[None, :]
        return jnp.logical_and(i // size == j // (tg * size), i % size == j % size).astype(BF16)

    def spread(compact, size_in, row_size, col_size):
        full = jnp.einsum('trk,kc->trc', compact.astype(BF16), replicate(compact.shape[2], size_in),
                          preferred_element_type=F32)
        row_group = (jnp.arange(full.shape[1]) // row_size) % tg
        col_group = (jnp.arange(full.shape[2]) // col_size) % tg
        return jnp.where(row_group[:, None] == col_group[None, :], full, 0.0).astype(BF16)

    m_op = spread(kc, cg, cg, cg)
    bst_op = spread(bc, p, cg, p)
    cout_op = spread(cc, cg, p, cg)

    pw = power(lc * (1 + jnp.arange(SUBLANES)))
    pw = pw.reshape(SUBLANES, nt, tg * p).transpose(1, 0, 2)
    pw_op = jnp.stack([jnp.real(pw), jnp.imag(pw)], axis=1).astype(F32)
    return m_op, bst_op, cout_op, pw_op


def _merge_kernel(x_ref, nw_ref, yret_ref, ys_ref, u_ref, d_ref, wglu_ref, wm_ref, wa_ref, wb_ref,
                  o_ref):
    d_model = x_ref.shape[1]
    h = _rms(x_ref[...], nw_ref[...]).astype(BF16)
    y = jax.nn.gelu(ys_ref[...] + d_ref[...] * u_ref[...])
    y_ssm = (y * jax.nn.sigmoid(_dot(y.astype(BF16), wglu_ref[...]))).astype(BF16)
    y_ret = yret_ref[...]
    cw = 512
    for c in range(d_model // cw):
        cs = slice(c * cw, (c + 1) * cw)
        gs = slice(d_model + c * cw, d_model + (c + 1) * cw)
        g0 = jax.nn.sigmoid(_dot(h, wm_ref[:, cs]))
        g1 = jax.nn.sigmoid(_dot(h, wm_ref[:, gs]))
        o_ref[:, cs] = (g0 * _dot(y_ret, wa_ref[:, cs]) + g1 * _dot(y_ssm, wb_ref[:, cs])).astype(o_ref.dtype)


def _merge(x, norm_w, y_ret, y_s5, u, ssm_d, w_glu, w_merge, w_a, w_b, tm=256):
    t, d = x.shape
    width = y_ret.shape[1]
    tok = lambda w: pl.BlockSpec((tm, w), lambda i: (i, 0))
    return pl.pallas_call(
        _merge_kernel,
        grid=(t // tm,),
        in_specs=[tok(d), _resident((1, d)), tok(width), tok(width), tok(width), _resident((1, width)),
                  _resident(w_glu.shape), _resident(w_merge.shape), _resident(w_a.shape),
                  _resident(w_b.shape)],
        out_specs=tok(d),
        out_shape=jax.ShapeDtypeStruct((t, d), BF16),
        compiler_params=_params(("parallel",)),
        name="merge",
    )(x, norm_w, y_ret, y_s5, u, ssm_d, w_glu, w_merge, w_a, w_b)


def _first_argmax(vals, lane):
    top = jnp.max(vals, axis=-1, keepdims=True)
    idx = jnp.min(jnp.where(vals == top, lane, ROUTER_LANES), axis=-1, keepdims=True)
    return top, idx


def _slab_store(ref, value, lead=()):
    rows = value.shape[0]
    for j in range(value.shape[1] // LANES):
        ref[lead + (pl.ds(j, rows, stride=SLAB), slice(None))] = value[:, j * LANES:(j + 1) * LANES]


def _slab_piece(ref, j, rows, lead=()):
    return ref[lead + (pl.ds(j, rows, stride=SLAB), slice(None))]


def _outproj_kernel(m_ref, x_ref, wo_ref, nw_ref, wr_ref, br_ref, x1_ref, ri_ref, rw_ref):
    x1 = x_ref[...] + _dot(m_ref[...], wo_ref[...])
    _slab_store(x1_ref, x1)
    h2 = _rms(x1, nw_ref[...])
    h_hi = h2.astype(BF16)
    h_lo = (h2 - h_hi.astype(F32)).astype(BF16)
    part = _dot(h_hi, wr_ref[...])
    logits = (part[:, :ROUTER_LANES] + part[:, ROUTER_LANES:]
              + _dot(h_lo, wr_ref[:, :ROUTER_LANES]) + br_ref[...])

    lane = lax.broadcasted_iota(I32, logits.shape, 1)
    neg = -jnp.inf
    is_group = lane < N_GROUPS
    g_top, g_idx = _first_argmax(jnp.where(is_group, logits, neg), lane)
    g_w = 1.0 / jnp.sum(jnp.where(is_group, jnp.exp(logits - g_top), 0.0), axis=-1, keepdims=True)
    first = N_GROUPS + EXPERTS_PER_GROUP * g_idx
    e_logits = jnp.where(lane >= first, jnp.where(lane < first + EXPERTS_PER_GROUP, logits, neg), neg)
    t1, i1 = _first_argmax(e_logits, lane)
    t2, i2 = _first_argmax(jnp.where(lane == i1, neg, e_logits), lane)
    ratio = jnp.exp(t2 - t1)
    w1 = g_w / (1.0 + ratio)
    w2 = w1 * ratio
    ri_ref[...] = jnp.where(lane == 0, i1 - N_GROUPS, jnp.where(lane == 1, i2 - N_GROUPS, 0))
    rw_ref[...] = jnp.where(lane == 0, w1, jnp.where(lane == 1, w2, 0.0))


def _outproj(merged, x, w_out, norm_w, w_router, b_router, tm=256):
    t, d = x.shape
    tok = lambda w: pl.BlockSpec((tm, w), lambda i: (i, 0))
    return pl.pallas_call(
        _outproj_kernel,
        grid=(t // tm,),
        in_specs=[tok(d), tok(d), _resident(w_out.shape), _resident((1, d)),
                  _resident(w_router.shape), _resident((1, ROUTER_LANES))],
        out_specs=[pl.BlockSpec((tm * SLAB, LANES), lambda i: (i, 0)), tok(ROUTER_LANES), tok(ROUTER_LANES)],
        out_shape=[jax.ShapeDtypeStruct((t * SLAB, LANES), F32),
                   jax.ShapeDtypeStruct((t, ROUTER_LANES), I32),
                   jax.ShapeDtypeStruct((t, ROUTER_LANES), F32)],
        compiler_params=_params(("parallel",)),
        name="outproj",
    )(merged, x, w_out, norm_w, w_router, b_router)


def _moe_kernel(be_ref, nu_ref, tok_ref, dst_ref,
                x_hbm, nw_ref, wg_ref, wu_ref, wd_ref,
                out_hbm,
                xbuf, ybuf, zbuf, h_ref, wgu_bf, wd_bf, gsem, ssem):
    s = pl.program_id(0)
    n_used = nu_ref[0]
    rows, d_model = h_ref.shape
    f = wg_ref.shape[1]
    slot = lax.rem(s, 2)
    other = 1 - slot

    def gather_row(blk, buf_slot, r):
        tok = pl.multiple_of(tok_ref[blk * rows + r], SLAB)
        return pltpu.make_async_copy(x_hbm.at[pl.ds(tok, SLAB)], xbuf.at[buf_slot, pl.ds(r * SLAB, SLAB)],
                                     gsem.at[buf_slot])

    def scatter_row(blk, buf_slot, r):
        dst = pl.multiple_of(dst_ref[(blk + 1) * rows + r], SLAB)
        return pltpu.make_async_copy(ybuf.at[buf_slot, pl.ds(r * SLAB, SLAB)], out_hbm.at[pl.ds(dst, SLAB)],
                                     ssem.at[buf_slot])

    def wait_slot(buf, sem, buf_slot):
        pltpu.make_async_copy(buf.at[buf_slot], buf.at[buf_slot], sem.at[buf_slot]).wait()

    @pl.when(s == 0)
    def _prologue():
        ybuf[...] = jnp.zeros_like(ybuf)
        zbuf[...] = jnp.zeros_like(zbuf)
        for r in range(rows):
            gather_row(0, 0, r).start()

    @pl.when(jnp.logical_and(s < n_used,
                             jnp.logical_or(s == 0, be_ref[s] != be_ref[jnp.maximum(s - 1, 0)])))
    def _cast_weights():
        wgu_bf[:, :f] = wg_ref[...].astype(BF16)
        wgu_bf[:, f:] = wu_ref[...].astype(BF16)
        wd_bf[...] = wd_ref[...].astype(BF16)

    @pl.when(s < n_used)
    def _block():
        wait_slot(xbuf, gsem, slot)
        for r in range(rows):
            scatter_row(s - 1, other, r).start()
        ss = jnp.zeros((rows, 1), F32)
        for j in range(SLAB):
            xj = _slab_piece(xbuf, j, rows, (slot,))
            ss = ss + jnp.sum(xj * xj, axis=-1, keepdims=True)
        inv = lax.rsqrt(ss * (1.0 / d_model) + EPS)
        for j in range(SLAB):
            cols = slice(j * LANES, (j + 1) * LANES)
            h_ref[:, cols] = (_slab_piece(xbuf, j, rows, (slot,)) * inv * nw_ref[:, cols]).astype(BF16)
        for r in range(rows):
            gather_row(s + 1, other, r).start()
        gu = _dot(h_ref[...], wgu_bf[...])
        gate = gu[:, :f]
        act = (gate * jax.nn.sigmoid(gate) * gu[:, f:]).astype(BF16)
        y = _dot(act, wd_bf[...])

        @pl.when(s >= 1)
        def _free_slot():
            wait_slot(ybuf, ssem, slot)

        _slab_store(ybuf, y, (slot,))

    @pl.when(s >= n_used)
    def _tail():
        wait_slot(ybuf, ssem, slot)

        @pl.when(s == n_used)
        def _last_block():
            wait_slot(xbuf, gsem, slot)
            for r in range(rows):
                scatter_row(s - 1, other, r).start()

        @pl.when(s > n_used)
        def _padding_block():
            dst = pl.multiple_of(dst_ref[s * rows], SLAB)
            pltpu.make_async_copy(zbuf, out_hbm.at[pl.ds(dst, rows * SLAB)], ssem.at[other]).start()

    @pl.when(s == pl.num_programs(0) - 1)
    def _drain():
        wait_slot(ybuf, ssem, other)


def _moe(block_expert, n_used, row_tok, row_dst, x1_slabs, norm_w, w_gate, w_up, w_down):
    n_steps = block_expert.shape[0]
    _, d, f = w_gate.shape
    assert d == SLAB * LANES

    def expert_block(shape):
        return pl.BlockSpec((None,) + shape, lambda b, be, *_: (be[b], 0, 0))

    grid_spec = pltpu.PrefetchScalarGridSpec(
        num_scalar_prefetch=4,
        grid=(n_steps,),
        in_specs=[pl.BlockSpec(memory_space=pl.ANY),
                  pl.BlockSpec((1, d), lambda b, *_: (0, 0)),
                  expert_block((d, f)), expert_block((d, f)), expert_block((f, d))],
        out_specs=pl.BlockSpec(memory_space=pl.ANY),
        scratch_shapes=[pltpu.VMEM((2, MOE_ROWS * SLAB, LANES), F32),
                        pltpu.VMEM((2, MOE_ROWS * SLAB, LANES), F32),
                        pltpu.VMEM((MOE_ROWS * SLAB, LANES), F32),
                        pltpu.VMEM((MOE_ROWS, d), BF16),
                        pltpu.VMEM((d, 2 * f), BF16),
                        pltpu.VMEM((f, d), BF16),
                        pltpu.SemaphoreType.DMA((2,)),
                        pltpu.SemaphoreType.DMA((2,))],
    )
    return pl.pallas_call(
        _moe_kernel,
        grid_spec=grid_spec,
        out_shape=jax.ShapeDtypeStruct((row_dst.shape[0] * SLAB, LANES), F32),
        compiler_params=pltpu.CompilerParams(dimension_semantics=("arbitrary",),
                                             vmem_limit_bytes=VMEM_LIMIT_BYTES,
                                             has_side_effects=True),
        name="moe",
    )(block_expert, n_used, row_tok * SLAB, row_dst * SLAB, x1_slabs, norm_w, w_gate, w_up, w_down)


def _row_layout(expert):
    t = expert.shape[0]
    n_assign = 2 * t
    n_blocks = n_assign // MOE_ROWS + N_EXPERTS
    flat_e = expert.reshape(-1)
    order = jnp.argsort(flat_e).astype(I32)
    counts = jnp.sum(flat_e[:, None] == jnp.arange(N_EXPERTS, dtype=I32)[None, :], axis=0, dtype=I32)
    start = jnp.cumsum(counts) - counts
    padded = (counts + MOE_ROWS - 1) // MOE_ROWS * MOE_ROWS
    pad_end = jnp.cumsum(padded)
    pad_start = pad_end - padded
    block_start = jnp.arange(n_blocks, dtype=I32) * MOE_ROWS
    used = block_start < pad_end[-1]
    block_expert = jnp.minimum(jnp.sum(pad_end[None, :] <= block_start[:, None], axis=1, dtype=I32),
                               N_EXPERTS - 1)
    last_expert = block_expert[jnp.maximum(pad_end[-1] // MOE_ROWS - 1, 0)]
    block_expert = jnp.where(used, block_expert, last_expert)
    rank0 = block_start - pad_start[block_expert]
    n_valid = jnp.where(used, jnp.clip(counts[block_expert] - rank0, 0, MOE_ROWS), 0).astype(I32)
    within = jnp.arange(MOE_ROWS, dtype=I32)[None, :]
    valid = within < n_valid[:, None]
    src = order[jnp.clip(start[block_expert][:, None] + rank0[:, None] + within, 0, n_assign - 1)]
    tok = src // 2
    n_pad = MOE_ROWS - n_valid
    spare = n_assign + MOE_ROWS + (jnp.cumsum(n_pad) - n_pad)[:, None] + (within - n_valid[:, None])
    row_dst = jnp.where(valid, (src % 2) * t + tok, spare).astype(I32)
    row_dst = jnp.concatenate([n_assign + within, row_dst], axis=0).reshape(-1)
    row_tok = jnp.concatenate([jnp.where(valid, tok, 0), jnp.zeros((2, MOE_ROWS), I32)], axis=0).reshape(-1)
    block_expert = jnp.concatenate([block_expert, last_expert[None]])
    n_used = (pad_end[-1] // MOE_ROWS).astype(I32).reshape(1)
    return block_expert, n_used, row_tok, row_dst


def _final_kernel(x1_ref, r0_ref, r1_ref, rw_ref, p_ref, nple_ref, wg_ref, wp_ref, nf_ref, o_ref, x2_ref):
    tm = x2_ref.shape[0]
    w0, w1 = rw_ref[:, 0:1], rw_ref[:, 1:2]
    for j in range(SLAB):
        x2_ref[:, j * LANES:(j + 1) * LANES] = _slab_piece(x1_ref, j, tm) + (
            w0 * _slab_piece(r0_ref, j, tm) + w1 * _slab_piece(r1_ref, j, tm))
    x2 = x2_ref[...]
    h3 = _rms(x2, nple_ref[...]).astype(BF16)
    gate = jax.nn.sigmoid(_dot(h3, wg_ref[...]))
    x3 = x2 + gate * _dot(p_ref[...].astype(BF16), wp_ref[...])
    o_ref[...] = _rms(x3, nf_ref[...])


def _final(x1_slabs, moe_slabs, route_w, p, norm_ple, w_ple_gate, w_ple, norm_f, tm=256):
    t = route_w.shape[0]
    d = SLAB * LANES
    nt = t // tm
    tok = lambda w: pl.BlockSpec((tm, w), lambda i: (i, 0))
    slabs = lambda first: pl.BlockSpec((tm * SLAB, LANES), lambda i: (i + first, 0))
    return pl.pallas_call(
        _final_kernel,
        grid=(nt,),
        in_specs=[slabs(0), slabs(0), slabs(nt), tok(ROUTER_LANES),
                  tok(p.shape[1]), _resident((1, d)), _resident(w_ple_gate.shape),
                  _resident(w_ple.shape), _resident((1, d))],
        out_specs=tok(d),
        out_shape=jax.ShapeDtypeStruct((t, d), F32),
        scratch_shapes=[pltpu.VMEM((tm, d), F32)],
        compiler_params=_params(("parallel",)),
        name="final",
    )(x1_slabs, moe_slabs, moe_slabs, route_w, p, norm_ple, w_ple_gate, w_ple, norm_f)


def _rope_tables(t):
    half = HEAD_DIM // 2
    inv_freq = ROPE_THETA ** (-jnp.arange(half, dtype=F32) / half)
    ang = jnp.arange(t, dtype=F32)[:, None] * inv_freq[None, :]
    cos, sin = jnp.cos(ang), jnp.sin(ang)
    return jnp.concatenate([cos, cos], axis=-1), jnp.concatenate([-sin, sin], axis=-1)


def _layer(x, p, norm_mix, w_in, ret_gn_w, lam_re, lam_im, log_dt, b_re, b_im, c_re, c_im, ssm_d,
           w_glu, w_branch_a, w_branch_b, w_merge, w_out, norm_ffn, w_rg, b_rg, w_re, b_re_router,
           w_exp_gate, w_exp_up, w_exp_down, norm_ple, w_ple_gate, w_ple, norm_f):
    t, d = x.shape
    row = lambda v: v.reshape(1, -1).astype(F32)

    q, k, v, g, u = _proj(x, row(norm_mix), w_in.astype(BF16))
    cos_tab, sin_tab = _rope_tables(t)
    y_ret = _retention(q, k, v, g, cos_tab, sin_tab, row(ret_gn_w))

    ops = _s5_operators(lam_re, lam_im, log_dt, b_re, b_im, c_re, c_im)
    y_s5 = _s5(u, *ops, tq=min(t, 4096))

    merged = _merge(x, row(norm_mix), y_ret, y_s5, u, row(ssm_d), w_glu.astype(BF16),
                    w_merge.astype(BF16), w_branch_a.astype(BF16), w_branch_b.astype(BF16))

    pad = ROUTER_LANES - N_GROUPS - N_EXPERTS
    w_router = jnp.concatenate([w_rg, w_re, jnp.zeros((d, pad), F32)], axis=1).astype(F32)
    w_router_hi = w_router.astype(BF16)
    w_router_lo = (w_router - w_router_hi.astype(F32)).astype(BF16)
    b_router = jnp.concatenate([b_rg, b_re_router, jnp.zeros((pad,), F32)]).reshape(1, ROUTER_LANES).astype(F32)
    x1, route_i, route_w = _outproj(merged, x, w_out.astype(BF16), row(norm_ffn),
                                    jnp.concatenate([w_router_hi, w_router_lo], axis=1), b_router)

    block_expert, n_used, row_tok, row_dst = _row_layout(route_i[:, :2])
    moe_rows = _moe(block_expert, n_used, row_tok, row_dst, x1, row(norm_ffn),
                    w_exp_gate, w_exp_up, w_exp_down)

    return _final(x1, moe_rows, route_w, p, row(norm_ple), w_ple_gate.astype(BF16), w_ple.astype(BF16),
                  row(norm_f))


def kernel(x, p, norm_mix, w_in, ret_gn_w, ssm_lam_re, ssm_lam_im, ssm_log_dt, ssm_b_re, ssm_b_im, ssm_c_re, ssm_c_im, ssm_d, w_glu, w_branch_a, w_branch_b, w_merge, w_out, norm_ffn, w_router_group, b_router_group, w_router_expert, b_router_expert, w_exp_gate, w_exp_up, w_exp_down, norm_ple, w_ple_gate, w_ple, norm_f):
    depth, bsz, seq, _ = p.shape
    assert depth == 1 and bsz == 1, "single layer, single sequence"
    out = _layer(x[0], p[0, 0], norm_mix[0], w_in[0], ret_gn_w[0], ssm_lam_re[0], ssm_lam_im[0],
                 ssm_log_dt[0], ssm_b_re[0], ssm_b_im[0], ssm_c_re[0], ssm_c_im[0], ssm_d[0], w_glu[0],
                 w_branch_a[0], w_branch_b[0], w_merge[0], w_out[0], norm_ffn[0], w_router_group[0],
                 b_router_group[0], w_router_expert[0], b_router_expert[0], w_exp_gate[0], w_exp_up[0],
                 w_exp_down[0], norm_ple[0], w_ple_gate[0], w_ple[0], norm_f)
    return out[None]
```

```python
import math

import jax
import jax.numpy as jnp
from jax import lax
from jax.experimental import pallas as pl
from jax.experimental.pallas import tpu as pltpu

F32 = jnp.float32
BF16 = jnp.bfloat16
I32 = jnp.int32
U32 = jnp.uint32

EPS = 1e-6
LANES = 128
SUBLANES = 8
CHUNK_LOG2 = 6
RET_HEADS = 8
HEAD_DIM = 128
ROPE_THETA = 10000.0
SSM_GROUPS = 64
SSM_GROUP_SIZE = 16
SSM_STATE = 64
S5_CHUNK = 16
S5_TILE_GROUPS = LANES // SSM_GROUP_SIZE
S5_TILES = SSM_GROUPS // S5_TILE_GROUPS
S5_HALF = S5_TILE_GROUPS * SSM_STATE
N_GROUPS = 4
EXPERTS_PER_GROUP = 8
N_EXPERTS = 32
ROUTER_LANES = LANES
MOE_ROWS = 256
PACK = 8

VMEM_LIMIT_BYTES = 56 * 1024 * 1024


def _params(sem):
    return pltpu.CompilerParams(dimension_semantics=sem, vmem_limit_bytes=VMEM_LIMIT_BYTES)


def _resident(shape):
    n = len(shape)
    return pl.BlockSpec(shape, lambda *_: (0,) * n, pipeline_mode=pl.Buffered(1))


def _rms(x, w):
    ms = jnp.mean(x * x, axis=-1, keepdims=True)
    return x * lax.rsqrt(ms + EPS) * w


def _dot(a, b):
    return jnp.dot(a, b, preferred_element_type=F32)


def _proj_kernel(x_ref, nw_ref, w_ref, q_ref, k_ref, v_ref, g_ref, u_ref):
    h = _rms(x_ref[...], nw_ref[...]).astype(BF16)
    for c, o_ref in enumerate((q_ref, k_ref, v_ref, g_ref, u_ref)):
        width = o_ref.shape[1]
        o_ref[...] = _dot(h, w_ref[:, c * width:(c + 1) * width])


def _proj(x, norm_w, w_in_bf16, tm=256):
    t, d = x.shape
    width = w_in_bf16.shape[1] // 5
    out = jax.ShapeDtypeStruct((t, width), F32)
    return pl.pallas_call(
        _proj_kernel,
        grid=(t // tm,),
        in_specs=[pl.BlockSpec((tm, d), lambda i: (i, 0)),
                  _resident((1, d)),
                  _resident(w_in_bf16.shape)],
        out_specs=[pl.BlockSpec((tm, width), lambda i: (i, 0))] * 5,
        out_shape=[out] * 5,
        compiler_params=_params(("parallel",)),
        name="proj",
    )(x, norm_w, w_in_bf16)


def _log_gamma(h):
    return math.log1p(-(2.0 ** (-5.0 - h)))


def _retention_kernel(q_ref, k_ref, v_ref, g_ref, cos_ref, sin_ref, gn_ref, o_ref,
                      state_ref, decay_ref, qdec_ref, kdec_ref):
    rows = q_ref.shape[0]

    @pl.when(pl.program_id(0) == 0)
    def _init():
        ti = lax.broadcasted_iota(I32, (rows, rows), 0)
        si = lax.broadcasted_iota(I32, (rows, rows), 1)
        visible = lax.shift_right_logical(si, CHUNK_LOG2) <= lax.shift_right_logical(ti, CHUNK_LOG2)
        dist = jnp.abs(ti - si).astype(F32)
        pos = lax.broadcasted_iota(I32, (rows, HEAD_DIM), 0).astype(F32)
        for h in range(RET_HEADS):
            lg = _log_gamma(h)
            decay_ref[h] = jnp.where(visible, jnp.exp(lg * dist), 0.0)
            qdec_ref[h] = jnp.exp(lg * (pos + 1.0))
            kdec_ref[h] = jnp.exp(lg * (rows - 1.0 - pos))
        state_ref[...] = jnp.zeros_like(state_ref)

    cos = cos_ref[...]
    sin = sin_ref[...]
    scale = HEAD_DIM ** -0.5
    for h in range(RET_HEADS):
        sl = slice(h * HEAD_DIM, (h + 1) * HEAD_DIM)
        qh = q_ref[:, sl]
        kh = k_ref[:, sl]
        vb = v_ref[:, sl].astype(BF16)
        qr = qh * cos + pltpu.roll(qh, HEAD_DIM // 2, 1) * sin
        kr = (kh * cos + pltpu.roll(kh, HEAD_DIM // 2, 1) * sin) * scale
        scores = lax.dot_general(qr.astype(BF16), kr.astype(BF16), (((1,), (1,)), ((), ())),
                                 preferred_element_type=F32) * decay_ref[h]
        state = state_ref[h]
        y = _dot(scores.astype(BF16), vb)
        y = y + _dot((qr * qdec_ref[h]).astype(BF16), state.astype(BF16))
        kv = lax.dot_general((kr * kdec_ref[h]).astype(BF16), vb, (((0,), (0,)), ((), ())),
                             preferred_element_type=F32)
        state_ref[h] = math.exp(_log_gamma(h) * rows) * state + kv
        mu = jnp.mean(y, axis=-1, keepdims=True)
        yc = y - mu
        var = jnp.mean(yc * yc, axis=-1, keepdims=True)
        yn = yc * lax.rsqrt(var + EPS) * gn_ref[:, sl]
        gh = g_ref[:, sl]
        o_ref[:, sl] = (gh * jax.nn.sigmoid(gh) * yn).astype(o_ref.dtype)


def _retention(q, k, v, g, cos_tab, sin_tab, gn_w, rows=256):
    t, width = q.shape
    tok = pl.BlockSpec((rows, width), lambda i: (i, 0))
    tab = pl.BlockSpec((rows, HEAD_DIM), lambda i: (i, 0))
    return pl.pallas_call(
        _retention_kernel,
        grid=(t // rows,),
        in_specs=[tok, tok, tok, tok, tab, tab, _resident((1, width))],
        out_specs=tok,
        out_shape=jax.ShapeDtypeStruct((t, width), BF16),
        scratch_shapes=[pltpu.VMEM((RET_HEADS, HEAD_DIM, HEAD_DIM), F32),
                        pltpu.VMEM((RET_HEADS, rows, rows), F32),
                        pltpu.VMEM((RET_HEADS, rows, HEAD_DIM), F32),
                        pltpu.VMEM((RET_HEADS, rows, HEAD_DIM), F32)],
        compiler_params=_params(("arbitrary",)),
        name="retention",
    )(q, k, v, g, cos_tab, sin_tab, gn_w)


def _cmul(ar, ai, br, bi):
    return ar * br - ai * bi, ar * bi + ai * br


def _s5_kernel(u_ref, m_ref, bst_ref, cout_ref, pw_ref, y_ref, a_ref, pr_ref, pi_ref, carry_ref):
    tq = u_ref.shape[0]
    nn = tq // S5_CHUNK
    nb = nn // SUBLANES
    half = S5_HALF

    @pl.when(pl.program_id(1) == 0)
    def _():
        carry_ref[...] = jnp.zeros_like(carry_ref)

    for l in range(S5_CHUNK):
        a_ref[:, l * LANES:(l + 1) * LANES] = u_ref[pl.ds(l, nn, stride=S5_CHUNK), :].astype(BF16)
    a = a_ref[...]
    yc = _dot(a, m_ref[...])
    s_in = _dot(a, bst_ref[...])
    sr, si = s_in[:, :half], s_in[:, half:]

    row = lax.broadcasted_iota(I32, (nn, half), 0)
    zr = jnp.where(row == 0, carry_ref[0:1, :], pltpu.roll(sr, 1, 0))
    zi = jnp.where(row == 0, carry_ref[1:2, :], pltpu.roll(si, 1, 0))
    in_block = jnp.bitwise_and(row, SUBLANES - 1)
    for d in (1, 2, 4):
        mr, mi = _cmul(pw_ref[0, d - 1:d, :], pw_ref[1, d - 1:d, :],
                       pltpu.roll(zr, d, 0), pltpu.roll(zi, d, 0))
        keep = in_block >= d
        zr = zr + jnp.where(keep, mr, 0.0)
        zi = zi + jnp.where(keep, mi, 0.0)
    pr_ref[...] = zr.reshape(nb, SUBLANES, half)
    pi_ref[...] = zi.reshape(nb, SUBLANES, half)

    wr, wi = pw_ref[0], pw_ref[1]

    def block_step(r, c):
        fr, fi = _cmul(wr, wi, c[0], c[1])
        nr = pr_ref[r] + fr
        ni = pi_ref[r] + fi
        pr_ref[r] = nr
        pi_ref[r] = ni
        last = slice(SUBLANES - 1, SUBLANES)
        return (jnp.broadcast_to(nr[last, :], (SUBLANES, half)),
                jnp.broadcast_to(ni[last, :], (SUBLANES, half)))

    zero = jnp.zeros((SUBLANES, half), F32)
    lr, li = lax.fori_loop(0, nb, block_step, (zero, zero))
    er, ei = _cmul(wr[0:1, :], wi[0:1, :], lr[0:1, :], li[0:1, :])
    carry_ref[0:1, :] = er + sr[nn - 1:nn, :]
    carry_ref[1:2, :] = ei + si[nn - 1:nn, :]

    p = jnp.concatenate([pr_ref[...].reshape(nn, half), pi_ref[...].reshape(nn, half)], axis=-1)
    yc = yc + _dot(p.astype(BF16), cout_ref[...])
    for l in range(S5_CHUNK):
        y_ref[pl.ds(l, nn, stride=S5_CHUNK), :] = yc[:, l * LANES:(l + 1) * LANES]


def _s5(u, m_op, bst_op, cout_op, pw_op, tq):
    t, width = u.shape
    nn = tq // S5_CHUNK
    kdim = S5_CHUNK * LANES

    def tile_block(*shape):
        n = len(shape)
        return pl.BlockSpec((None,) + shape, lambda o, i: (o,) + (0,) * n)

    tok = pl.BlockSpec((tq, LANES), lambda o, i: (i, o))
    return pl.pallas_call(
        _s5_kernel,
        grid=(width // LANES, t // tq),
        in_specs=[tok, tile_block(kdim, kdim), tile_block(kdim, 2 * S5_HALF),
                  tile_block(2 * S5_HALF, kdim), tile_block(2, SUBLANES, S5_HALF)],
        out_specs=tok,
        out_shape=jax.ShapeDtypeStruct((t, width), F32),
        scratch_shapes=[pltpu.VMEM((nn, kdim), BF16),
                        pltpu.VMEM((nn // SUBLANES, SUBLANES, S5_HALF), F32),
                        pltpu.VMEM((nn // SUBLANES, SUBLANES, S5_HALF), F32),
                        pltpu.VMEM((2, S5_HALF), F32)],
        compiler_params=_params(("parallel", "arbitrary")),
        name="s5",
    )(u, m_op, bst_op, cout_op, pw_op)


def _s5_operators(lam_re, lam_im, log_dt, b_re, b_im, c_re, c_im):
    g, p, cg = b_re.shape
    lc, nt, tg = S5_CHUNK, S5_TILES, S5_TILE_GROUPS
    lam = lax.complex(jnp.minimum(lam_re, -1e-4), lam_im)
    log_lam_bar = lam * jnp.exp(log_dt)[:, None]
    lam_bar = jnp.exp(log_lam_bar)
    b_bar = ((lam_bar - 1.0) / lam)[..., None] * lax.complex(b_re, b_im)
    c_mat = lax.complex(c_re, c_im)

    def power(n):
        n = jnp.asarray(n, F32)
        return jnp.exp(n[:, None, None].astype(jnp.complex64) * log_lam_bar)

    lags = jnp.arange(lc)
    taps = jnp.real(jnp.einsum('gap,dgp,gpc->dgac', c_mat, power(lags), b_bar,
                               precision=lax.Precision.HIGHEST))
    taps = taps.reshape(lc, nt, tg, cg, cg).transpose(1, 2, 4, 0, 3).reshape(nt, tg * cg, lc * cg)
    taps = jnp.pad(taps, ((0, 0), (0, 0), ((lc - 1) * cg, 0)))
    kc = jnp.stack([taps[:, :, (lc - 1 - l) * cg:(2 * lc - 1 - l) * cg] for l in range(lc)], axis=1)
    kc = kc.reshape(nt, lc * tg * cg, lc * cg)

    bst = power(lc - 1 - lags)[:, :, None, :] * b_bar.transpose(0, 2, 1)[None]
    bst = bst.reshape(lc, nt, tg * cg, p).transpose(1, 0, 2, 3).reshape(nt, lc * tg * cg, p)
    bc = jnp.concatenate([jnp.real(bst), jnp.imag(bst)], axis=-1)

    pw_out = jnp.exp((lags + 1).astype(jnp.complex64)[None, None, :] * log_lam_bar[:, :, None])
    cout = pw_out[:, :, :, None] * c_mat.transpose(0, 2, 1)[:, :, None, :]
    cout = cout.reshape(nt, tg * p, lc * cg)
    cc = jnp.concatenate([jnp.real(cout), -jnp.imag(cout)], axis=1)

    def replicate(n_in, size):
        i = jnp.arange(n_in)[:, None]
        j = jnp.arange(n_in * tg)[None, :]
        return jnp.logical_and(i // size == j // (tg * size), i % size == j % size).astype(BF16)

    def spread(compact, size_in, row_size, col_size):
        full = jnp.einsum('trk,kc->trc', compact.astype(BF16), replicate(compact.shape[2], size_in),
                          preferred_element_type=F32)
        row_group = (jnp.arange(full.shape[1]) // row_size) % tg
        col_group = (jnp.arange(full.shape[2]) // col_size) % tg
        return jnp.where(row_group[:, None] == col_group[None, :], full, 0.0).astype(BF16)

    m_op = spread(kc, cg, cg, cg)
    bst_op = spread(bc, p, cg, p)
    cout_op = spread(cc, cg, p, cg)

    pw = power(lc * (1 + jnp.arange(SUBLANES)))
    pw = pw.reshape(SUBLANES, nt, tg * p).transpose(1, 0, 2)
    pw_op = jnp.stack([jnp.real(pw), jnp.imag(pw)], axis=1).astype(F32)
    return m_op, bst_op, cout_op, pw_op


def _merge_kernel(x_ref, nw_ref, yret_ref, ys_ref, u_ref, d_ref, wglu_ref, wm_ref, wa_ref, wb_ref,
                  o_ref):
    d_model = x_ref.shape[1]
    h = _rms(x_ref[...], nw_ref[...]).astype(BF16)
    y = jax.nn.gelu(ys_ref[...] + d_ref[...] * u_ref[...])
    y_ssm = (y * jax.nn.sigmoid(_dot(y.astype(BF16), wglu_ref[...]))).astype(BF16)
    y_ret = yret_ref[...]
    cw = 512
    for c in range(d_model // cw):
        cs = slice(c * cw, (c + 1) * cw)
        gs = slice(d_model + c * cw, d_model + (c + 1) * cw)
        g0 = jax.nn.sigmoid(_dot(h, wm_ref[:, cs]))
        g1 = jax.nn.sigmoid(_dot(h, wm_ref[:, gs]))
        o_ref[:, cs] = (g0 * _dot(y_ret, wa_ref[:, cs]) + g1 * _dot(y_ssm, wb_ref[:, cs])).astype(o_ref.dtype)


def _merge(x, norm_w, y_ret, y_s5, u, ssm_d, w_glu, w_merge, w_a, w_b, tm=256):
    t, d = x.shape
    width = y_ret.shape[1]
    tok = lambda w: pl.BlockSpec((tm, w), lambda i: (i, 0))
    return pl.pallas_call(
        _merge_kernel,
        grid=(t // tm,),
        in_specs=[tok(d), _resident((1, d)), tok(width), tok(width), tok(width), _resident((1, width)),
                  _resident(w_glu.shape), _resident(w_merge.shape), _resident(w_a.shape),
                  _resident(w_b.shape)],
        out_specs=tok(d),
        out_shape=jax.ShapeDtypeStruct((t, d), BF16),
        compiler_params=_params(("parallel",)),
        name="merge",
    )(x, norm_w, y_ret, y_s5, u, ssm_d, w_glu, w_merge, w_a, w_b)


def _first_argmax(vals, lane):
    top = jnp.max(vals, axis=-1, keepdims=True)
    idx = jnp.min(jnp.where(vals == top, lane, ROUTER_LANES), axis=-1, keepdims=True)
    return top, idx


def _bf16_bits(x):
    return lax.bitcast_convert_type(x.astype(BF16).astype(F32), U32)


def _packed_store(ref, value, lead=()):
    rows, d = value.shape
    for i in range(PACK):
        lo = lax.shift_right_logical(_bf16_bits(value[:, i * LANES:(i + 1) * LANES]), U32(16))
        hi = _bf16_bits(value[:, d // 2 + i * LANES:d // 2 + (i + 1) * LANES])
        ref[lead + (pl.ds(i, rows, stride=PACK), slice(None))] = jnp.bitwise_or(hi, lo)


def _packed_pieces(ref, i, rows, lead=()):
    word = ref[lead + (pl.ds(i, rows, stride=PACK), slice(None))]
    lo = lax.bitcast_convert_type(lax.shift_left(word, U32(16)), F32)
    hi = lax.bitcast_convert_type(jnp.bitwise_and(word, U32(0xFFFF0000)), F32)
    return lo, hi


def _outproj_kernel(m_ref, x_ref, wo_ref, nw_ref, wr_ref, br_ref, x1_ref, xp_ref, ri_ref, rw_ref, cnt_ref):
    x1 = x_ref[...] + _dot(m_ref[...], wo_ref[...])
    x1_ref[...] = x1
    _packed_store(xp_ref, x1)
    h2 = _rms(x1, nw_ref[...])
    h_hi = h2.astype(BF16)
    h_lo = (h2 - h_hi.astype(F32)).astype(BF16)
    part = _dot(h_hi, wr_ref[...])
    logits = (part[:, :ROUTER_LANES] + part[:, ROUTER_LANES:]
              + _dot(h_lo, wr_ref[:, :ROUTER_LANES]) + br_ref[...])

    lane = lax.broadcasted_iota(I32, logits.shape, 1)
    neg = -jnp.inf
    is_group = lane < N_GROUPS
    g_top, g_idx = _first_argmax(jnp.where(is_group, logits, neg), lane)
    g_w = 1.0 / jnp.sum(jnp.where(is_group, jnp.exp(logits - g_top), 0.0), axis=-1, keepdims=True)
    first = N_GROUPS + EXPERTS_PER_GROUP * g_idx
    e_logits = jnp.where(lane >= first, jnp.where(lane < first + EXPERTS_PER_GROUP, logits, neg), neg)
    t1, i1 = _first_argmax(e_logits, lane)
    t2, i2 = _first_argmax(jnp.where(lane == i1, neg, e_logits), lane)
    ratio = jnp.exp(t2 - t1)
    w1 = g_w / (1.0 + ratio)
    w2 = w1 * ratio
    e1 = i1 - N_GROUPS
    e2 = i2 - N_GROUPS
    ri_ref[...] = jnp.where(lane == 0, e1, jnp.where(lane == 1, e2, 0))
    rw_ref[...] = jnp.where(lane == 0, w1, jnp.where(lane == 1, w2, 0.0))

    @pl.when(pl.program_id(0) == 0)
    def _():
        cnt_ref[...] = jnp.zeros_like(cnt_ref)

    picked = jnp.where(lane == e1, 1.0, 0.0) + jnp.where(lane == e2, 1.0, 0.0)
    cnt_ref[...] += jnp.sum(picked, axis=0, keepdims=True)


def _outproj(merged, x, w_out, norm_w, w_router, b_router, tm=256):
    t, d = x.shape
    tok = lambda w: pl.BlockSpec((tm, w), lambda i: (i, 0))
    return pl.pallas_call(
        _outproj_kernel,
        grid=(t // tm,),
        in_specs=[tok(d), tok(d), _resident(w_out.shape), _resident((1, d)),
                  _resident(w_router.shape), _resident((1, ROUTER_LANES))],
        out_specs=[tok(d), pl.BlockSpec((tm * PACK, LANES), lambda i: (i, 0)), tok(ROUTER_LANES),
                   tok(ROUTER_LANES), pl.BlockSpec((1, ROUTER_LANES), lambda i: (0, 0))],
        out_shape=[jax.ShapeDtypeStruct((t, d), F32),
                   jax.ShapeDtypeStruct((t * PACK, LANES), U32),
                   jax.ShapeDtypeStruct((t, ROUTER_LANES), I32),
                   jax.ShapeDtypeStruct((t, ROUTER_LANES), F32),
                   jax.ShapeDtypeStruct((1, ROUTER_LANES), F32)],
        compiler_params=_params(("arbitrary",)),
        name="outproj",
    )(merged, x, w_out, norm_w, w_router, b_router)


def _moe_kernel(be_ref, nu_ref, tok_ref, dst_ref,
                x_hbm, nw_ref, wg_ref, wu_ref, wd_ref,
                out_hbm,
                xbuf, ybuf, zbuf, h_ref, wgu_bf, wd_bf, gsem, ssem):
    s = pl.program_id(0)
    n_used = nu_ref[0]
    rows, d_model = h_ref.shape
    f = wg_ref.shape[1]
    slot = lax.rem(s, 2)
    other = 1 - slot

    def gather_row(blk, buf_slot, r):
        tok = pl.multiple_of(tok_ref[blk * rows + r], PACK)
        return pltpu.make_async_copy(x_hbm.at[pl.ds(tok, PACK)], xbuf.at[buf_slot, pl.ds(r * PACK, PACK)],
                                     gsem.at[buf_slot])

    def scatter_row(blk, buf_slot, r):
        dst = pl.multiple_of(dst_ref[(blk + 1) * rows + r], PACK)
        return pltpu.make_async_copy(ybuf.at[buf_slot, pl.ds(r * PACK, PACK)], out_hbm.at[pl.ds(dst, PACK)],
                                     ssem.at[buf_slot])

    def wait_slot(buf, sem, buf_slot):
        pltpu.make_async_copy(buf.at[buf_slot], buf.at[buf_slot], sem.at[buf_slot]).wait()

    @pl.when(s == 0)
    def _prologue():
        ybuf[...] = jnp.zeros_like(ybuf)
        zbuf[...] = jnp.zeros_like(zbuf)
        for r in range(rows):
            gather_row(0, 0, r).start()

    @pl.when(jnp.logical_and(s < n_used,
                             jnp.logical_or(s == 0, be_ref[s] != be_ref[jnp.maximum(s - 1, 0)])))
    def _cast_weights():
        wgu_bf[:, :f] = wg_ref[...].astype(BF16)
        wgu_bf[:, f:] = wu_ref[...].astype(BF16)
        wd_bf[...] = wd_ref[...].astype(BF16)

    @pl.when(s < n_used)
    def _block():
        wait_slot(xbuf, gsem, slot)
        for r in range(rows):
            scatter_row(s - 1, other, r).start()
        ss = jnp.zeros((rows, 1), F32)
        for i in range(PACK):
            lo, hi = _packed_pieces(xbuf, i, rows, (slot,))
            ss = ss + jnp.sum(lo * lo + hi * hi, axis=-1, keepdims=True)
        inv = lax.rsqrt(ss * (1.0 / d_model) + EPS)
        for i in range(PACK):
            lo, hi = _packed_pieces(xbuf, i, rows, (slot,))
            lo_cols = slice(i * LANES, (i + 1) * LANES)
            hi_cols = slice(d_model // 2 + i * LANES, d_model // 2 + (i + 1) * LANES)
            h_ref[:, lo_cols] = (lo * inv * nw_ref[:, lo_cols]).astype(BF16)
            h_ref[:, hi_cols] = (hi * inv * nw_ref[:, hi_cols]).astype(BF16)
        for r in range(rows):
            gather_row(s + 1, other, r).start()
        gu = _dot(h_ref[...], wgu_bf[...])
        gate = gu[:, :f]
        act = (gate * jax.nn.sigmoid(gate) * gu[:, f:]).astype(BF16)
        y = _dot(act, wd_bf[...])

        @pl.when(s >= 1)
        def _free_slot():
            wait_slot(ybuf, ssem, slot)

        _packed_store(ybuf, y, (slot,))

    @pl.when(s >= n_used)
    def _tail():
        wait_slot(ybuf, ssem, slot)

        @pl.when(s == n_used)
        def _last_block():
            wait_slot(xbuf, gsem, slot)
            for r in range(rows):
                scatter_row(s - 1, other, r).start()

        @pl.when(s > n_used)
        def _padding_block():
            dst = pl.multiple_of(dst_ref[s * rows], PACK)
            pltpu.make_async_copy(zbuf, out_hbm.at[pl.ds(dst, rows * PACK)], ssem.at[other]).start()

    @pl.when(s == pl.num_programs(0) - 1)
    def _drain():
        wait_slot(ybuf, ssem, other)


def _moe(block_expert, n_used, row_tok, row_dst, x1_packed, norm_w, w_gate, w_up, w_down):
    n_steps = block_expert.shape[0]
    _, d, f = w_gate.shape
    assert d == 2 * PACK * LANES

    def expert_block(shape):
        return pl.BlockSpec((None,) + shape, lambda b, be, *_: (be[b], 0, 0))

    grid_spec = pltpu.PrefetchScalarGridSpec(
        num_scalar_prefetch=4,
        grid=(n_steps,),
        in_specs=[pl.BlockSpec(memory_space=pl.ANY),
                  pl.BlockSpec((1, d), lambda b, *_: (0, 0)),
                  expert_block((d, f)), expert_block((d, f)), expert_block((f, d))],
        out_specs=pl.BlockSpec(memory_space=pl.ANY),
        scratch_shapes=[pltpu.VMEM((2, MOE_ROWS * PACK, LANES), U32),
                        pltpu.VMEM((2, MOE_ROWS * PACK, LANES), U32),
                        pltpu.VMEM((MOE_ROWS * PACK, LANES), U32),
                        pltpu.VMEM((MOE_ROWS, d), BF16),
                        pltpu.VMEM((d, 2 * f), BF16),
                        pltpu.VMEM((f, d), BF16),
                        pltpu.SemaphoreType.DMA((2,)),
                        pltpu.SemaphoreType.DMA((2,))],
    )
    return pl.pallas_call(
        _moe_kernel,
        grid_spec=grid_spec,
        out_shape=jax.ShapeDtypeStruct((row_dst.shape[0] * PACK, LANES), U32),
        compiler_params=pltpu.CompilerParams(dimension_semantics=("arbitrary",),
                                             vmem_limit_bytes=VMEM_LIMIT_BYTES,
                                             has_side_effects=True),
        name="moe",
    )(block_expert, n_used, row_tok * PACK, row_dst * PACK, x1_packed, norm_w, w_gate, w_up, w_down)


def _row_layout(expert, counts):
    t = expert.shape[0]
    n_assign = 2 * t
    n_blocks = n_assign // MOE_ROWS + N_EXPERTS
    flat_e = expert.reshape(-1)
    order = jnp.argsort(flat_e).astype(I32)
    start = jnp.cumsum(counts) - counts
    padded = (counts + MOE_ROWS - 1) // MOE_ROWS * MOE_ROWS
    pad_end = jnp.cumsum(padded)
    pad_start = pad_end - padded
    block_start = jnp.arange(n_blocks, dtype=I32) * MOE_ROWS
    used = block_start < pad_end[-1]
    block_expert = jnp.minimum(jnp.sum(pad_end[None, :] <= block_start[:, None], axis=1, dtype=I32),
                               N_EXPERTS - 1)
    last_expert = block_expert[jnp.maximum(pad_end[-1] // MOE_ROWS - 1, 0)]
    block_expert = jnp.where(used, block_expert, last_expert)
    rank0 = block_start - pad_start[block_expert]
    n_valid = jnp.where(used, jnp.clip(counts[block_expert] - rank0, 0, MOE_ROWS), 0).astype(I32)
    within = jnp.arange(MOE_ROWS, dtype=I32)[None, :]
    valid = within < n_valid[:, None]
    src = order[jnp.clip(start[block_expert][:, None] + rank0[:, None] + within, 0, n_assign - 1)]
    tok = src // 2
    n_pad = MOE_ROWS - n_valid
    spare = n_assign + MOE_ROWS + (jnp.cumsum(n_pad) - n_pad)[:, None] + (within - n_valid[:, None])
    row_dst = jnp.where(valid, (src % 2) * t + tok, spare).astype(I32)
    row_dst = jnp.concatenate([n_assign + within, row_dst], axis=0).reshape(-1)
    row_tok = jnp.concatenate([jnp.where(valid, tok, 0), jnp.zeros((2, MOE_ROWS), I32)], axis=0).reshape(-1)
    block_expert = jnp.concatenate([block_expert, last_expert[None]])
    n_used = (pad_end[-1] // MOE_ROWS).astype(I32).reshape(1)
    return block_expert, n_used, row_tok, row_dst


def _final_kernel(x1_ref, r0_ref, r1_ref, rw_ref, p_ref, nple_ref, wg_ref, wp_ref, nf_ref, o_ref, x2_ref):
    tm, d = x2_ref.shape
    w0, w1 = rw_ref[:, 0:1], rw_ref[:, 1:2]
    for i in range(PACK):
        lo0, hi0 = _packed_pieces(r0_ref, i, tm)
        lo1, hi1 = _packed_pieces(r1_ref, i, tm)
        lo_cols = slice(i * LANES, (i + 1) * LANES)
        hi_cols = slice(d // 2 + i * LANES, d // 2 + (i + 1) * LANES)
        x2_ref[:, lo_cols] = x1_ref[:, lo_cols] + (w0 * lo0 + w1 * lo1)
        x2_ref[:, hi_cols] = x1_ref[:, hi_cols] + (w0 * hi0 + w1 * hi1)
    x2 = x2_ref[...]
    h3 = _rms(x2, nple_ref[...]).astype(BF16)
    gate = jax.nn.sigmoid(_dot(h3, wg_ref[...]))
    x3 = x2 + gate * _dot(p_ref[...].astype(BF16), wp_ref[...])
    o_ref[...] = _rms(x3, nf_ref[...])


def _final(x1, moe_packed, route_w, p, norm_ple, w_ple_gate, w_ple, norm_f, tm=256):
    t, d = x1.shape
    nt = t // tm
    tok = lambda w: pl.BlockSpec((tm, w), lambda i: (i, 0))
    packed = lambda first: pl.BlockSpec((tm * PACK, LANES), lambda i: (i + first, 0))
    return pl.pallas_call(
        _final_kernel,
        grid=(nt,),
        in_specs=[tok(d), packed(0), packed(nt), tok(ROUTER_LANES),
                  tok(p.shape[1]), _resident((1, d)), _resident(w_ple_gate.shape),
                  _resident(w_ple.shape), _resident((1, d))],
        out_specs=tok(d),
        out_shape=jax.ShapeDtypeStruct((t, d), F32),
        scratch_shapes=[pltpu.VMEM((tm, d), F32)],
        compiler_params=_params(("parallel",)),
        name="final",
    )(x1, moe_packed, moe_packed, route_w, p, norm_ple, w_ple_gate, w_ple, norm_f)


def _rope_tables(t):
    half = HEAD_DIM // 2
    inv_freq = ROPE_THETA ** (-jnp.arange(half, dtype=F32) / half)
    ang = jnp.arange(t, dtype=F32)[:, None] * inv_freq[None, :]
    cos, sin = jnp.cos(ang), jnp.sin(ang)
    return jnp.concatenate([cos, cos], axis=-1), jnp.concatenate([-sin, sin], axis=-1)


def _layer(x, p, norm_mix, w_in, ret_gn_w, lam_re, lam_im, log_dt, b_re, b_im, c_re, c_im, ssm_d,
           w_glu, w_branch_a, w_branch_b, w_merge, w_out, norm_ffn, w_rg, b_rg, w_re, b_re_router,
           w_exp_gate, w_exp_up, w_exp_down, norm_ple, w_ple_gate, w_ple, norm_f):
    t, d = x.shape
    row = lambda v: v.reshape(1, -1).astype(F32)

    q, k, v, g, u = _proj(x, row(norm_mix), w_in.astype(BF16))
    cos_tab, sin_tab = _rope_tables(t)
    y_ret = _retention(q, k, v, g, cos_tab, sin_tab, row(ret_gn_w))

    ops = _s5_operators(lam_re, lam_im, log_dt, b_re, b_im, c_re, c_im)
    y_s5 = _s5(u, *ops, tq=min(t, 4096))

    merged = _merge(x, row(norm_mix), y_ret, y_s5, u, row(ssm_d), w_glu.astype(BF16),
                    w_merge.astype(BF16), w_branch_a.astype(BF16), w_branch_b.astype(BF16))

    pad = ROUTER_LANES - N_GROUPS - N_EXPERTS
    w_router = jnp.concatenate([w_rg, w_re, jnp.zeros((d, pad), F32)], axis=1).astype(F32)
    w_router_hi = w_router.astype(BF16)
    w_router_lo = (w_router - w_router_hi.astype(F32)).astype(BF16)
    b_router = jnp.concatenate([b_rg, b_re_router, jnp.zeros((pad,), F32)]).reshape(1, ROUTER_LANES).astype(F32)
    x1, x1_packed, route_i, route_w, counts = _outproj(
        merged, x, w_out.astype(BF16), row(norm_ffn),
        jnp.concatenate([w_router_hi, w_router_lo], axis=1), b_router)

    block_expert, n_used, row_tok, row_dst = _row_layout(route_i[:, :2], counts[0, :N_EXPERTS].astype(I32))
    moe_rows = _moe(block_expert, n_used, row_tok, row_dst, x1_packed, row(norm_ffn),
                    w_exp_gate, w_exp_up, w_exp_down)

    return _final(x1, moe_rows, route_w, p, row(norm_ple), w_ple_gate.astype(BF16), w_ple.astype(BF16),
                  row(norm_f))


def kernel(x, p, norm_mix, w_in, ret_gn_w, ssm_lam_re, ssm_lam_im, ssm_log_dt, ssm_b_re, ssm_b_im, ssm_c_re, ssm_c_im, ssm_d, w_glu, w_branch_a, w_branch_b, w_merge, w_out, norm_ffn, w_router_group, b_router_group, w_router_expert, b_router_expert, w_exp_gate, w_exp_up, w_exp_down, norm_ple, w_ple_gate, w_ple, norm_f):
    depth, bsz, seq, _ = p.shape
    assert depth == 1 and bsz == 1, "single layer, single sequence"
    out = _layer(x[0], p[0, 0], norm_mix[0], w_in[0], ret_gn_w[0], ssm_lam_re[0], ssm_lam_im[0],
                 ssm_log_dt[0], ssm_b_re[0], ssm_b_im[0], ssm_c_re[0], ssm_c_im[0], ssm_d[0], w_glu[0],
                 w_branch_a[0], w_branch_b[0], w_merge[0], w_out[0], norm_ffn[0], w_router_group[0],
                 b_router_group[0], w_router_expert[0], b_router_expert[0], w_exp_gate[0], w_exp_up[0],
                 w_exp_down[0], norm_ple[0], w_ple_gate[0], w_ple[0], norm_f)
    return out[None]
```

```python
import math

import jax
import jax.numpy as jnp
from jax import lax
from jax.experimental import pallas as pl
from jax.experimental.pallas import tpu as pltpu

F32 = jnp.float32
BF16 = jnp.bfloat16
I32 = jnp.int32
U32 = jnp.uint32

EPS = 1e-6
LANES = 128
SUBLANES = 8
CHUNK_LOG2 = 6
RET_HEADS = 8
HEAD_DIM = 128
ROPE_THETA = 10000.0
SSM_GROUPS = 64
SSM_GROUP_SIZE = 16
SSM_STATE = 64
S5_CHUNK = 16
S5_TILE_GROUPS = LANES // SSM_GROUP_SIZE
S5_TILES = SSM_GROUPS // S5_TILE_GROUPS
S5_HALF = S5_TILE_GROUPS * SSM_STATE
N_GROUPS = 4
EXPERTS_PER_GROUP = 8
N_EXPERTS = 32
ROUTER_LANES = LANES
MOE_ROWS = 256
PACK = 8

VMEM_LIMIT_BYTES = 56 * 1024 * 1024


def _params(sem):
    return pltpu.CompilerParams(dimension_semantics=sem, vmem_limit_bytes=VMEM_LIMIT_BYTES)


def _resident(shape):
    n = len(shape)
    return pl.BlockSpec(shape, lambda *_: (0,) * n, pipeline_mode=pl.Buffered(1))


def _rms(x, w):
    ms = jnp.mean(x * x, axis=-1, keepdims=True)
    return x * lax.rsqrt(ms + EPS) * w


def _dot(a, b):
    return jnp.dot(a, b, preferred_element_type=F32)


def _proj_kernel(x_ref, nw_ref, w_ref, q_ref, k_ref, v_ref, g_ref, u_ref):
    h = _rms(x_ref[...], nw_ref[...]).astype(BF16)
    for c, o_ref in enumerate((q_ref, k_ref, v_ref, g_ref, u_ref)):
        width = o_ref.shape[1]
        o_ref[...] = _dot(h, w_ref[:, c * width:(c + 1) * width])


def _proj(x, norm_w, w_in_bf16, tm=256):
    t, d = x.shape
    width = w_in_bf16.shape[1] // 5
    out = jax.ShapeDtypeStruct((t, width), F32)
    return pl.pallas_call(
        _proj_kernel,
        grid=(t // tm,),
        in_specs=[pl.BlockSpec((tm, d), lambda i: (i, 0)),
                  _resident((1, d)),
                  _resident(w_in_bf16.shape)],
        out_specs=[pl.BlockSpec((tm, width), lambda i: (i, 0))] * 5,
        out_shape=[out] * 5,
        compiler_params=_params(("parallel",)),
        name="proj",
    )(x, norm_w, w_in_bf16)


def _log_gamma(h):
    return math.log1p(-(2.0 ** (-5.0 - h)))


def _retention_kernel(q_ref, k_ref, v_ref, g_ref, cos_ref, sin_ref, gn_ref, o_ref,
                      state_ref, decay_ref, qdec_ref, kdec_ref):
    rows = q_ref.shape[0]

    @pl.when(pl.program_id(0) == 0)
    def _init():
        ti = lax.broadcasted_iota(I32, (rows, rows), 0)
        si = lax.broadcasted_iota(I32, (rows, rows), 1)
        visible = lax.shift_right_logical(si, CHUNK_LOG2) <= lax.shift_right_logical(ti, CHUNK_LOG2)
        dist = jnp.abs(ti - si).astype(F32)
        pos = lax.broadcasted_iota(I32, (rows, HEAD_DIM), 0).astype(F32)
        for h in range(RET_HEADS):
            lg = _log_gamma(h)
            decay_ref[h] = jnp.where(visible, jnp.exp(lg * dist), 0.0)
            qdec_ref[h] = jnp.exp(lg * (pos + 1.0))
            kdec_ref[h] = jnp.exp(lg * (rows - 1.0 - pos))
        state_ref[...] = jnp.zeros_like(state_ref)

    cos = cos_ref[...]
    sin = sin_ref[...]
    scale = HEAD_DIM ** -0.5
    for h in range(RET_HEADS):
        sl = slice(h * HEAD_DIM, (h + 1) * HEAD_DIM)
        qh = q_ref[:, sl]
        kh = k_ref[:, sl]
        vb = v_ref[:, sl].astype(BF16)
        qr = qh * cos + pltpu.roll(qh, HEAD_DIM // 2, 1) * sin
        kr = (kh * cos + pltpu.roll(kh, HEAD_DIM // 2, 1) * sin) * scale
        scores = lax.dot_general(qr.astype(BF16), kr.astype(BF16), (((1,), (1,)), ((), ())),
                                 preferred_element_type=F32) * decay_ref[h]
        state = state_ref[h]
        y = _dot(scores.astype(BF16), vb)
        y = y + _dot((qr * qdec_ref[h]).astype(BF16), state.astype(BF16))
        kv = lax.dot_general((kr * kdec_ref[h]).astype(BF16), vb, (((0,), (0,)), ((), ())),
                             preferred_element_type=F32)
        state_ref[h] = math.exp(_log_gamma(h) * rows) * state + kv
        mu = jnp.mean(y, axis=-1, keepdims=True)
        yc = y - mu
        var = jnp.mean(yc * yc, axis=-1, keepdims=True)
        yn = yc * lax.rsqrt(var + EPS) * gn_ref[:, sl]
        gh = g_ref[:, sl]
        o_ref[:, sl] = (gh * jax.nn.sigmoid(gh) * yn).astype(o_ref.dtype)


def _retention(q, k, v, g, cos_tab, sin_tab, gn_w, rows=256):
    t, width = q.shape
    tok = pl.BlockSpec((rows, width), lambda i: (i, 0))
    tab = pl.BlockSpec((rows, HEAD_DIM), lambda i: (i, 0))
    return pl.pallas_call(
        _retention_kernel,
        grid=(t // rows,),
        in_specs=[tok, tok, tok, tok, tab, tab, _resident((1, width))],
        out_specs=tok,
        out_shape=jax.ShapeDtypeStruct((t, width), BF16),
        scratch_shapes=[pltpu.VMEM((RET_HEADS, HEAD_DIM, HEAD_DIM), F32),
                        pltpu.VMEM((RET_HEADS, rows, rows), F32),
                        pltpu.VMEM((RET_HEADS, rows, HEAD_DIM), F32),
                        pltpu.VMEM((RET_HEADS, rows, HEAD_DIM), F32)],
        compiler_params=_params(("arbitrary",)),
        name="retention",
    )(q, k, v, g, cos_tab, sin_tab, gn_w)


def _cmul(ar, ai, br, bi):
    return ar * br - ai * bi, ar * bi + ai * br


def _s5_kernel(u_ref, m_ref, bst_ref, cout_ref, pw_ref, y_ref, a_ref, pr_ref, pi_ref, carry_ref):
    tq = u_ref.shape[0]
    nn = tq // S5_CHUNK
    nb = nn // SUBLANES
    half = S5_HALF

    @pl.when(pl.program_id(1) == 0)
    def _():
        carry_ref[...] = jnp.zeros_like(carry_ref)

    for l in range(S5_CHUNK):
        a_ref[:, l * LANES:(l + 1) * LANES] = u_ref[pl.ds(l, nn, stride=S5_CHUNK), :].astype(BF16)
    a = a_ref[...]
    kt = 2 * LANES
    yc = [_dot(a_ref[:, :(n + 1) * kt], m_ref[:(n + 1) * kt, n * kt:(n + 1) * kt])
          for n in range(S5_CHUNK // 2)]
    s_in = _dot(a, bst_ref[...])
    sr, si = s_in[:, :half], s_in[:, half:]

    row = lax.broadcasted_iota(I32, (nn, half), 0)
    zr = jnp.where(row == 0, carry_ref[0:1, :], pltpu.roll(sr, 1, 0))
    zi = jnp.where(row == 0, carry_ref[1:2, :], pltpu.roll(si, 1, 0))
    in_block = jnp.bitwise_and(row, SUBLANES - 1)
    for d in (1, 2, 4):
        mr, mi = _cmul(pw_ref[0, d - 1:d, :], pw_ref[1, d - 1:d, :],
                       pltpu.roll(zr, d, 0), pltpu.roll(zi, d, 0))
        keep = in_block >= d
        zr = zr + jnp.where(keep, mr, 0.0)
        zi = zi + jnp.where(keep, mi, 0.0)
    pr_ref[...] = zr.reshape(nb, SUBLANES, half)
    pi_ref[...] = zi.reshape(nb, SUBLANES, half)

    wr, wi = pw_ref[0], pw_ref[1]

    def block_step(r, c):
        fr, fi = _cmul(wr, wi, c[0], c[1])
        nr = pr_ref[r] + fr
        ni = pi_ref[r] + fi
        pr_ref[r] = nr
        pi_ref[r] = ni
        last = slice(SUBLANES - 1, SUBLANES)
        return (jnp.broadcast_to(nr[last, :], (SUBLANES, half)),
                jnp.broadcast_to(ni[last, :], (SUBLANES, half)))

    zero = jnp.zeros((SUBLANES, half), F32)
    lr, li = lax.fori_loop(0, nb, block_step, (zero, zero))
    er, ei = _cmul(wr[0:1, :], wi[0:1, :], lr[0:1, :], li[0:1, :])
    carry_ref[0:1, :] = er + sr[nn - 1:nn, :]
    carry_ref[1:2, :] = ei + si[nn - 1:nn, :]

    p = jnp.concatenate([pr_ref[...].reshape(nn, half), pi_ref[...].reshape(nn, half)], axis=-1)
    y_cross = _dot(p.astype(BF16), cout_ref[...])
    for l in range(S5_CHUNK):
        y_intra = yc[l // 2][:, (l % 2) * LANES:(l % 2 + 1) * LANES]
        y_ref[pl.ds(l, nn, stride=S5_CHUNK), :] = y_intra + y_cross[:, l * LANES:(l + 1) * LANES]


def _s5(u, m_op, bst_op, cout_op, pw_op, tq):
    t, width = u.shape
    nn = tq // S5_CHUNK
    kdim = S5_CHUNK * LANES

    def tile_block(*shape):
        n = len(shape)
        return pl.BlockSpec((None,) + shape, lambda o, i: (o,) + (0,) * n)

    tok = pl.BlockSpec((tq, LANES), lambda o, i: (i, o))
    return pl.pallas_call(
        _s5_kernel,
        grid=(width // LANES, t // tq),
        in_specs=[tok, tile_block(kdim, kdim), tile_block(kdim, 2 * S5_HALF),
                  tile_block(2 * S5_HALF, kdim), tile_block(2, SUBLANES, S5_HALF)],
        out_specs=tok,
        out_shape=jax.ShapeDtypeStruct((t, width), F32),
        scratch_shapes=[pltpu.VMEM((nn, kdim), BF16),
                        pltpu.VMEM((nn // SUBLANES, SUBLANES, S5_HALF), F32),
                        pltpu.VMEM((nn // SUBLANES, SUBLANES, S5_HALF), F32),
                        pltpu.VMEM((2, S5_HALF), F32)],
        compiler_params=_params(("parallel", "arbitrary")),
        name="s5",
    )(u, m_op, bst_op, cout_op, pw_op)


def _s5_operators(lam_re, lam_im, log_dt, b_re, b_im, c_re, c_im):
    g, p, cg = b_re.shape
    lc, nt, tg = S5_CHUNK, S5_TILES, S5_TILE_GROUPS
    lam = lax.complex(jnp.minimum(lam_re, -1e-4), lam_im)
    log_lam_bar = lam * jnp.exp(log_dt)[:, None]
    lam_bar = jnp.exp(log_lam_bar)
    b_bar = ((lam_bar - 1.0) / lam)[..., None] * lax.complex(b_re, b_im)
    c_mat = lax.complex(c_re, c_im)

    def power(n):
        n = jnp.asarray(n, F32)
        return jnp.exp(n[:, None, None].astype(jnp.complex64) * log_lam_bar)

    lags = jnp.arange(lc)
    taps = jnp.real(jnp.einsum('gap,dgp,gpc->dgac', c_mat, power(lags), b_bar,
                               precision=lax.Precision.HIGHEST))
    taps = taps.reshape(lc, nt, tg, cg, cg).transpose(1, 2, 4, 0, 3).reshape(nt, tg * cg, lc * cg)
    taps = jnp.pad(taps, ((0, 0), (0, 0), ((lc - 1) * cg, 0)))
    kc = jnp.stack([taps[:, :, (lc - 1 - l) * cg:(2 * lc - 1 - l) * cg] for l in range(lc)], axis=1)
    kc = kc.reshape(nt, lc * tg * cg, lc * cg)

    bst = power(lc - 1 - lags)[:, :, None, :] * b_bar.transpose(0, 2, 1)[None]
    bst = bst.reshape(lc, nt, tg * cg, p).transpose(1, 0, 2, 3).reshape(nt, lc * tg * cg, p)
    bc = jnp.concatenate([jnp.real(bst), jnp.imag(bst)], axis=-1)

    pw_out = jnp.exp((lags + 1).astype(jnp.complex64)[None, None, :] * log_lam_bar[:, :, None])
    cout = pw_out[:, :, :, None] * c_mat.transpose(0, 2, 1)[:, :, None, :]
    cout = cout.reshape(nt, tg * p, lc * cg)
    cc = jnp.concatenate([jnp.real(cout), -jnp.imag(cout)], axis=1)

    def replicate(n_in, size):
        i = jnp.arange(n_in)[:, None]
        j = jnp.arange(n_in * tg)[None, :]
        return jnp.logical_and(i // size == j // (tg * size), i % size == j % size).astype(BF16)

    def spread(compact, size_in, row_size, col_size):
        full = jnp.einsum('trk,kc->trc', compact.astype(BF16), replicate(compact.shape[2], size_in),
                          preferred_element_type=F32)
        row_group = (jnp.arange(full.shape[1]) // row_size) % tg
        col_group = (jnp.arange(full.shape[2]) // col_size) % tg
        return jnp.where(row_group[:, None] == col_group[None, :], full, 0.0).astype(BF16)

    m_op = spread(kc, cg, cg, cg)
    bst_op = spread(bc, p, cg, p)
    cout_op = spread(cc, cg, p, cg)

    pw = power(lc * (1 + jnp.arange(SUBLANES)))
    pw = pw.reshape(SUBLANES, nt, tg * p).transpose(1, 0, 2)
    pw_op = jnp.stack([jnp.real(pw), jnp.imag(pw)], axis=1).astype(F32)
    return m_op, bst_op, cout_op, pw_op


def _merge_kernel(x_ref, nw_ref, yret_ref, ys_ref, u_ref, d_ref, wglu_ref, wm_ref, wa_ref, wb_ref,
                  o_ref):
    d_model = x_ref.shape[1]
    h = _rms(x_ref[...], nw_ref[...]).astype(BF16)
    y = jax.nn.gelu(ys_ref[...] + d_ref[...] * u_ref[...])
    y_ssm = (y * jax.nn.sigmoid(_dot(y.astype(BF16), wglu_ref[...]))).astype(BF16)
    y_ret = yret_ref[...]
    cw = 512
    for c in range(d_model // cw):
        cs = slice(c * cw, (c + 1) * cw)
        gs = slice(d_model + c * cw, d_model + (c + 1) * cw)
        g0 = jax.nn.sigmoid(_dot(h, wm_ref[:, cs]))
        g1 = jax.nn.sigmoid(_dot(h, wm_ref[:, gs]))
        o_ref[:, cs] = (g0 * _dot(y_ret, wa_ref[:, cs]) + g1 * _dot(y_ssm, wb_ref[:, cs])).astype(o_ref.dtype)


def _merge(x, norm_w, y_ret, y_s5, u, ssm_d, w_glu, w_merge, w_a, w_b, tm=256):
    t, d = x.shape
    width = y_ret.shape[1]
    tok = lambda w: pl.BlockSpec((tm, w), lambda i: (i, 0))
    return pl.pallas_call(
        _merge_kernel,
        grid=(t // tm,),
        in_specs=[tok(d), _resident((1, d)), tok(width), tok(width), tok(width), _resident((1, width)),
                  _resident(w_glu.shape), _resident(w_merge.shape), _resident(w_a.shape),
                  _resident(w_b.shape)],
        out_specs=tok(d),
        out_shape=jax.ShapeDtypeStruct((t, d), BF16),
        compiler_params=_params(("parallel",)),
        name="merge",
    )(x, norm_w, y_ret, y_s5, u, ssm_d, w_glu, w_merge, w_a, w_b)


def _first_argmax(vals, lane):
    top = jnp.max(vals, axis=-1, keepdims=True)
    idx = jnp.min(jnp.where(vals == top, lane, ROUTER_LANES), axis=-1, keepdims=True)
    return top, idx


def _bf16_bits(x):
    return lax.bitcast_convert_type(x.astype(BF16).astype(F32), U32)


def _packed_store(ref, value, lead=()):
    rows, d = value.shape
    for i in range(PACK):
        lo = lax.shift_right_logical(_bf16_bits(value[:, i * LANES:(i + 1) * LANES]), U32(16))
        hi = _bf16_bits(value[:, d // 2 + i * LANES:d // 2 + (i + 1) * LANES])
        ref[lead + (pl.ds(i, rows, stride=PACK), slice(None))] = jnp.bitwise_or(hi, lo)


def _packed_pieces(ref, i, rows, lead=()):
    word = ref[lead + (pl.ds(i, rows, stride=PACK), slice(None))]
    lo = lax.bitcast_convert_type(lax.shift_left(word, U32(16)), F32)
    hi = lax.bitcast_convert_type(jnp.bitwise_and(word, U32(0xFFFF0000)), F32)
    return lo, hi


def _outproj_kernel(m_ref, x_ref, wo_ref, nw_ref, wr_ref, br_ref, x1_ref, xp_ref, ri_ref, rw_ref, cnt_ref):
    x1 = x_ref[...] + _dot(m_ref[...], wo_ref[...])
    x1_ref[...] = x1
    _packed_store(xp_ref, x1)
    h2 = _rms(x1, nw_ref[...])
    h_hi = h2.astype(BF16)
    h_lo = (h2 - h_hi.astype(F32)).astype(BF16)
    part = _dot(h_hi, wr_ref[...])
    logits = (part[:, :ROUTER_LANES] + part[:, ROUTER_LANES:]
              + _dot(h_lo, wr_ref[:, :ROUTER_LANES]) + br_ref[...])

    lane = lax.broadcasted_iota(I32, logits.shape, 1)
    neg = -jnp.inf
    is_group = lane < N_GROUPS
    g_top, g_idx = _first_argmax(jnp.where(is_group, logits, neg), lane)
    g_w = 1.0 / jnp.sum(jnp.where(is_group, jnp.exp(logits - g_top), 0.0), axis=-1, keepdims=True)
    first = N_GROUPS + EXPERTS_PER_GROUP * g_idx
    e_logits = jnp.where(lane >= first, jnp.where(lane < first + EXPERTS_PER_GROUP, logits, neg), neg)
    t1, i1 = _first_argmax(e_logits, lane)
    t2, i2 = _first_argmax(jnp.where(lane == i1, neg, e_logits), lane)
    ratio = jnp.exp(t2 - t1)
    w1 = g_w / (1.0 + ratio)
    w2 = w1 * ratio
    e1 = i1 - N_GROUPS
    e2 = i2 - N_GROUPS
    ri_ref[...] = jnp.where(lane == 0, e1, jnp.where(lane == 1, e2, 0))
    rw_ref[...] = jnp.where(lane == 0, w1, jnp.where(lane == 1, w2, 0.0))

    @pl.when(pl.program_id(0) == 0)
    def _():
        cnt_ref[...] = jnp.zeros_like(cnt_ref)

    picked = jnp.where(lane == e1, 1.0, 0.0) + jnp.where(lane == e2, 1.0, 0.0)
    cnt_ref[...] += jnp.sum(picked, axis=0, keepdims=True)


def _outproj(merged, x, w_out, norm_w, w_router, b_router, tm=256):
    t, d = x.shape
    tok = lambda w: pl.BlockSpec((tm, w), lambda i: (i, 0))
    return pl.pallas_call(
        _outproj_kernel,
        grid=(t // tm,),
        in_specs=[tok(d), tok(d), _resident(w_out.shape), _resident((1, d)),
                  _resident(w_router.shape), _resident((1, ROUTER_LANES))],
        out_specs=[tok(d), pl.BlockSpec((tm * PACK, LANES), lambda i: (i, 0)), tok(ROUTER_LANES),
                   tok(ROUTER_LANES), pl.BlockSpec((1, ROUTER_LANES), lambda i: (0, 0))],
        out_shape=[jax.ShapeDtypeStruct((t, d), F32),
                   jax.ShapeDtypeStruct((t * PACK, LANES), U32),
                   jax.ShapeDtypeStruct((t, ROUTER_LANES), I32),
                   jax.ShapeDtypeStruct((t, ROUTER_LANES), F32),
                   jax.ShapeDtypeStruct((1, ROUTER_LANES), F32)],
        compiler_params=_params(("arbitrary",)),
        name="outproj",
    )(merged, x, w_out, norm_w, w_router, b_router)


def _moe_kernel(be_ref, nu_ref, tok_ref, dst_ref,
                x_hbm, nw_ref, wg_ref, wu_ref, wd_ref,
                out_hbm,
                xbuf, ybuf, zbuf, h_ref, wgu_bf, wd_bf, gsem, ssem):
    s = pl.program_id(0)
    n_used = nu_ref[0]
    rows, d_model = h_ref.shape
    f = wg_ref.shape[1]
    slot = lax.rem(s, 2)
    other = 1 - slot

    def gather_row(blk, buf_slot, r):
        tok = pl.multiple_of(tok_ref[blk * rows + r], PACK)
        return pltpu.make_async_copy(x_hbm.at[pl.ds(tok, PACK)], xbuf.at[buf_slot, pl.ds(r * PACK, PACK)],
                                     gsem.at[buf_slot])

    def scatter_row(blk, buf_slot, r):
        dst = pl.multiple_of(dst_ref[(blk + 1) * rows + r], PACK)
        return pltpu.make_async_copy(ybuf.at[buf_slot, pl.ds(r * PACK, PACK)], out_hbm.at[pl.ds(dst, PACK)],
                                     ssem.at[buf_slot])

    def wait_slot(buf, sem, buf_slot):
        pltpu.make_async_copy(buf.at[buf_slot], buf.at[buf_slot], sem.at[buf_slot]).wait()

    @pl.when(s == 0)
    def _prologue():
        ybuf[...] = jnp.zeros_like(ybuf)
        zbuf[...] = jnp.zeros_like(zbuf)
        for r in range(rows):
            gather_row(0, 0, r).start()

    @pl.when(jnp.logical_and(s < n_used,
                             jnp.logical_or(s == 0, be_ref[s] != be_ref[jnp.maximum(s - 1, 0)])))
    def _cast_weights():
        wgu_bf[:, :f] = wg_ref[...].astype(BF16)
        wgu_bf[:, f:] = wu_ref[...].astype(BF16)
        wd_bf[...] = wd_ref[...].astype(BF16)

    @pl.when(s < n_used)
    def _block():
        wait_slot(xbuf, gsem, slot)
        for r in range(rows):
            scatter_row(s - 1, other, r).start()
        ss = jnp.zeros((rows, 1), F32)
        for i in range(PACK):
            lo, hi = _packed_pieces(xbuf, i, rows, (slot,))
            ss = ss + jnp.sum(lo * lo + hi * hi, axis=-1, keepdims=True)
        inv = lax.rsqrt(ss * (1.0 / d_model) + EPS)
        for i in range(PACK):
            lo, hi = _packed_pieces(xbuf, i, rows, (slot,))
            lo_cols = slice(i * LANES, (i + 1) * LANES)
            hi_cols = slice(d_model // 2 + i * LANES, d_model // 2 + (i + 1) * LANES)
            h_ref[:, lo_cols] = (lo * inv * nw_ref[:, lo_cols]).astype(BF16)
            h_ref[:, hi_cols] = (hi * inv * nw_ref[:, hi_cols]).astype(BF16)
        for r in range(rows):
            gather_row(s + 1, other, r).start()
        gu = _dot(h_ref[...], wgu_bf[...])
        gate = gu[:, :f]
        act = (gate * jax.nn.sigmoid(gate) * gu[:, f:]).astype(BF16)
        y = _dot(act, wd_bf[...])

        @pl.when(s >= 1)
        def _free_slot():
            wait_slot(ybuf, ssem, slot)

        _packed_store(ybuf, y, (slot,))

    @pl.when(s >= n_used)
    def _tail():
        wait_slot(ybuf, ssem, slot)

        @pl.when(s == n_used)
        def _last_block():
            wait_slot(xbuf, gsem, slot)
            for r in range(rows):
                scatter_row(s - 1, other, r).start()

        @pl.when(s > n_used)
        def _padding_block():
            dst = pl.multiple_of(dst_ref[s * rows], PACK)
            pltpu.make_async_copy(zbuf, out_hbm.at[pl.ds(dst, rows * PACK)], ssem.at[other]).start()

    @pl.when(s == pl.num_programs(0) - 1)
    def _drain():
        wait_slot(ybuf, ssem, other)


def _moe(block_expert, n_used, row_tok, row_dst, x1_packed, norm_w, w_gate, w_up, w_down):
    n_steps = block_expert.shape[0]
    _, d, f = w_gate.shape
    assert d == 2 * PACK * LANES

    def expert_block(shape):
        return pl.BlockSpec((None,) + shape, lambda b, be, *_: (be[b], 0, 0))

    grid_spec = pltpu.PrefetchScalarGridSpec(
        num_scalar_prefetch=4,
        grid=(n_steps,),
        in_specs=[pl.BlockSpec(memory_space=pl.ANY),
                  pl.BlockSpec((1, d), lambda b, *_: (0, 0)),
                  expert_block((d, f)), expert_block((d, f)), expert_block((f, d))],
        out_specs=pl.BlockSpec(memory_space=pl.ANY),
        scratch_shapes=[pltpu.VMEM((2, MOE_ROWS * PACK, LANES), U32),
                        pltpu.VMEM((2, MOE_ROWS * PACK, LANES), U32),
                        pltpu.VMEM((MOE_ROWS * PACK, LANES), U32),
                        pltpu.VMEM((MOE_ROWS, d), BF16),
                        pltpu.VMEM((d, 2 * f), BF16),
                        pltpu.VMEM((f, d), BF16),
                        pltpu.SemaphoreType.DMA((2,)),
                        pltpu.SemaphoreType.DMA((2,))],
    )
    return pl.pallas_call(
        _moe_kernel,
        grid_spec=grid_spec,
        out_shape=jax.ShapeDtypeStruct((row_dst.shape[0] * PACK, LANES), U32),
        compiler_params=pltpu.CompilerParams(dimension_semantics=("arbitrary",),
                                             vmem_limit_bytes=VMEM_LIMIT_BYTES,
                                             has_side_effects=True),
        name="moe",
    )(block_expert, n_used, row_tok * PACK, row_dst * PACK, x1_packed, norm_w, w_gate, w_up, w_down)


def _row_layout(expert, counts):
    t = expert.shape[0]
    n_assign = 2 * t
    n_blocks = n_assign // MOE_ROWS + N_EXPERTS
    flat_e = expert.reshape(-1)
    order = jnp.argsort(flat_e).astype(I32)
    start = jnp.cumsum(counts) - counts
    padded = (counts + MOE_ROWS - 1) // MOE_ROWS * MOE_ROWS
    pad_end = jnp.cumsum(padded)
    pad_start = pad_end - padded
    block_start = jnp.arange(n_blocks, dtype=I32) * MOE_ROWS
    n_rows = jnp.sum(padded)
    used = block_start < n_rows
    block_expert = jnp.minimum(jnp.sum(pad_end[None, :] <= block_start[:, None], axis=1, dtype=I32),
                               N_EXPERTS - 1)
    last_expert = jnp.max(jnp.where(used, block_expert, 0))
    block_expert = jnp.where(used, block_expert, last_expert)
    is_expert = block_expert[:, None] == jnp.arange(N_EXPERTS, dtype=I32)[None, :]

    def of_block(per_expert):
        return jnp.sum(jnp.where(is_expert, per_expert[None, :], 0), axis=1, dtype=I32)

    rank0 = block_start - of_block(pad_start)
    n_valid = jnp.where(used, jnp.clip(of_block(counts) - rank0, 0, MOE_ROWS), 0).astype(I32)
    within = jnp.arange(MOE_ROWS, dtype=I32)[None, :]
    valid = within < n_valid[:, None]
    src = order[jnp.clip(of_block(start)[:, None] + rank0[:, None] + within, 0, n_assign - 1)]
    tok = src // 2
    n_pad = MOE_ROWS - n_valid
    spare = n_assign + MOE_ROWS + (jnp.cumsum(n_pad) - n_pad)[:, None] + (within - n_valid[:, None])
    row_dst = jnp.where(valid, (src % 2) * t + tok, spare).astype(I32)
    row_dst = jnp.concatenate([n_assign + within, row_dst], axis=0).reshape(-1)
    row_tok = jnp.concatenate([jnp.where(valid, tok, 0), jnp.zeros((2, MOE_ROWS), I32)], axis=0).reshape(-1)
    block_expert = jnp.concatenate([block_expert, last_expert[None]])
    n_used = (n_rows // MOE_ROWS).astype(I32).reshape(1)
    return block_expert, n_used, row_tok, row_dst


def _final_kernel(x1_ref, r0_ref, r1_ref, rw_ref, p_ref, nple_ref, wg_ref, wp_ref, nf_ref, o_ref, x2_ref):
    tm, d = x2_ref.shape
    w0, w1 = rw_ref[:, 0:1], rw_ref[:, 1:2]
    for i in range(PACK):
        lo0, hi0 = _packed_pieces(r0_ref, i, tm)
        lo1, hi1 = _packed_pieces(r1_ref, i, tm)
        lo_cols = slice(i * LANES, (i + 1) * LANES)
        hi_cols = slice(d // 2 + i * LANES, d // 2 + (i + 1) * LANES)
        x2_ref[:, lo_cols] = x1_ref[:, lo_cols] + (w0 * lo0 + w1 * lo1)
        x2_ref[:, hi_cols] = x1_ref[:, hi_cols] + (w0 * hi0 + w1 * hi1)
    x2 = x2_ref[...]
    h3 = _rms(x2, nple_ref[...]).astype(BF16)
    gate = jax.nn.sigmoid(_dot(h3, wg_ref[...]))
    x3 = x2 + gate * _dot(p_ref[...].astype(BF16), wp_ref[...])
    o_ref[...] = _rms(x3, nf_ref[...])


def _final(x1, moe_packed, route_w, p, norm_ple, w_ple_gate, w_ple, norm_f, tm=256):
    t, d = x1.shape
    nt = t // tm
    tok = lambda w: pl.BlockSpec((tm, w), lambda i: (i, 0))
    packed = lambda first: pl.BlockSpec((tm * PACK, LANES), lambda i: (i + first, 0))
    return pl.pallas_call(
        _final_kernel,
        grid=(nt,),
        in_specs=[tok(d), packed(0), packed(nt), tok(ROUTER_LANES),
                  tok(p.shape[1]), _resident((1, d)), _resident(w_ple_gate.shape),
                  _resident(w_ple.shape), _resident((1, d))],
        out_specs=tok(d),
        out_shape=jax.ShapeDtypeStruct((t, d), F32),
        scratch_shapes=[pltpu.VMEM((tm, d), F32)],
        compiler_params=_params(("parallel",)),
        name="final",
    )(x1, moe_packed, moe_packed, route_w, p, norm_ple, w_ple_gate, w_ple, norm_f)


def _rope_tables(t):
    half = HEAD_DIM // 2
    inv_freq = ROPE_THETA ** (-jnp.arange(half, dtype=F32) / half)
    ang = jnp.arange(t, dtype=F32)[:, None] * inv_freq[None, :]
    cos, sin = jnp.cos(ang), jnp.sin(ang)
    return jnp.concatenate([cos, cos], axis=-1), jnp.concatenate([-sin, sin], axis=-1)


def _layer(x, p, norm_mix, w_in, ret_gn_w, lam_re, lam_im, log_dt, b_re, b_im, c_re, c_im, ssm_d,
           w_glu, w_branch_a, w_branch_b, w_merge, w_out, norm_ffn, w_rg, b_rg, w_re, b_re_router,
           w_exp_gate, w_exp_up, w_exp_down, norm_ple, w_ple_gate, w_ple, norm_f):
    t, d = x.shape
    row = lambda v: v.reshape(1, -1).astype(F32)

    q, k, v, g, u = _proj(x, row(norm_mix), w_in.astype(BF16))
    cos_tab, sin_tab = _rope_tables(t)
    y_ret = _retention(q, k, v, g, cos_tab, sin_tab, row(ret_gn_w))

    ops = _s5_operators(lam_re, lam_im, log_dt, b_re, b_im, c_re, c_im)
    y_s5 = _s5(u, *ops, tq=min(t, 4096))

    merged = _merge(x, row(norm_mix), y_ret, y_s5, u, row(ssm_d), w_glu.astype(BF16),
                    w_merge.astype(BF16), w_branch_a.astype(BF16), w_branch_b.astype(BF16))

    pad = ROUTER_LANES - N_GROUPS - N_EXPERTS
    w_router = jnp.concatenate([w_rg, w_re, jnp.zeros((d, pad), F32)], axis=1).astype(F32)
    w_router_hi = w_router.astype(BF16)
    w_router_lo = (w_router - w_router_hi.astype(F32)).astype(BF16)
    b_router = jnp.concatenate([b_rg, b_re_router, jnp.zeros((pad,), F32)]).reshape(1, ROUTER_LANES).astype(F32)
    x1, x1_packed, route_i, route_w, counts = _outproj(
        merged, x, w_out.astype(BF16), row(norm_ffn),
        jnp.concatenate([w_router_hi, w_router_lo], axis=1), b_router)

    block_expert, n_used, row_tok, row_dst = _row_layout(route_i[:, :2], counts[0, :N_EXPERTS].astype(I32))
    moe_rows = _moe(block_expert, n_used, row_tok, row_dst, x1_packed, row(norm_ffn),
                    w_exp_gate, w_exp_up, w_exp_down)

    return _final(x1, moe_rows, route_w, p, row(norm_ple), w_ple_gate.astype(BF16), w_ple.astype(BF16),
                  row(norm_f))


def kernel(x, p, norm_mix, w_in, ret_gn_w, ssm_lam_re, ssm_lam_im, ssm_log_dt, ssm_b_re, ssm_b_im, ssm_c_re, ssm_c_im, ssm_d, w_glu, w_branch_a, w_branch_b, w_merge, w_out, norm_ffn, w_router_group, b_router_group, w_router_expert, b_router_expert, w_exp_gate, w_exp_up, w_exp_down, norm_ple, w_ple_gate, w_ple, norm_f):
    depth, bsz, seq, _ = p.shape
    assert depth == 1 and bsz == 1, "single layer, single sequence"
    out = _layer(x[0], p[0, 0], norm_mix[0], w_in[0], ret_gn_w[0], ssm_lam_re[0], ssm_lam_im[0],
                 ssm_log_dt[0], ssm_b_re[0], ssm_b_im[0], ssm_c_re[0], ssm_c_im[0], ssm_d[0], w_glu[0],
                 w_branch_a[0], w_branch_b[0], w_merge[0], w_out[0], norm_ffn[0], w_router_group[0],
                 b_router_group[0], w_router_expert[0], b_router_expert[0], w_exp_gate[0], w_exp_up[0],
                 w_exp_down[0], norm_ple[0], w_ple_gate[0], w_ple[0], norm_f)
    return out[None]
```

```python
import math

import jax
import jax.numpy as jnp
from jax import lax
from jax.experimental import pallas as pl
from jax.experimental.pallas import tpu as pltpu

F32 = jnp.float32
BF16 = jnp.bfloat16
I32 = jnp.int32
U32 = jnp.uint32

EPS = 1e-6
LANES = 128
SUBLANES = 8
CHUNK_LOG2 = 6
RET_HEADS = 8
HEAD_DIM = 128
ROPE_THETA = 10000.0
SSM_GROUPS = 64
SSM_GROUP_SIZE = 16
SSM_STATE = 64
S5_CHUNK = 16
S5_TILE_GROUPS = LANES // SSM_GROUP_SIZE
S5_TILES = SSM_GROUPS // S5_TILE_GROUPS
S5_HALF = S5_TILE_GROUPS * SSM_STATE
N_GROUPS = 4
EXPERTS_PER_GROUP = 8
N_EXPERTS = 32
ROUTER_LANES = LANES
MOE_ROWS = 256
PACK = 8

VMEM_LIMIT_BYTES = 56 * 1024 * 1024


def _params(sem):
    return pltpu.CompilerParams(dimension_semantics=sem, vmem_limit_bytes=VMEM_LIMIT_BYTES)


def _resident(shape):
    n = len(shape)
    return pl.BlockSpec(shape, lambda *_: (0,) * n, pipeline_mode=pl.Buffered(1))


def _rms(x, w):
    ms = jnp.mean(x * x, axis=-1, keepdims=True)
    return x * lax.rsqrt(ms + EPS) * w


def _dot(a, b):
    return jnp.dot(a, b, preferred_element_type=F32)


def _proj_kernel(x_ref, nw_ref, w_ref, q_ref, k_ref, v_ref, g_ref, u_ref):
    h = _rms(x_ref[...], nw_ref[...]).astype(BF16)
    for c, o_ref in enumerate((q_ref, k_ref, v_ref, g_ref, u_ref)):
        width = o_ref.shape[1]
        o_ref[...] = _dot(h, w_ref[:, c * width:(c + 1) * width])


def _proj(x, norm_w, w_in_bf16, tm=512):
    t, d = x.shape
    width = w_in_bf16.shape[1] // 5
    out = jax.ShapeDtypeStruct((t, width), F32)
    return pl.pallas_call(
        _proj_kernel,
        grid=(t // tm,),
        in_specs=[pl.BlockSpec((tm, d), lambda i: (i, 0)),
                  _resident((1, d)),
                  _resident(w_in_bf16.shape)],
        out_specs=[pl.BlockSpec((tm, width), lambda i: (i, 0))] * 5,
        out_shape=[out] * 5,
        compiler_params=_params(("parallel",)),
        name="proj",
    )(x, norm_w, w_in_bf16)


def _log_gamma(h):
    return math.log1p(-(2.0 ** (-5.0 - h)))


def _retention_kernel(q_ref, k_ref, v_ref, g_ref, cos_ref, sin_ref, gn_ref, o_ref,
                      state_ref, decay_ref, qdec_ref, kdec_ref):
    rows = q_ref.shape[0]

    @pl.when(pl.program_id(0) == 0)
    def _init():
        ti = lax.broadcasted_iota(I32, (rows, rows), 0)
        si = lax.broadcasted_iota(I32, (rows, rows), 1)
        visible = lax.shift_right_logical(si, CHUNK_LOG2) <= lax.shift_right_logical(ti, CHUNK_LOG2)
        dist = jnp.abs(ti - si).astype(F32)
        pos = lax.broadcasted_iota(I32, (rows, HEAD_DIM), 0).astype(F32)
        for h in range(RET_HEADS):
            lg = _log_gamma(h)
            decay_ref[h] = jnp.where(visible, jnp.exp(lg * dist), 0.0)
            qdec_ref[h] = jnp.exp(lg * (pos + 1.0))
            kdec_ref[h] = jnp.exp(lg * (rows - 1.0 - pos))
        state_ref[...] = jnp.zeros_like(state_ref)

    cos = cos_ref[...]
    sin = sin_ref[...]
    scale = HEAD_DIM ** -0.5
    for h in range(RET_HEADS):
        sl = slice(h * HEAD_DIM, (h + 1) * HEAD_DIM)
        qh = q_ref[:, sl]
        kh = k_ref[:, sl]
        vb = v_ref[:, sl].astype(BF16)
        qr = qh * cos + pltpu.roll(qh, HEAD_DIM // 2, 1) * sin
        kr = (kh * cos + pltpu.roll(kh, HEAD_DIM // 2, 1) * sin) * scale
        scores = lax.dot_general(qr.astype(BF16), kr.astype(BF16), (((1,), (1,)), ((), ())),
                                 preferred_element_type=F32) * decay_ref[h]
        state = state_ref[h]
        y = _dot(scores.astype(BF16), vb)
        y = y + _dot((qr * qdec_ref[h]).astype(BF16), state.astype(BF16))
        kv = lax.dot_general((kr * kdec_ref[h]).astype(BF16), vb, (((0,), (0,)), ((), ())),
                             preferred_element_type=F32)
        state_ref[h] = math.exp(_log_gamma(h) * rows) * state + kv
        mu = jnp.mean(y, axis=-1, keepdims=True)
        yc = y - mu
        var = jnp.mean(yc * yc, axis=-1, keepdims=True)
        yn = yc * lax.rsqrt(var + EPS) * gn_ref[:, sl]
        gh = g_ref[:, sl]
        o_ref[:, sl] = (gh * jax.nn.sigmoid(gh) * yn).astype(o_ref.dtype)


def _retention(q, k, v, g, cos_tab, sin_tab, gn_w, rows=256):
    t, width = q.shape
    tok = pl.BlockSpec((rows, width), lambda i: (i, 0))
    tab = pl.BlockSpec((rows, HEAD_DIM), lambda i: (i, 0))
    return pl.pallas_call(
        _retention_kernel,
        grid=(t // rows,),
        in_specs=[tok, tok, tok, tok, tab, tab, _resident((1, width))],
        out_specs=tok,
        out_shape=jax.ShapeDtypeStruct((t, width), BF16),
        scratch_shapes=[pltpu.VMEM((RET_HEADS, HEAD_DIM, HEAD_DIM), F32),
                        pltpu.VMEM((RET_HEADS, rows, rows), F32),
                        pltpu.VMEM((RET_HEADS, rows, HEAD_DIM), F32),
                        pltpu.VMEM((RET_HEADS, rows, HEAD_DIM), F32)],
        compiler_params=_params(("arbitrary",)),
        name="retention",
    )(q, k, v, g, cos_tab, sin_tab, gn_w)


def _cmul(ar, ai, br, bi):
    return ar * br - ai * bi, ar * bi + ai * br


def _s5_kernel(u_ref, m_ref, bst_ref, cout_ref, pw_ref, y_ref, a_ref, pr_ref, pi_ref, carry_ref):
    tq = u_ref.shape[0]
    nn = tq // S5_CHUNK
    nb = nn // SUBLANES
    half = S5_HALF

    @pl.when(pl.program_id(1) == 0)
    def _():
        carry_ref[...] = jnp.zeros_like(carry_ref)

    for l in range(S5_CHUNK):
        a_ref[:, l * LANES:(l + 1) * LANES] = u_ref[pl.ds(l, nn, stride=S5_CHUNK), :].astype(BF16)
    a = a_ref[...]
    kt = 2 * LANES
    yc = [_dot(a_ref[:, :(n + 1) * kt], m_ref[:(n + 1) * kt, n * kt:(n + 1) * kt])
          for n in range(S5_CHUNK // 2)]
    s_in = _dot(a, bst_ref[...])
    sr, si = s_in[:, :half], s_in[:, half:]

    row = lax.broadcasted_iota(I32, (nn, half), 0)
    zr = jnp.where(row == 0, carry_ref[0:1, :], pltpu.roll(sr, 1, 0))
    zi = jnp.where(row == 0, carry_ref[1:2, :], pltpu.roll(si, 1, 0))
    in_block = jnp.bitwise_and(row, SUBLANES - 1)
    for d in (1, 2, 4):
        mr, mi = _cmul(pw_ref[0, d - 1:d, :], pw_ref[1, d - 1:d, :],
                       pltpu.roll(zr, d, 0), pltpu.roll(zi, d, 0))
        keep = in_block >= d
        zr = zr + jnp.where(keep, mr, 0.0)
        zi = zi + jnp.where(keep, mi, 0.0)
    pr_ref[...] = zr.reshape(nb, SUBLANES, half)
    pi_ref[...] = zi.reshape(nb, SUBLANES, half)

    wr, wi = pw_ref[0], pw_ref[1]

    def block_step(r, c):
        fr, fi = _cmul(wr, wi, c[0], c[1])
        nr = pr_ref[r] + fr
        ni = pi_ref[r] + fi
        pr_ref[r] = nr
        pi_ref[r] = ni
        last = slice(SUBLANES - 1, SUBLANES)
        return (jnp.broadcast_to(nr[last, :], (SUBLANES, half)),
                jnp.broadcast_to(ni[last, :], (SUBLANES, half)))

    zero = jnp.zeros((SUBLANES, half), F32)
    lr, li = lax.fori_loop(0, nb, block_step, (zero, zero))
    er, ei = _cmul(wr[0:1, :], wi[0:1, :], lr[0:1, :], li[0:1, :])
    carry_ref[0:1, :] = er + sr[nn - 1:nn, :]
    carry_ref[1:2, :] = ei + si[nn - 1:nn, :]

    p = jnp.concatenate([pr_ref[...].reshape(nn, half), pi_ref[...].reshape(nn, half)], axis=-1)
    y_cross = _dot(p.astype(BF16), cout_ref[...])
    for l in range(S5_CHUNK):
        y_intra = yc[l // 2][:, (l % 2) * LANES:(l % 2 + 1) * LANES]
        y_ref[pl.ds(l, nn, stride=S5_CHUNK), :] = y_intra + y_cross[:, l * LANES:(l + 1) * LANES]


def _s5(u, m_op, bst_op, cout_op, pw_op, tq):
    t, width = u.shape
    nn = tq // S5_CHUNK
    kdim = S5_CHUNK * LANES

    def tile_block(*shape):
        n = len(shape)
        return pl.BlockSpec((None,) + shape, lambda o, i: (o,) + (0,) * n)

    tok = pl.BlockSpec((tq, LANES), lambda o, i: (i, o))
    return pl.pallas_call(
        _s5_kernel,
        grid=(width // LANES, t // tq),
        in_specs=[tok, tile_block(kdim, kdim), tile_block(kdim, 2 * S5_HALF),
                  tile_block(2 * S5_HALF, kdim), tile_block(2, SUBLANES, S5_HALF)],
        out_specs=tok,
        out_shape=jax.ShapeDtypeStruct((t, width), F32),
        scratch_shapes=[pltpu.VMEM((nn, kdim), BF16),
                        pltpu.VMEM((nn // SUBLANES, SUBLANES, S5_HALF), F32),
                        pltpu.VMEM((nn // SUBLANES, SUBLANES, S5_HALF), F32),
                        pltpu.VMEM((2, S5_HALF), F32)],
        compiler_params=_params(("parallel", "arbitrary")),
        name="s5",
    )(u, m_op, bst_op, cout_op, pw_op)


def _s5_operators(lam_re, lam_im, log_dt, b_re, b_im, c_re, c_im):
    g, p, cg = b_re.shape
    lc, nt, tg = S5_CHUNK, S5_TILES, S5_TILE_GROUPS
    lam = lax.complex(jnp.minimum(lam_re, -1e-4), lam_im)
    log_lam_bar = lam * jnp.exp(log_dt)[:, None]
    lam_bar = jnp.exp(log_lam_bar)
    b_bar = ((lam_bar - 1.0) / lam)[..., None] * lax.complex(b_re, b_im)
    c_mat = lax.complex(c_re, c_im)

    def power(n):
        n = jnp.asarray(n, F32)
        return jnp.exp(n[:, None, None].astype(jnp.complex64) * log_lam_bar)

    lags = jnp.arange(lc)
    taps = jnp.real(jnp.einsum('gap,dgp,gpc->dgac', c_mat, power(lags), b_bar,
                               precision=lax.Precision.HIGHEST))
    taps = taps.reshape(lc, nt, tg, cg, cg).transpose(1, 2, 4, 0, 3).reshape(nt, tg * cg, lc * cg)
    taps = jnp.pad(taps, ((0, 0), (0, 0), ((lc - 1) * cg, 0)))
    kc = jnp.stack([taps[:, :, (lc - 1 - l) * cg:(2 * lc - 1 - l) * cg] for l in range(lc)], axis=1)
    kc = kc.reshape(nt, lc * tg * cg, lc * cg)

    bst = power(lc - 1 - lags)[:, :, None, :] * b_bar.transpose(0, 2, 1)[None]
    bst = bst.reshape(lc, nt, tg * cg, p).transpose(1, 0, 2, 3).reshape(nt, lc * tg * cg, p)
    bc = jnp.concatenate([jnp.real(bst), jnp.imag(bst)], axis=-1)

    pw_out = jnp.exp((lags + 1).astype(jnp.complex64)[None, None, :] * log_lam_bar[:, :, None])
    cout = pw_out[:, :, :, None] * c_mat.transpose(0, 2, 1)[:, :, None, :]
    cout = cout.reshape(nt, tg * p, lc * cg)
    cc = jnp.concatenate([jnp.real(cout), -jnp.imag(cout)], axis=1)

    def replicate(n_in, size):
        i = jnp.arange(n_in)[:, None]
        j = jnp.arange(n_in * tg)[None, :]
        return jnp.logical_and(i // size == j // (tg * size), i % size == j % size).astype(BF16)

    def spread(compact, size_in, row_size, col_size):
        full = jnp.einsum('trk,kc->trc', compact.astype(BF16), replicate(compact.shape[2], size_in),
                          preferred_element_type=F32)
        row_group = (jnp.arange(full.shape[1]) // row_size) % tg
        col_group = (jnp.arange(full.shape[2]) // col_size) % tg
        return jnp.where(row_group[:, None] == col_group[None, :], full, 0.0).astype(BF16)

    m_op = spread(kc, cg, cg, cg)
    bst_op = spread(bc, p, cg, p)
    cout_op = spread(cc, cg, p, cg)

    pw = power(lc * (1 + jnp.arange(SUBLANES)))
    pw = pw.reshape(SUBLANES, nt, tg * p).transpose(1, 0, 2)
    pw_op = jnp.stack([jnp.real(pw), jnp.imag(pw)], axis=1).astype(F32)
    return m_op, bst_op, cout_op, pw_op


def _merge_kernel(x_ref, nw_ref, yret_ref, ys_ref, u_ref, d_ref, wglu_ref, wm_ref, wa_ref, wb_ref,
                  o_ref):
    d_model = x_ref.shape[1]
    h = _rms(x_ref[...], nw_ref[...]).astype(BF16)
    y = jax.nn.gelu(ys_ref[...] + d_ref[...] * u_ref[...])
    y_ssm = (y * jax.nn.sigmoid(_dot(y.astype(BF16), wglu_ref[...]))).astype(BF16)
    y_ret = yret_ref[...]
    cw = 512
    for c in range(d_model // cw):
        cs = slice(c * cw, (c + 1) * cw)
        gs = slice(d_model + c * cw, d_model + (c + 1) * cw)
        g0 = jax.nn.sigmoid(_dot(h, wm_ref[:, cs]))
        g1 = jax.nn.sigmoid(_dot(h, wm_ref[:, gs]))
        o_ref[:, cs] = (g0 * _dot(y_ret, wa_ref[:, cs]) + g1 * _dot(y_ssm, wb_ref[:, cs])).astype(o_ref.dtype)


def _merge(x, norm_w, y_ret, y_s5, u, ssm_d, w_glu, w_merge, w_a, w_b, tm=512):
    t, d = x.shape
    width = y_ret.shape[1]
    tok = lambda w: pl.BlockSpec((tm, w), lambda i: (i, 0))
    return pl.pallas_call(
        _merge_kernel,
        grid=(t // tm,),
        in_specs=[tok(d), _resident((1, d)), tok(width), tok(width), tok(width), _resident((1, width)),
                  _resident(w_glu.shape), _resident(w_merge.shape), _resident(w_a.shape),
                  _resident(w_b.shape)],
        out_specs=tok(d),
        out_shape=jax.ShapeDtypeStruct((t, d), BF16),
        compiler_params=_params(("parallel",)),
        name="merge",
    )(x, norm_w, y_ret, y_s5, u, ssm_d, w_glu, w_merge, w_a, w_b)


def _first_argmax(vals, lane):
    top = jnp.max(vals, axis=-1, keepdims=True)
    idx = jnp.min(jnp.where(vals == top, lane, ROUTER_LANES), axis=-1, keepdims=True)
    return top, idx


def _bf16_bits(x):
    return lax.bitcast_convert_type(x.astype(BF16).astype(F32), U32)


def _packed_store(ref, value, lead=()):
    rows, d = value.shape
    for i in range(PACK):
        lo = lax.shift_right_logical(_bf16_bits(value[:, i * LANES:(i + 1) * LANES]), U32(16))
        hi = _bf16_bits(value[:, d // 2 + i * LANES:d // 2 + (i + 1) * LANES])
        ref[lead + (pl.ds(i, rows, stride=PACK), slice(None))] = jnp.bitwise_or(hi, lo)


def _packed_pieces(ref, i, rows, lead=()):
    word = ref[lead + (pl.ds(i, rows, stride=PACK), slice(None))]
    lo = lax.bitcast_convert_type(lax.shift_left(word, U32(16)), F32)
    hi = lax.bitcast_convert_type(jnp.bitwise_and(word, U32(0xFFFF0000)), F32)
    return lo, hi


def _outproj_kernel(m_ref, x_ref, wo_ref, nw_ref, wr_ref, br_ref, x1_ref, xp_ref, ri_ref, rw_ref, cnt_ref):
    x1 = x_ref[...] + _dot(m_ref[...], wo_ref[...])
    x1_ref[...] = x1
    _packed_store(xp_ref, x1)
    h2 = _rms(x1, nw_ref[...])
    h_hi = h2.astype(BF16)
    h_lo = (h2 - h_hi.astype(F32)).astype(BF16)
    part = _dot(h_hi, wr_ref[...])
    logits = (part[:, :ROUTER_LANES] + part[:, ROUTER_LANES:]
              + _dot(h_lo, wr_ref[:, :ROUTER_LANES]) + br_ref[...])

    lane = lax.broadcasted_iota(I32, logits.shape, 1)
    neg = -jnp.inf
    is_group = lane < N_GROUPS
    g_top, g_idx = _first_argmax(jnp.where(is_group, logits, neg), lane)
    g_w = 1.0 / jnp.sum(jnp.where(is_group, jnp.exp(logits - g_top), 0.0), axis=-1, keepdims=True)
    first = N_GROUPS + EXPERTS_PER_GROUP * g_idx
    e_logits = jnp.where(lane >= first, jnp.where(lane < first + EXPERTS_PER_GROUP, logits, neg), neg)
    t1, i1 = _first_argmax(e_logits, lane)
    t2, i2 = _first_argmax(jnp.where(lane == i1, neg, e_logits), lane)
    ratio = jnp.exp(t2 - t1)
    w1 = g_w / (1.0 + ratio)
    w2 = w1 * ratio
    e1 = i1 - N_GROUPS
    e2 = i2 - N_GROUPS
    ri_ref[...] = jnp.where(lane == 0, e1, jnp.where(lane == 1, e2, 0))
    rw_ref[...] = jnp.where(lane == 0, w1, jnp.where(lane == 1, w2, 0.0))

    @pl.when(pl.program_id(0) == 0)
    def _():
        cnt_ref[...] = jnp.zeros_like(cnt_ref)

    picked = jnp.where(lane == e1, 1.0, 0.0) + jnp.where(lane == e2, 1.0, 0.0)
    cnt_ref[...] += jnp.sum(picked, axis=0, keepdims=True)


def _outproj(merged, x, w_out, norm_w, w_router, b_router, tm=512):
    t, d = x.shape
    tok = lambda w: pl.BlockSpec((tm, w), lambda i: (i, 0))
    return pl.pallas_call(
        _outproj_kernel,
        grid=(t // tm,),
        in_specs=[tok(d), tok(d), _resident(w_out.shape), _resident((1, d)),
                  _resident(w_router.shape), _resident((1, ROUTER_LANES))],
        out_specs=[tok(d), pl.BlockSpec((tm * PACK, LANES), lambda i: (i, 0)), tok(ROUTER_LANES),
                   tok(ROUTER_LANES), pl.BlockSpec((1, ROUTER_LANES), lambda i: (0, 0))],
        out_shape=[jax.ShapeDtypeStruct((t, d), F32),
                   jax.ShapeDtypeStruct((t * PACK, LANES), U32),
                   jax.ShapeDtypeStruct((t, ROUTER_LANES), I32),
                   jax.ShapeDtypeStruct((t, ROUTER_LANES), F32),
                   jax.ShapeDtypeStruct((1, ROUTER_LANES), F32)],
        compiler_params=_params(("arbitrary",)),
        name="outproj",
    )(merged, x, w_out, norm_w, w_router, b_router)


def _moe_kernel(be_ref, nu_ref, tok_ref, dst_ref,
                x_hbm, nw_ref, wg_ref, wu_ref, wd_ref,
                out_hbm,
                xbuf, ybuf, zbuf, h_ref, wgu_bf, wd_bf, gsem, ssem):
    s = pl.program_id(0)
    n_used = nu_ref[0]
    rows, d_model = h_ref.shape
    f = wg_ref.shape[1]
    slot = lax.rem(s, 2)
    other = 1 - slot

    def gather_row(blk, buf_slot, r):
        tok = pl.multiple_of(tok_ref[blk * rows + r], PACK)
        return pltpu.make_async_copy(x_hbm.at[pl.ds(tok, PACK)], xbuf.at[buf_slot, pl.ds(r * PACK, PACK)],
                                     gsem.at[buf_slot])

    def scatter_row(blk, buf_slot, r):
        dst = pl.multiple_of(dst_ref[(blk + 1) * rows + r], PACK)
        return pltpu.make_async_copy(ybuf.at[buf_slot, pl.ds(r * PACK, PACK)], out_hbm.at[pl.ds(dst, PACK)],
                                     ssem.at[buf_slot])

    def wait_slot(buf, sem, buf_slot):
        pltpu.make_async_copy(buf.at[buf_slot], buf.at[buf_slot], sem.at[buf_slot]).wait()

    @pl.when(s == 0)
    def _prologue():
        ybuf[...] = jnp.zeros_like(ybuf)
        zbuf[...] = jnp.zeros_like(zbuf)
        for r in range(rows):
            gather_row(0, 0, r).start()

    @pl.when(jnp.logical_and(s < n_used,
                             jnp.logical_or(s == 0, be_ref[s] != be_ref[jnp.maximum(s - 1, 0)])))
    def _cast_weights():
        wgu_bf[:, :f] = wg_ref[...].astype(BF16)
        wgu_bf[:, f:] = wu_ref[...].astype(BF16)
        wd_bf[...] = wd_ref[...].astype(BF16)

    @pl.when(s < n_used)
    def _block():
        wait_slot(xbuf, gsem, slot)
        for r in range(rows):
            scatter_row(s - 1, other, r).start(priority=1)
        ss = jnp.zeros((rows, 1), F32)
        for i in range(PACK):
            lo, hi = _packed_pieces(xbuf, i, rows, (slot,))
            ss = ss + jnp.sum(lo * lo + hi * hi, axis=-1, keepdims=True)
        inv = lax.rsqrt(ss * (1.0 / d_model) + EPS)
        for i in range(PACK):
            lo, hi = _packed_pieces(xbuf, i, rows, (slot,))
            lo_cols = slice(i * LANES, (i + 1) * LANES)
            hi_cols = slice(d_model // 2 + i * LANES, d_model // 2 + (i + 1) * LANES)
            h_ref[:, lo_cols] = (lo * inv * nw_ref[:, lo_cols]).astype(BF16)
            h_ref[:, hi_cols] = (hi * inv * nw_ref[:, hi_cols]).astype(BF16)
        for r in range(rows):
            gather_row(s + 1, other, r).start()
        gu = _dot(h_ref[...], wgu_bf[...])
        gate = gu[:, :f]
        act = (gate * jax.nn.sigmoid(gate) * gu[:, f:]).astype(BF16)
        y = _dot(act, wd_bf[...])

        @pl.when(s >= 1)
        def _free_slot():
            wait_slot(ybuf, ssem, slot)

        _packed_store(ybuf, y, (slot,))

    @pl.when(s >= n_used)
    def _tail():
        wait_slot(ybuf, ssem, slot)

        @pl.when(s == n_used)
        def _last_block():
            wait_slot(xbuf, gsem, slot)
            for r in range(rows):
                scatter_row(s - 1, other, r).start()

        @pl.when(s > n_used)
        def _padding_block():
            dst = pl.multiple_of(dst_ref[s * rows], PACK)
            pltpu.make_async_copy(zbuf, out_hbm.at[pl.ds(dst, rows * PACK)], ssem.at[other]).start()

    @pl.when(s == pl.num_programs(0) - 1)
    def _drain():
        wait_slot(ybuf, ssem, other)


def _moe(block_expert, n_used, row_tok, row_dst, x1_packed, norm_w, w_gate, w_up, w_down):
    n_steps = block_expert.shape[0]
    _, d, f = w_gate.shape
    assert d == 2 * PACK * LANES

    def expert_block(shape):
        return pl.BlockSpec((None,) + shape, lambda b, be, *_: (be[b], 0, 0))

    grid_spec = pltpu.PrefetchScalarGridSpec(
        num_scalar_prefetch=4,
        grid=(n_steps,),
        in_specs=[pl.BlockSpec(memory_space=pl.ANY),
                  pl.BlockSpec((1, d), lambda b, *_: (0, 0)),
                  expert_block((d, f)), expert_block((d, f)), expert_block((f, d))],
        out_specs=pl.BlockSpec(memory_space=pl.ANY),
        scratch_shapes=[pltpu.VMEM((2, MOE_ROWS * PACK, LANES), U32),
                        pltpu.VMEM((2, MOE_ROWS * PACK, LANES), U32),
                        pltpu.VMEM((MOE_ROWS * PACK, LANES), U32),
                        pltpu.VMEM((MOE_ROWS, d), BF16),
                        pltpu.VMEM((d, 2 * f), BF16),
                        pltpu.VMEM((f, d), BF16),
                        pltpu.SemaphoreType.DMA((2,)),
                        pltpu.SemaphoreType.DMA((2,))],
    )
    return pl.pallas_call(
        _moe_kernel,
        grid_spec=grid_spec,
        out_shape=jax.ShapeDtypeStruct((row_dst.shape[0] * PACK, LANES), U32),
        compiler_params=pltpu.CompilerParams(dimension_semantics=("arbitrary",),
                                             vmem_limit_bytes=VMEM_LIMIT_BYTES,
                                             has_side_effects=True),
        name="moe",
    )(block_expert, n_used, row_tok * PACK, row_dst * PACK, x1_packed, norm_w, w_gate, w_up, w_down)


def _row_layout(expert, counts):
    t = expert.shape[0]
    n_assign = 2 * t
    n_blocks = n_assign // MOE_ROWS + N_EXPERTS
    flat_e = expert.reshape(-1)
    order = jnp.argsort(flat_e).astype(I32)
    start = jnp.cumsum(counts) - counts
    padded = (counts + MOE_ROWS - 1) // MOE_ROWS * MOE_ROWS
    pad_end = jnp.cumsum(padded)
    pad_start = pad_end - padded
    block_start = jnp.arange(n_blocks, dtype=I32) * MOE_ROWS
    n_rows = jnp.sum(padded)
    used = block_start < n_rows
    block_expert = jnp.minimum(jnp.sum(pad_end[None, :] <= block_start[:, None], axis=1, dtype=I32),
                               N_EXPERTS - 1)
    last_expert = jnp.max(jnp.where(used, block_expert, 0))
    block_expert = jnp.where(used, block_expert, last_expert)
    is_expert = block_expert[:, None] == jnp.arange(N_EXPERTS, dtype=I32)[None, :]

    def of_block(per_expert):
        return jnp.sum(jnp.where(is_expert, per_expert[None, :], 0), axis=1, dtype=I32)

    rank0 = block_start - of_block(pad_start)
    n_valid = jnp.where(used, jnp.clip(of_block(counts) - rank0, 0, MOE_ROWS), 0).astype(I32)
    within = jnp.arange(MOE_ROWS, dtype=I32)[None, :]
    valid = within < n_valid[:, None]
    src = order[jnp.clip(of_block(start)[:, None] + rank0[:, None] + within, 0, n_assign - 1)]
    tok = src // 2
    n_pad = MOE_ROWS - n_valid
    spare = n_assign + MOE_ROWS + (jnp.cumsum(n_pad) - n_pad)[:, None] + (within - n_valid[:, None])
    row_dst = jnp.where(valid, (src % 2) * t + tok, spare).astype(I32)
    row_dst = jnp.concatenate([n_assign + within, row_dst], axis=0).reshape(-1)
    row_tok = jnp.concatenate([jnp.where(valid, tok, 0), jnp.zeros((2, MOE_ROWS), I32)], axis=0).reshape(-1)
    block_expert = jnp.concatenate([block_expert, last_expert[None]])
    n_used = (n_rows // MOE_ROWS).astype(I32).reshape(1)
    return block_expert, n_used, row_tok, row_dst


def _final_kernel(x1_ref, r0_ref, r1_ref, rw_ref, p_ref, nple_ref, wg_ref, wp_ref, nf_ref, o_ref, x2_ref):
    tm, d = x2_ref.shape
    w0, w1 = rw_ref[:, 0:1], rw_ref[:, 1:2]
    for i in range(PACK):
        lo0, hi0 = _packed_pieces(r0_ref, i, tm)
        lo1, hi1 = _packed_pieces(r1_ref, i, tm)
        lo_cols = slice(i * LANES, (i + 1) * LANES)
        hi_cols = slice(d // 2 + i * LANES, d // 2 + (i + 1) * LANES)
        x2_ref[:, lo_cols] = x1_ref[:, lo_cols] + (w0 * lo0 + w1 * lo1)
        x2_ref[:, hi_cols] = x1_ref[:, hi_cols] + (w0 * hi0 + w1 * hi1)
    x2 = x2_ref[...]
    h3 = _rms(x2, nple_ref[...]).astype(BF16)
    gate = jax.nn.sigmoid(_dot(h3, wg_ref[...]))
    x3 = x2 + gate * _dot(p_ref[...].astype(BF16), wp_ref[...])
    o_ref[...] = _rms(x3, nf_ref[...])


def _final(x1, moe_packed, route_w, p, norm_ple, w_ple_gate, w_ple, norm_f, tm=512):
    t, d = x1.shape
    nt = t // tm
    tok = lambda w: pl.BlockSpec((tm, w), lambda i: (i, 0))
    packed = lambda first: pl.BlockSpec((tm * PACK, LANES), lambda i: (i + first, 0))
    return pl.pallas_call(
        _final_kernel,
        grid=(nt,),
        in_specs=[tok(d), packed(0), packed(nt), tok(ROUTER_LANES),
                  tok(p.shape[1]), _resident((1, d)), _resident(w_ple_gate.shape),
                  _resident(w_ple.shape), _resident((1, d))],
        out_specs=tok(d),
        out_shape=jax.ShapeDtypeStruct((t, d), F32),
        scratch_shapes=[pltpu.VMEM((tm, d), F32)],
        compiler_params=_params(("parallel",)),
        name="final",
    )(x1, moe_packed, moe_packed, route_w, p, norm_ple, w_ple_gate, w_ple, norm_f)


def _rope_tables(t):
    half = HEAD_DIM // 2
    inv_freq = ROPE_THETA ** (-jnp.arange(half, dtype=F32) / half)
    ang = jnp.arange(t, dtype=F32)[:, None] * inv_freq[None, :]
    cos, sin = jnp.cos(ang), jnp.sin(ang)
    return jnp.concatenate([cos, cos], axis=-1), jnp.concatenate([-sin, sin], axis=-1)


def _layer(x, p, norm_mix, w_in, ret_gn_w, lam_re, lam_im, log_dt, b_re, b_im, c_re, c_im, ssm_d,
           w_glu, w_branch_a, w_branch_b, w_merge, w_out, norm_ffn, w_rg, b_rg, w_re, b_re_router,
           w_exp_gate, w_exp_up, w_exp_down, norm_ple, w_ple_gate, w_ple, norm_f):
    t, d = x.shape
    row = lambda v: v.reshape(1, -1).astype(F32)

    q, k, v, g, u = _proj(x, row(norm_mix), w_in.astype(BF16))
    cos_tab, sin_tab = _rope_tables(t)
    y_ret = _retention(q, k, v, g, cos_tab, sin_tab, row(ret_gn_w))

    ops = _s5_operators(lam_re, lam_im, log_dt, b_re, b_im, c_re, c_im)
    y_s5 = _s5(u, *ops, tq=min(t, 4096))

    merged = _merge(x, row(norm_mix), y_ret, y_s5, u, row(ssm_d), w_glu.astype(BF16),
                    w_merge.astype(BF16), w_branch_a.astype(BF16), w_branch_b.astype(BF16))

    pad = ROUTER_LANES - N_GROUPS - N_EXPERTS
    w_router = jnp.concatenate([w_rg, w_re, jnp.zeros((d, pad), F32)], axis=1).astype(F32)
    w_router_hi = w_router.astype(BF16)
    w_router_lo = (w_router - w_router_hi.astype(F32)).astype(BF16)
    b_router = jnp.concatenate([b_rg, b_re_router, jnp.zeros((pad,), F32)]).reshape(1, ROUTER_LANES).astype(F32)
    x1, x1_packed, route_i, route_w, counts = _outproj(
        merged, x, w_out.astype(BF16), row(norm_ffn),
        jnp.concatenate([w_router_hi, w_router_lo], axis=1), b_router)

    block_expert, n_used, row_tok, row_dst = _row_layout(route_i[:, :2], counts[0, :N_EXPERTS].astype(I32))
    moe_rows = _moe(block_expert, n_used, row_tok, row_dst, x1_packed, row(norm_ffn),
                    w_exp_gate, w_exp_up, w_exp_down)

    return _final(x1, moe_rows, route_w, p, row(norm_ple), w_ple_gate.astype(BF16), w_ple.astype(BF16),
                  row(norm_f))


def kernel(x, p, norm_mix, w_in, ret_gn_w, ssm_lam_re, ssm_lam_im, ssm_log_dt, ssm_b_re, ssm_b_im, ssm_c_re, ssm_c_im, ssm_d, w_glu, w_branch_a, w_branch_b, w_merge, w_out, norm_ffn, w_router_group, b_router_group, w_router_expert, b_router_expert, w_exp_gate, w_exp_up, w_exp_down, norm_ple, w_ple_gate, w_ple, norm_f):
    depth, bsz, seq, _ = p.shape
    assert depth == 1 and bsz == 1, "single layer, single sequence"
    out = _layer(x[0], p[0, 0], norm_mix[0], w_in[0], ret_gn_w[0], ssm_lam_re[0], ssm_lam_im[0],
                 ssm_log_dt[0], ssm_b_re[0], ssm_b_im[0], ssm_c_re[0], ssm_c_im[0], ssm_d[0], w_glu[0],
                 w_branch_a[0], w_branch_b[0], w_merge[0], w_out[0], norm_ffn[0], w_router_group[0],
                 b_router_group[0], w_router_expert[0], b_router_expert[0], w_exp_gate[0], w_exp_up[0],
                 w_exp_down[0], norm_ple[0], w_ple_gate[0], w_ple[0], norm_f)
    return out[None]
```

```python
import math

import jax
import jax.numpy as jnp
from jax import lax
from jax.experimental import pallas as pl
from jax.experimental.pallas import tpu as pltpu

F32 = jnp.float32
BF16 = jnp.bfloat16
I32 = jnp.int32
U32 = jnp.uint32

EPS = 1e-6
LANES = 128
SUBLANES = 8
CHUNK_LOG2 = 6
RET_HEADS = 8
HEAD_DIM = 128
ROPE_THETA = 10000.0
SSM_GROUPS = 64
SSM_GROUP_SIZE = 16
SSM_STATE = 64
S5_CHUNK = 16
S5_TILE_GROUPS = LANES // SSM_GROUP_SIZE
S5_TILES = SSM_GROUPS // S5_TILE_GROUPS
S5_HALF = S5_TILE_GROUPS * SSM_STATE
N_GROUPS = 4
EXPERTS_PER_GROUP = 8
N_EXPERTS = 32
ROUTER_LANES = LANES
MOE_ROWS = 256
MOE_GATHER_SLOTS = 3
PACK = 8

VMEM_LIMIT_BYTES = 56 * 1024 * 1024


def _params(sem):
    return pltpu.CompilerParams(dimension_semantics=sem, vmem_limit_bytes=VMEM_LIMIT_BYTES)


def _resident(shape):
    n = len(shape)
    return pl.BlockSpec(shape, lambda *_: (0,) * n, pipeline_mode=pl.Buffered(1))


def _rms(x, w):
    ms = jnp.mean(x * x, axis=-1, keepdims=True)
    return x * lax.rsqrt(ms + EPS) * w


def _dot(a, b):
    return jnp.dot(a, b, preferred_element_type=F32)


def _proj_kernel(x_ref, nw_ref, w_ref, q_ref, k_ref, v_ref, g_ref, u_ref):
    h = _rms(x_ref[...], nw_ref[...]).astype(BF16)
    for c, o_ref in enumerate((q_ref, k_ref, v_ref, g_ref, u_ref)):
        width = o_ref.shape[1]
        o_ref[...] = _dot(h, w_ref[:, c * width:(c + 1) * width])


def _proj(x, norm_w, w_in_bf16, tm=512):
    t, d = x.shape
    width = w_in_bf16.shape[1] // 5
    out = jax.ShapeDtypeStruct((t, width), F32)
    return pl.pallas_call(
        _proj_kernel,
        grid=(t // tm,),
        in_specs=[pl.BlockSpec((tm, d), lambda i: (i, 0)),
                  _resident((1, d)),
                  _resident(w_in_bf16.shape)],
        out_specs=[pl.BlockSpec((tm, width), lambda i: (i, 0))] * 5,
        out_shape=[out] * 5,
        compiler_params=_params(("parallel",)),
        name="proj",
    )(x, norm_w, w_in_bf16)


def _log_gamma(h):
    return math.log1p(-(2.0 ** (-5.0 - h)))


def _retention_kernel(q_ref, k_ref, v_ref, g_ref, cos_ref, sin_ref, gn_ref, o_ref,
                      state_ref, decay_ref, qdec_ref, kdec_ref):
    rows = q_ref.shape[0]

    @pl.when(pl.program_id(0) == 0)
    def _init():
        ti = lax.broadcasted_iota(I32, (rows, rows), 0)
        si = lax.broadcasted_iota(I32, (rows, rows), 1)
        visible = lax.shift_right_logical(si, CHUNK_LOG2) <= lax.shift_right_logical(ti, CHUNK_LOG2)
        dist = jnp.abs(ti - si).astype(F32)
        pos = lax.broadcasted_iota(I32, (rows, HEAD_DIM), 0).astype(F32)
        for h in range(RET_HEADS):
            lg = _log_gamma(h)
            decay_ref[h] = jnp.where(visible, jnp.exp(lg * dist), 0.0)
            qdec_ref[h] = jnp.exp(lg * (pos + 1.0))
            kdec_ref[h] = jnp.exp(lg * (rows - 1.0 - pos))
        state_ref[...] = jnp.zeros_like(state_ref)

    cos = cos_ref[...]
    sin = sin_ref[...]
    scale = HEAD_DIM ** -0.5
    for h in range(RET_HEADS):
        sl = slice(h * HEAD_DIM, (h + 1) * HEAD_DIM)
        qh = q_ref[:, sl]
        kh = k_ref[:, sl]
        vb = v_ref[:, sl].astype(BF16)
        qr = qh * cos + pltpu.roll(qh, HEAD_DIM // 2, 1) * sin
        kr = (kh * cos + pltpu.roll(kh, HEAD_DIM // 2, 1) * sin) * scale
        scores = lax.dot_general(qr.astype(BF16), kr.astype(BF16), (((1,), (1,)), ((), ())),
                                 preferred_element_type=F32) * decay_ref[h]
        state = state_ref[h]
        y = _dot(scores.astype(BF16), vb)
        y = y + _dot((qr * qdec_ref[h]).astype(BF16), state.astype(BF16))
        kv = lax.dot_general((kr * kdec_ref[h]).astype(BF16), vb, (((0,), (0,)), ((), ())),
                             preferred_element_type=F32)
        state_ref[h] = math.exp(_log_gamma(h) * rows) * state + kv
        mu = jnp.mean(y, axis=-1, keepdims=True)
        yc = y - mu
        var = jnp.mean(yc * yc, axis=-1, keepdims=True)
        yn = yc * lax.rsqrt(var + EPS) * gn_ref[:, sl]
        gh = g_ref[:, sl]
        o_ref[:, sl] = (gh * jax.nn.sigmoid(gh) * yn).astype(o_ref.dtype)


def _retention(q, k, v, g, cos_tab, sin_tab, gn_w, rows=256):
    t, width = q.shape
    tok = pl.BlockSpec((rows, width), lambda i: (i, 0))
    tab = pl.BlockSpec((rows, HEAD_DIM), lambda i: (i, 0))
    return pl.pallas_call(
        _retention_kernel,
        grid=(t // rows,),
        in_specs=[tok, tok, tok, tok, tab, tab, _resident((1, width))],
        out_specs=tok,
        out_shape=jax.ShapeDtypeStruct((t, width), BF16),
        scratch_shapes=[pltpu.VMEM((RET_HEADS, HEAD_DIM, HEAD_DIM), F32),
                        pltpu.VMEM((RET_HEADS, rows, rows), F32),
                        pltpu.VMEM((RET_HEADS, rows, HEAD_DIM), F32),
                        pltpu.VMEM((RET_HEADS, rows, HEAD_DIM), F32)],
        compiler_params=_params(("arbitrary",)),
        name="retention",
    )(q, k, v, g, cos_tab, sin_tab, gn_w)


def _cmul(ar, ai, br, bi):
    return ar * br - ai * bi, ar * bi + ai * br


def _s5_kernel(u_ref, m_ref, bst_ref, cout_ref, pw_ref, y_ref, a_ref, pr_ref, pi_ref, carry_ref):
    tq = u_ref.shape[0]
    nn = tq // S5_CHUNK
    nb = nn // SUBLANES
    half = S5_HALF

    @pl.when(pl.program_id(1) == 0)
    def _():
        carry_ref[...] = jnp.zeros_like(carry_ref)

    for l in range(S5_CHUNK):
        a_ref[:, l * LANES:(l + 1) * LANES] = u_ref[pl.ds(l, nn, stride=S5_CHUNK), :].astype(BF16)
    a = a_ref[...]
    kt = 2 * LANES
    yc = [_dot(a_ref[:, :(n + 1) * kt], m_ref[:(n + 1) * kt, n * kt:(n + 1) * kt])
          for n in range(S5_CHUNK // 2)]
    s_in = _dot(a, bst_ref[...])
    sr, si = s_in[:, :half], s_in[:, half:]

    row = lax.broadcasted_iota(I32, (nn, half), 0)
    zr = jnp.where(row == 0, carry_ref[0:1, :], pltpu.roll(sr, 1, 0))
    zi = jnp.where(row == 0, carry_ref[1:2, :], pltpu.roll(si, 1, 0))
    in_block = jnp.bitwise_and(row, SUBLANES - 1)
    for d in (1, 2, 4):
        mr, mi = _cmul(pw_ref[0, d - 1:d, :], pw_ref[1, d - 1:d, :],
                       pltpu.roll(zr, d, 0), pltpu.roll(zi, d, 0))
        keep = in_block >= d
        zr = zr + jnp.where(keep, mr, 0.0)
        zi = zi + jnp.where(keep, mi, 0.0)
    pr_ref[...] = zr.reshape(nb, SUBLANES, half)
    pi_ref[...] = zi.reshape(nb, SUBLANES, half)

    wr, wi = pw_ref[0], pw_ref[1]

    def block_step(r, c):
        fr, fi = _cmul(wr, wi, c[0], c[1])
        nr = pr_ref[r] + fr
        ni = pi_ref[r] + fi
        pr_ref[r] = nr
        pi_ref[r] = ni
        last = slice(SUBLANES - 1, SUBLANES)
        return (jnp.broadcast_to(nr[last, :], (SUBLANES, half)),
                jnp.broadcast_to(ni[last, :], (SUBLANES, half)))

    zero = jnp.zeros((SUBLANES, half), F32)
    lr, li = lax.fori_loop(0, nb, block_step, (zero, zero))
    er, ei = _cmul(wr[0:1, :], wi[0:1, :], lr[0:1, :], li[0:1, :])
    carry_ref[0:1, :] = er + sr[nn - 1:nn, :]
    carry_ref[1:2, :] = ei + si[nn - 1:nn, :]

    p = jnp.concatenate([pr_ref[...].reshape(nn, half), pi_ref[...].reshape(nn, half)], axis=-1)
    y_cross = _dot(p.astype(BF16), cout_ref[...])
    for l in range(S5_CHUNK):
        y_intra = yc[l // 2][:, (l % 2) * LANES:(l % 2 + 1) * LANES]
        y_ref[pl.ds(l, nn, stride=S5_CHUNK), :] = y_intra + y_cross[:, l * LANES:(l + 1) * LANES]


def _s5(u, m_op, bst_op, cout_op, pw_op, tq):
    t, width = u.shape
    nn = tq // S5_CHUNK
    kdim = S5_CHUNK * LANES

    def tile_block(*shape):
        n = len(shape)
        return pl.BlockSpec((None,) + shape, lambda o, i: (o,) + (0,) * n)

    tok = pl.BlockSpec((tq, LANES), lambda o, i: (i, o))
    return pl.pallas_call(
        _s5_kernel,
        grid=(width // LANES, t // tq),
        in_specs=[tok, tile_block(kdim, kdim), tile_block(kdim, 2 * S5_HALF),
                  tile_block(2 * S5_HALF, kdim), tile_block(2, SUBLANES, S5_HALF)],
        out_specs=tok,
        out_shape=jax.ShapeDtypeStruct((t, width), F32),
        scratch_shapes=[pltpu.VMEM((nn, kdim), BF16),
                        pltpu.VMEM((nn // SUBLANES, SUBLANES, S5_HALF), F32),
                        pltpu.VMEM((nn // SUBLANES, SUBLANES, S5_HALF), F32),
                        pltpu.VMEM((2, S5_HALF), F32)],
        compiler_params=_params(("parallel", "arbitrary")),
        name="s5",
    )(u, m_op, bst_op, cout_op, pw_op)


def _s5_operators(lam_re, lam_im, log_dt, b_re, b_im, c_re, c_im):
    g, p, cg = b_re.shape
    lc, nt, tg = S5_CHUNK, S5_TILES, S5_TILE_GROUPS
    lam = lax.complex(jnp.minimum(lam_re, -1e-4), lam_im)
    log_lam_bar = lam * jnp.exp(log_dt)[:, None]
    lam_bar = jnp.exp(log_lam_bar)
    b_bar = ((lam_bar - 1.0) / lam)[..., None] * lax.complex(b_re, b_im)
    c_mat = lax.complex(c_re, c_im)

    def power(n):
        n = jnp.asarray(n, F32)
        return jnp.exp(n[:, None, None].astype(jnp.complex64) * log_lam_bar)

    lags = jnp.arange(lc)
    taps = jnp.real(jnp.einsum('gap,dgp,gpc->dgac', c_mat, power(lags), b_bar,
                               precision=lax.Precision.HIGHEST))
    taps = taps.reshape(lc, nt, tg, cg, cg).transpose(1, 2, 4, 0, 3).reshape(nt, tg * cg, lc * cg)
    taps = jnp.pad(taps, ((0, 0), (0, 0), ((lc - 1) * cg, 0)))
    kc = jnp.stack([taps[:, :, (lc - 1 - l) * cg:(2 * lc - 1 - l) * cg] for l in range(lc)], axis=1)
    kc = kc.reshape(nt, lc * tg * cg, lc * cg)

    bst = power(lc - 1 - lags)[:, :, None, :] * b_bar.transpose(0, 2, 1)[None]
    bst = bst.reshape(lc, nt, tg * cg, p).transpose(1, 0, 2, 3).reshape(nt, lc * tg * cg, p)
    bc = jnp.concatenate([jnp.real(bst), jnp.imag(bst)], axis=-1)

    pw_out = jnp.exp((lags + 1).astype(jnp.complex64)[None, None, :] * log_lam_bar[:, :, None])
    cout = pw_out[:, :, :, None] * c_mat.transpose(0, 2, 1)[:, :, None, :]
    cout = cout.reshape(nt, tg * p, lc * cg)
    cc = jnp.concatenate([jnp.real(cout), -jnp.imag(cout)], axis=1)

    def replicate(n_in, size):
        i = jnp.arange(n_in)[:, None]
        j = jnp.arange(n_in * tg)[None, :]
        return jnp.logical_and(i // size == j // (tg * size), i % size == j % size).astype(BF16)

    def spread(compact, size_in, row_size, col_size):
        full = jnp.einsum('trk,kc->trc', compact.astype(BF16), replicate(compact.shape[2], size_in),
                          preferred_element_type=F32)
        row_group = (jnp.arange(full.shape[1]) // row_size) % tg
        col_group = (jnp.arange(full.shape[2]) // col_size) % tg
        return jnp.where(row_group[:, None] == col_group[None, :], full, 0.0).astype(BF16)

    m_op = spread(kc, cg, cg, cg)
    bst_op = spread(bc, p, cg, p)
    cout_op = spread(cc, cg, p, cg)

    pw = power(lc * (1 + jnp.arange(SUBLANES)))
    pw = pw.reshape(SUBLANES, nt, tg * p).transpose(1, 0, 2)
    pw_op = jnp.stack([jnp.real(pw), jnp.imag(pw)], axis=1).astype(F32)
    return m_op, bst_op, cout_op, pw_op


def _merge_kernel(x_ref, nw_ref, yret_ref, ys_ref, u_ref, d_ref, wglu_ref, wm_ref, wa_ref, wb_ref,
                  o_ref):
    d_model = x_ref.shape[1]
    h = _rms(x_ref[...], nw_ref[...]).astype(BF16)
    y = jax.nn.gelu(ys_ref[...] + d_ref[...] * u_ref[...])
    y_ssm = (y * jax.nn.sigmoid(_dot(y.astype(BF16), wglu_ref[...]))).astype(BF16)
    y_ret = yret_ref[...]
    cw = 512
    for c in range(d_model // cw):
        cs = slice(c * cw, (c + 1) * cw)
        gs = slice(d_model + c * cw, d_model + (c + 1) * cw)
        g0 = jax.nn.sigmoid(_dot(h, wm_ref[:, cs]))
        g1 = jax.nn.sigmoid(_dot(h, wm_ref[:, gs]))
        o_ref[:, cs] = (g0 * _dot(y_ret, wa_ref[:, cs]) + g1 * _dot(y_ssm, wb_ref[:, cs])).astype(o_ref.dtype)


def _merge(x, norm_w, y_ret, y_s5, u, ssm_d, w_glu, w_merge, w_a, w_b, tm=512):
    t, d = x.shape
    width = y_ret.shape[1]
    tok = lambda w: pl.BlockSpec((tm, w), lambda i: (i, 0))
    return pl.pallas_call(
        _merge_kernel,
        grid=(t // tm,),
        in_specs=[tok(d), _resident((1, d)), tok(width), tok(width), tok(width), _resident((1, width)),
                  _resident(w_glu.shape), _resident(w_merge.shape), _resident(w_a.shape),
                  _resident(w_b.shape)],
        out_specs=tok(d),
        out_shape=jax.ShapeDtypeStruct((t, d), BF16),
        compiler_params=_params(("parallel",)),
        name="merge",
    )(x, norm_w, y_ret, y_s5, u, ssm_d, w_glu, w_merge, w_a, w_b)


def _first_argmax(vals, lane):
    top = jnp.max(vals, axis=-1, keepdims=True)
    idx = jnp.min(jnp.where(vals == top, lane, ROUTER_LANES), axis=-1, keepdims=True)
    return top, idx


def _bf16_bits(x):
    return lax.bitcast_convert_type(x.astype(BF16).astype(F32), U32)


def _packed_store(ref, value, lead=()):
    rows, d = value.shape
    for i in range(PACK):
        lo = lax.shift_right_logical(_bf16_bits(value[:, i * LANES:(i + 1) * LANES]), U32(16))
        hi = _bf16_bits(value[:, d // 2 + i * LANES:d // 2 + (i + 1) * LANES])
        ref[lead + (pl.ds(i, rows, stride=PACK), slice(None))] = jnp.bitwise_or(hi, lo)


def _packed_pieces(ref, i, rows, lead=()):
    word = ref[lead + (pl.ds(i, rows, stride=PACK), slice(None))]
    lo = lax.bitcast_convert_type(lax.shift_left(word, U32(16)), F32)
    hi = lax.bitcast_convert_type(jnp.bitwise_and(word, U32(0xFFFF0000)), F32)
    return lo, hi


def _outproj_kernel(m_ref, x_ref, wo_ref, nw_ref, wr_ref, br_ref, x1_ref, xp_ref, ri_ref, rw_ref, cnt_ref):
    x1 = x_ref[...] + _dot(m_ref[...], wo_ref[...])
    x1_ref[...] = x1
    _packed_store(xp_ref, x1)
    h2 = _rms(x1, nw_ref[...])
    h_hi = h2.astype(BF16)
    h_lo = (h2 - h_hi.astype(F32)).astype(BF16)
    part = _dot(h_hi, wr_ref[...])
    logits = (part[:, :ROUTER_LANES] + part[:, ROUTER_LANES:]
              + _dot(h_lo, wr_ref[:, :ROUTER_LANES]) + br_ref[...])

    lane = lax.broadcasted_iota(I32, logits.shape, 1)
    neg = -jnp.inf
    is_group = lane < N_GROUPS
    g_top, g_idx = _first_argmax(jnp.where(is_group, logits, neg), lane)
    g_w = 1.0 / jnp.sum(jnp.where(is_group, jnp.exp(logits - g_top), 0.0), axis=-1, keepdims=True)
    first = N_GROUPS + EXPERTS_PER_GROUP * g_idx
    e_logits = jnp.where(lane >= first, jnp.where(lane < first + EXPERTS_PER_GROUP, logits, neg), neg)
    t1, i1 = _first_argmax(e_logits, lane)
    t2, i2 = _first_argmax(jnp.where(lane == i1, neg, e_logits), lane)
    ratio = jnp.exp(t2 - t1)
    w1 = g_w / (1.0 + ratio)
    w2 = w1 * ratio
    e1 = i1 - N_GROUPS
    e2 = i2 - N_GROUPS
    ri_ref[...] = jnp.where(lane == 0, e1, jnp.where(lane == 1, e2, 0))
    rw_ref[...] = jnp.where(lane == 0, w1, jnp.where(lane == 1, w2, 0.0))

    @pl.when(pl.program_id(0) == 0)
    def _():
        cnt_ref[...] = jnp.zeros_like(cnt_ref)

    picked = jnp.where(lane == e1, 1.0, 0.0) + jnp.where(lane == e2, 1.0, 0.0)
    cnt_ref[...] += jnp.sum(picked, axis=0, keepdims=True)


def _outproj(merged, x, w_out, norm_w, w_router, b_router, tm=512):
    t, d = x.shape
    tok = lambda w: pl.BlockSpec((tm, w), lambda i: (i, 0))
    return pl.pallas_call(
        _outproj_kernel,
        grid=(t // tm,),
        in_specs=[tok(d), tok(d), _resident(w_out.shape), _resident((1, d)),
                  _resident(w_router.shape), _resident((1, ROUTER_LANES))],
        out_specs=[tok(d), pl.BlockSpec((tm * PACK, LANES), lambda i: (i, 0)), tok(ROUTER_LANES),
                   tok(ROUTER_LANES), pl.BlockSpec((1, ROUTER_LANES), lambda i: (0, 0))],
        out_shape=[jax.ShapeDtypeStruct((t, d), F32),
                   jax.ShapeDtypeStruct((t * PACK, LANES), U32),
                   jax.ShapeDtypeStruct((t, ROUTER_LANES), I32),
                   jax.ShapeDtypeStruct((t, ROUTER_LANES), F32),
                   jax.ShapeDtypeStruct((1, ROUTER_LANES), F32)],
        compiler_params=_params(("arbitrary",)),
        name="outproj",
    )(merged, x, w_out, norm_w, w_router, b_router)


def _moe_kernel(be_ref, nu_ref, tok_ref, dst_ref,
                x_hbm, nw_ref, wg_ref, wu_ref, wd_ref,
                out_hbm,
                xbuf, ybuf, zbuf, h_ref, wgu_bf, wd_bf, gsem, ssem):
    s = pl.program_id(0)
    n_used = nu_ref[0]
    rows, d_model = h_ref.shape
    f = wg_ref.shape[1]
    slot = lax.rem(s, 2)
    other = 1 - slot
    n_xbuf = xbuf.shape[0]
    xslot = lax.rem(s, n_xbuf)
    xslot_ahead = lax.rem(s + n_xbuf - 1, n_xbuf)

    def gather_row(blk, buf_slot, r):
        tok = pl.multiple_of(tok_ref[blk * rows + r], PACK)
        return pltpu.make_async_copy(x_hbm.at[pl.ds(tok, PACK)], xbuf.at[buf_slot, pl.ds(r * PACK, PACK)],
                                     gsem.at[buf_slot])

    def scatter_row(blk, buf_slot, r):
        dst = pl.multiple_of(dst_ref[(blk + 1) * rows + r], PACK)
        return pltpu.make_async_copy(ybuf.at[buf_slot, pl.ds(r * PACK, PACK)], out_hbm.at[pl.ds(dst, PACK)],
                                     ssem.at[buf_slot])

    def wait_slot(buf, sem, buf_slot):
        pltpu.make_async_copy(buf.at[buf_slot], buf.at[buf_slot], sem.at[buf_slot]).wait()

    @pl.when(s == 0)
    def _prologue():
        ybuf[...] = jnp.zeros_like(ybuf)
        zbuf[...] = jnp.zeros_like(zbuf)
        for blk in range(n_xbuf - 1):
            for r in range(rows):
                gather_row(blk, blk, r).start()

    @pl.when(jnp.logical_and(s < n_used,
                             jnp.logical_or(s == 0, be_ref[s] != be_ref[jnp.maximum(s - 1, 0)])))
    def _cast_weights():
        wgu_bf[:, :f] = wg_ref[...].astype(BF16)
        wgu_bf[:, f:] = wu_ref[...].astype(BF16)
        wd_bf[...] = wd_ref[...].astype(BF16)

    @pl.when(s < n_used)
    def _block():
        wait_slot(xbuf, gsem, xslot)
        for r in range(rows):
            scatter_row(s - 1, other, r).start()
        ss = jnp.zeros((rows, 1), F32)
        for i in range(PACK):
            lo, hi = _packed_pieces(xbuf, i, rows, (xslot,))
            ss = ss + jnp.sum(lo * lo + hi * hi, axis=-1, keepdims=True)
        inv = lax.rsqrt(ss * (1.0 / d_model) + EPS)
        for i in range(PACK):
            lo, hi = _packed_pieces(xbuf, i, rows, (xslot,))
            lo_cols = slice(i * LANES, (i + 1) * LANES)
            hi_cols = slice(d_model // 2 + i * LANES, d_model // 2 + (i + 1) * LANES)
            h_ref[:, lo_cols] = (lo * inv * nw_ref[:, lo_cols]).astype(BF16)
            h_ref[:, hi_cols] = (hi * inv * nw_ref[:, hi_cols]).astype(BF16)
        for r in range(rows):
            gather_row(s + n_xbuf - 1, xslot_ahead, r).start()
        gu = _dot(h_ref[...], wgu_bf[...])
        gate = gu[:, :f]
        act = (gate * jax.nn.sigmoid(gate) * gu[:, f:]).astype(BF16)
        y = _dot(act, wd_bf[...])

        @pl.when(s >= 1)
        def _free_slot():
            wait_slot(ybuf, ssem, slot)

        _packed_store(ybuf, y, (slot,))

    @pl.when(s >= n_used)
    def _tail():
        wait_slot(ybuf, ssem, slot)

        @pl.when(s == n_used)
        def _last_block():
            for ahead in range(n_xbuf - 1):
                wait_slot(xbuf, gsem, lax.rem(s + ahead, n_xbuf))
            for r in range(rows):
                scatter_row(s - 1, other, r).start()

        @pl.when(s > n_used)
        def _padding_block():
            dst = pl.multiple_of(dst_ref[s * rows], PACK)
            pltpu.make_async_copy(zbuf, out_hbm.at[pl.ds(dst, rows * PACK)], ssem.at[other]).start()

    @pl.when(s == pl.num_programs(0) - 1)
    def _drain():
        wait_slot(ybuf, ssem, other)


def _moe(block_expert, n_used, row_tok, row_dst, x1_packed, norm_w, w_gate, w_up, w_down):
    n_steps = block_expert.shape[0]
    _, d, f = w_gate.shape
    assert d == 2 * PACK * LANES

    def expert_block(shape):
        return pl.BlockSpec((None,) + shape, lambda b, be, *_: (be[b], 0, 0))

    grid_spec = pltpu.PrefetchScalarGridSpec(
        num_scalar_prefetch=4,
        grid=(n_steps,),
        in_specs=[pl.BlockSpec(memory_space=pl.ANY),
                  pl.BlockSpec((1, d), lambda b, *_: (0, 0)),
                  expert_block((d, f)), expert_block((d, f)), expert_block((f, d))],
        out_specs=pl.BlockSpec(memory_space=pl.ANY),
        scratch_shapes=[pltpu.VMEM((MOE_GATHER_SLOTS, MOE_ROWS * PACK, LANES), U32),
                        pltpu.VMEM((2, MOE_ROWS * PACK, LANES), U32),
                        pltpu.VMEM((MOE_ROWS * PACK, LANES), U32),
                        pltpu.VMEM((MOE_ROWS, d), BF16),
                        pltpu.VMEM((d, 2 * f), BF16),
                        pltpu.VMEM((f, d), BF16),
                        pltpu.SemaphoreType.DMA((MOE_GATHER_SLOTS,)),
                        pltpu.SemaphoreType.DMA((2,))],
    )
    return pl.pallas_call(
        _moe_kernel,
        grid_spec=grid_spec,
        out_shape=jax.ShapeDtypeStruct((row_dst.shape[0] * PACK, LANES), U32),
        compiler_params=pltpu.CompilerParams(dimension_semantics=("arbitrary",),
                                             vmem_limit_bytes=VMEM_LIMIT_BYTES,
                                             has_side_effects=True),
        name="moe",
    )(block_expert, n_used, row_tok * PACK, row_dst * PACK, x1_packed, norm_w, w_gate, w_up, w_down)


def _row_layout(expert, counts):
    t = expert.shape[0]
    n_assign = 2 * t
    n_blocks = n_assign // MOE_ROWS + N_EXPERTS
    flat_e = expert.reshape(-1)
    order = jnp.argsort(flat_e).astype(I32)
    start = jnp.cumsum(counts) - counts
    padded = (counts + MOE_ROWS - 1) // MOE_ROWS * MOE_ROWS
    pad_end = jnp.cumsum(padded)
    pad_start = pad_end - padded
    block_start = jnp.arange(n_blocks, dtype=I32) * MOE_ROWS
    n_rows = jnp.sum(padded)
    used = block_start < n_rows
    block_expert = jnp.minimum(jnp.sum(pad_end[None, :] <= block_start[:, None], axis=1, dtype=I32),
                               N_EXPERTS - 1)
    last_expert = jnp.max(jnp.where(used, block_expert, 0))
    block_expert = jnp.where(used, block_expert, last_expert)
    is_expert = block_expert[:, None] == jnp.arange(N_EXPERTS, dtype=I32)[None, :]

    def of_block(per_expert):
        return jnp.sum(jnp.where(is_expert, per_expert[None, :], 0), axis=1, dtype=I32)

    rank0 = block_start - of_block(pad_start)
    n_valid = jnp.where(used, jnp.clip(of_block(counts) - rank0, 0, MOE_ROWS), 0).astype(I32)
    within = jnp.arange(MOE_ROWS, dtype=I32)[None, :]
    valid = within < n_valid[:, None]
    src = order[jnp.clip(of_block(start)[:, None] + rank0[:, None] + within, 0, n_assign - 1)]
    tok = src // 2
    n_pad = MOE_ROWS - n_valid
    spare = n_assign + MOE_ROWS + (jnp.cumsum(n_pad) - n_pad)[:, None] + (within - n_valid[:, None])
    row_dst = jnp.where(valid, (src % 2) * t + tok, spare).astype(I32)
    row_dst = jnp.concatenate([n_assign + within, row_dst], axis=0).reshape(-1)
    row_tok = jnp.concatenate([jnp.where(valid, tok, 0), jnp.zeros((2, MOE_ROWS), I32)], axis=0).reshape(-1)
    block_expert = jnp.concatenate([block_expert, last_expert[None]])
    n_used = (n_rows // MOE_ROWS).astype(I32).reshape(1)
    return block_expert, n_used, row_tok, row_dst


def _final_kernel(x1_ref, r0_ref, r1_ref, rw_ref, p_ref, nple_ref, wg_ref, wp_ref, nf_ref, o_ref, x2_ref):
    tm, d = x2_ref.shape
    w0, w1 = rw_ref[:, 0:1], rw_ref[:, 1:2]
    for i in range(PACK):
        lo0, hi0 = _packed_pieces(r0_ref, i, tm)
        lo1, hi1 = _packed_pieces(r1_ref, i, tm)
        lo_cols = slice(i * LANES, (i + 1) * LANES)
        hi_cols = slice(d // 2 + i * LANES, d // 2 + (i + 1) * LANES)
        x2_ref[:, lo_cols] = x1_ref[:, lo_cols] + (w0 * lo0 + w1 * lo1)
        x2_ref[:, hi_cols] = x1_ref[:, hi_cols] + (w0 * hi0 + w1 * hi1)
    x2 = x2_ref[...]
    h3 = _rms(x2, nple_ref[...]).astype(BF16)
    gate = jax.nn.sigmoid(_dot(h3, wg_ref[...]))
    x3 = x2 + gate * _dot(p_ref[...].astype(BF16), wp_ref[...])
    o_ref[...] = _rms(x3, nf_ref[...])


def _final(x1, moe_packed, route_w, p, norm_ple, w_ple_gate, w_ple, norm_f, tm=512):
    t, d = x1.shape
    nt = t // tm
    tok = lambda w: pl.BlockSpec((tm, w), lambda i: (i, 0))
    packed = lambda first: pl.BlockSpec((tm * PACK, LANES), lambda i: (i + first, 0))
    return pl.pallas_call(
        _final_kernel,
        grid=(nt,),
        in_specs=[tok(d), packed(0), packed(nt), tok(ROUTER_LANES),
                  tok(p.shape[1]), _resident((1, d)), _resident(w_ple_gate.shape),
                  _resident(w_ple.shape), _resident((1, d))],
        out_specs=tok(d),
        out_shape=jax.ShapeDtypeStruct((t, d), F32),
        scratch_shapes=[pltpu.VMEM((tm, d), F32)],
        compiler_params=_params(("parallel",)),
        name="final",
    )(x1, moe_packed, moe_packed, route_w, p, norm_ple, w_ple_gate, w_ple, norm_f)


def _rope_tables(t):
    half = HEAD_DIM // 2
    inv_freq = ROPE_THETA ** (-jnp.arange(half, dtype=F32) / half)
    ang = jnp.arange(t, dtype=F32)[:, None] * inv_freq[None, :]
    cos, sin = jnp.cos(ang), jnp.sin(ang)
    return jnp.concatenate([cos, cos], axis=-1), jnp.concatenate([-sin, sin], axis=-1)


def _layer(x, p, norm_mix, w_in, ret_gn_w, lam_re, lam_im, log_dt, b_re, b_im, c_re, c_im, ssm_d,
           w_glu, w_branch_a, w_branch_b, w_merge, w_out, norm_ffn, w_rg, b_rg, w_re, b_re_router,
           w_exp_gate, w_exp_up, w_exp_down, norm_ple, w_ple_gate, w_ple, norm_f):
    t, d = x.shape
    row = lambda v: v.reshape(1, -1).astype(F32)

    q, k, v, g, u = _proj(x, row(norm_mix), w_in.astype(BF16))
    cos_tab, sin_tab = _rope_tables(t)
    y_ret = _retention(q, k, v, g, cos_tab, sin_tab, row(ret_gn_w))

    ops = _s5_operators(lam_re, lam_im, log_dt, b_re, b_im, c_re, c_im)
    y_s5 = _s5(u, *ops, tq=min(t, 4096))

    merged = _merge(x, row(norm_mix), y_ret, y_s5, u, row(ssm_d), w_glu.astype(BF16),
                    w_merge.astype(BF16), w_branch_a.astype(BF16), w_branch_b.astype(BF16))

    pad = ROUTER_LANES - N_GROUPS - N_EXPERTS
    w_router = jnp.concatenate([w_rg, w_re, jnp.zeros((d, pad), F32)], axis=1).astype(F32)
    w_router_hi = w_router.astype(BF16)
    w_router_lo = (w_router - w_router_hi.astype(F32)).astype(BF16)
    b_router = jnp.concatenate([b_rg, b_re_router, jnp.zeros((pad,), F32)]).reshape(1, ROUTER_LANES).astype(F32)
    x1, x1_packed, route_i, route_w, counts = _outproj(
        merged, x, w_out.astype(BF16), row(norm_ffn),
        jnp.concatenate([w_router_hi, w_router_lo], axis=1), b_router)

    block_expert, n_used, row_tok, row_dst = _row_layout(route_i[:, :2], counts[0, :N_EXPERTS].astype(I32))
    moe_rows = _moe(block_expert, n_used, row_tok, row_dst, x1_packed, row(norm_ffn),
                    w_exp_gate, w_exp_up, w_exp_down)

    return _final(x1, moe_rows, route_w, p, row(norm_ple), w_ple_gate.astype(BF16), w_ple.astype(BF16),
                  row(norm_f))


def kernel(x, p, norm_mix, w_in, ret_gn_w, ssm_lam_re, ssm_lam_im, ssm_log_dt, ssm_b_re, ssm_b_im, ssm_c_re, ssm_c_im, ssm_d, w_glu, w_branch_a, w_branch_b, w_merge, w_out, norm_ffn, w_router_group, b_router_group, w_router_expert, b_router_expert, w_exp_gate, w_exp_up, w_exp_down, norm_ple, w_ple_gate, w_ple, norm_f):
    depth, bsz, seq, _ = p.shape
    assert depth == 1 and bsz == 1, "single layer, single sequence"
    out = _layer(x[0], p[0, 0], norm_mix[0], w_in[0], ret_gn_w[0], ssm_lam_re[0], ssm_lam_im[0],
                 ssm_log_dt[0], ssm_b_re[0], ssm_b_im[0], ssm_c_re[0], ssm_c_im[0], ssm_d[0], w_glu[0],
                 w_branch_a[0], w_branch_b[0], w_merge[0], w_out[0], norm_ffn[0], w_router_group[0],
                 b_router_group[0], w_router_expert[0], b_router_expert[0], w_exp_gate[0], w_exp_up[0],
                 w_exp_down[0], norm_ple[0], w_ple_gate[0], w_ple[0], norm_f)
    return out[None]
```

```python
import math

import jax
import jax.numpy as jnp
from jax import lax
from jax.experimental import pallas as pl
from jax.experimental.pallas import tpu as pltpu

F32 = jnp.float32
BF16 = jnp.bfloat16
I32 = jnp.int32
U32 = jnp.uint32

EPS = 1e-6
LANES = 128
SUBLANES = 8
CHUNK_LOG2 = 6
RET_HEADS = 8
HEAD_DIM = 128
ROPE_THETA = 10000.0
SSM_GROUPS = 64
SSM_GROUP_SIZE = 16
SSM_STATE = 64
S5_CHUNK = 16
S5_TILE_GROUPS = LANES // SSM_GROUP_SIZE
S5_TILES = SSM_GROUPS // S5_TILE_GROUPS
S5_HALF = S5_TILE_GROUPS * SSM_STATE
N_GROUPS = 4
EXPERTS_PER_GROUP = 8
N_EXPERTS = 32
ROUTER_LANES = LANES
MOE_ROWS = 256
MOE_GATHER_SLOTS = 3
PACK = 8

VMEM_LIMIT_BYTES = 56 * 1024 * 1024


def _params(sem):
    return pltpu.CompilerParams(dimension_semantics=sem, vmem_limit_bytes=VMEM_LIMIT_BYTES)


def _resident(shape):
    n = len(shape)
    return pl.BlockSpec(shape, lambda *_: (0,) * n, pipeline_mode=pl.Buffered(1))


def _rms(x, w):
    ms = jnp.mean(x * x, axis=-1, keepdims=True)
    return x * lax.rsqrt(ms + EPS) * w


def _dot(a, b):
    return jnp.dot(a, b, preferred_element_type=F32)


def _proj_kernel(x_ref, nw_ref, w_ref, q_ref, k_ref, v_ref, g_ref, u_ref):
    h = _rms(x_ref[...], nw_ref[...]).astype(BF16)
    for c, o_ref in enumerate((q_ref, k_ref, v_ref, g_ref, u_ref)):
        width = o_ref.shape[1]
        o_ref[...] = _dot(h, w_ref[:, c * width:(c + 1) * width])


def _proj(x, norm_w, w_in_bf16, tm=512):
    t, d = x.shape
    width = w_in_bf16.shape[1] // 5
    out = jax.ShapeDtypeStruct((t, width), F32)
    return pl.pallas_call(
        _proj_kernel,
        grid=(t // tm,),
        in_specs=[pl.BlockSpec((tm, d), lambda i: (i, 0)),
                  _resident((1, d)),
                  _resident(w_in_bf16.shape)],
        out_specs=[pl.BlockSpec((tm, width), lambda i: (i, 0))] * 5,
        out_shape=[out] * 5,
        compiler_params=_params(("parallel",)),
        name="proj",
    )(x, norm_w, w_in_bf16)


def _log_gamma(h):
    return math.log1p(-(2.0 ** (-5.0 - h)))


def _retention_kernel(q_ref, k_ref, v_ref, g_ref, rope_row_ref, rope_tile_ref, gn_ref, o_ref,
                      state_ref, decay_ref, qdec_ref, kdec_ref):
    rows = q_ref.shape[0]

    @pl.when(pl.program_id(0) == 0)
    def _init():
        ti = lax.broadcasted_iota(I32, (rows, rows), 0)
        si = lax.broadcasted_iota(I32, (rows, rows), 1)
        visible = lax.shift_right_logical(si, CHUNK_LOG2) <= lax.shift_right_logical(ti, CHUNK_LOG2)
        dist = jnp.abs(ti - si).astype(F32)
        pos = lax.broadcasted_iota(I32, (rows, HEAD_DIM), 0).astype(F32)
        for h in range(RET_HEADS):
            lg = _log_gamma(h)
            decay_ref[h] = jnp.where(visible, jnp.exp(lg * dist), 0.0)
            qdec_ref[h] = jnp.exp(lg * (pos + 1.0))
            kdec_ref[h] = jnp.exp(lg * (rows - 1.0 - pos))
        state_ref[...] = jnp.zeros_like(state_ref)

    cos_r, sin_r = rope_row_ref[0], rope_row_ref[1]
    cos_t, sin_t = rope_tile_ref[0], rope_tile_ref[1]
    cos = cos_r * cos_t - sin_r * sin_t
    sin = sin_r * cos_t + cos_r * sin_t
    first_half = lax.broadcasted_iota(I32, sin.shape, 1) < HEAD_DIM // 2
    sin = jnp.where(first_half, -sin, sin)
    scale = HEAD_DIM ** -0.5
    for h in range(RET_HEADS):
        sl = slice(h * HEAD_DIM, (h + 1) * HEAD_DIM)
        qh = q_ref[:, sl]
        kh = k_ref[:, sl]
        vb = v_ref[:, sl].astype(BF16)
        qr = qh * cos + pltpu.roll(qh, HEAD_DIM // 2, 1) * sin
        kr = (kh * cos + pltpu.roll(kh, HEAD_DIM // 2, 1) * sin) * scale
        scores = lax.dot_general(qr.astype(BF16), kr.astype(BF16), (((1,), (1,)), ((), ())),
                                 preferred_element_type=F32) * decay_ref[h]
        state = state_ref[h]
        y = _dot(scores.astype(BF16), vb)
        y = y + _dot((qr * qdec_ref[h]).astype(BF16), state.astype(BF16))
        kv = lax.dot_general((kr * kdec_ref[h]).astype(BF16), vb, (((0,), (0,)), ((), ())),
                             preferred_element_type=F32)
        state_ref[h] = math.exp(_log_gamma(h) * rows) * state + kv
        mu = jnp.mean(y, axis=-1, keepdims=True)
        yc = y - mu
        var = jnp.mean(yc * yc, axis=-1, keepdims=True)
        yn = yc * lax.rsqrt(var + EPS) * gn_ref[:, sl]
        gh = g_ref[:, sl]
        o_ref[:, sl] = (gh * jax.nn.sigmoid(gh) * yn).astype(o_ref.dtype)


def _retention(q, k, v, g, gn_w, rows=256):
    t, width = q.shape
    tok = pl.BlockSpec((rows, width), lambda i: (i, 0))
    half = HEAD_DIM // 2
    inv_freq = ROPE_THETA ** (-jnp.arange(half, dtype=F32) / half)

    def cos_sin(positions):
        ang = positions.astype(F32)[:, None] * inv_freq[None, :]
        ang = jnp.concatenate([ang, ang], axis=-1)
        return jnp.stack([jnp.cos(ang), jnp.sin(ang)])

    rope_row = cos_sin(jnp.arange(rows))
    rope_tile = cos_sin(jnp.arange(t // rows) * rows).transpose(1, 0, 2)[:, :, None, :]
    return pl.pallas_call(
        _retention_kernel,
        grid=(t // rows,),
        in_specs=[tok, tok, tok, tok, _resident((2, rows, HEAD_DIM)),
                  pl.BlockSpec((None, 2, 1, HEAD_DIM), lambda i: (i, 0, 0, 0)), _resident((1, width))],
        out_specs=tok,
        out_shape=jax.ShapeDtypeStruct((t, width), BF16),
        scratch_shapes=[pltpu.VMEM((RET_HEADS, HEAD_DIM, HEAD_DIM), F32),
                        pltpu.VMEM((RET_HEADS, rows, rows), F32),
                        pltpu.VMEM((RET_HEADS, rows, HEAD_DIM), F32),
                        pltpu.VMEM((RET_HEADS, rows, HEAD_DIM), F32)],
        compiler_params=_params(("arbitrary",)),
        name="retention",
    )(q, k, v, g, rope_row, rope_tile, gn_w)


def _cmul(ar, ai, br, bi):
    return ar * br - ai * bi, ar * bi + ai * br


def _s5_kernel(u_ref, m_ref, bst_ref, cout_ref, pw_ref, y_ref, a_ref, pr_ref, pi_ref, carry_ref):
    tq = u_ref.shape[0]
    nn = tq // S5_CHUNK
    nb = nn // SUBLANES
    half = S5_HALF

    @pl.when(pl.program_id(1) == 0)
    def _():
        carry_ref[...] = jnp.zeros_like(carry_ref)

    for l in range(S5_CHUNK):
        a_ref[:, l * LANES:(l + 1) * LANES] = u_ref[pl.ds(l, nn, stride=S5_CHUNK), :].astype(BF16)
    a = a_ref[...]
    kt = 2 * LANES
    yc = [_dot(a_ref[:, :(n + 1) * kt], m_ref[:(n + 1) * kt, n * kt:(n + 1) * kt])
          for n in range(S5_CHUNK // 2)]
    s_in = _dot(a, bst_ref[...])
    sr, si = s_in[:, :half], s_in[:, half:]

    row = lax.broadcasted_iota(I32, (nn, half), 0)
    zr = jnp.where(row == 0, carry_ref[0:1, :], pltpu.roll(sr, 1, 0))
    zi = jnp.where(row == 0, carry_ref[1:2, :], pltpu.roll(si, 1, 0))
    in_block = jnp.bitwise_and(row, SUBLANES - 1)
    for d in (1, 2, 4):
        mr, mi = _cmul(pw_ref[0, d - 1:d, :], pw_ref[1, d - 1:d, :],
                       pltpu.roll(zr, d, 0), pltpu.roll(zi, d, 0))
        keep = in_block >= d
        zr = zr + jnp.where(keep, mr, 0.0)
        zi = zi + jnp.where(keep, mi, 0.0)
    pr_ref[...] = zr.reshape(nb, SUBLANES, half)
    pi_ref[...] = zi.reshape(nb, SUBLANES, half)

    wr, wi = pw_ref[0], pw_ref[1]

    def block_step(r, c):
        fr, fi = _cmul(wr, wi, c[0], c[1])
        nr = pr_ref[r] + fr
        ni = pi_ref[r] + fi
        pr_ref[r] = nr
        pi_ref[r] = ni
        last = slice(SUBLANES - 1, SUBLANES)
        return (jnp.broadcast_to(nr[last, :], (SUBLANES, half)),
                jnp.broadcast_to(ni[last, :], (SUBLANES, half)))

    zero = jnp.zeros((SUBLANES, half), F32)
    lr, li = lax.fori_loop(0, nb, block_step, (zero, zero))
    er, ei = _cmul(wr[0:1, :], wi[0:1, :], lr[0:1, :], li[0:1, :])
    carry_ref[0:1, :] = er + sr[nn - 1:nn, :]
    carry_ref[1:2, :] = ei + si[nn - 1:nn, :]

    p = jnp.concatenate([pr_ref[...].reshape(nn, half), pi_ref[...].reshape(nn, half)], axis=-1)
    y_cross = _dot(p.astype(BF16), cout_ref[...])
    for l in range(S5_CHUNK):
        y_intra = yc[l // 2][:, (l % 2) * LANES:(l % 2 + 1) * LANES]
        y_ref[pl.ds(l, nn, stride=S5_CHUNK), :] = y_intra + y_cross[:, l * LANES:(l + 1) * LANES]


def _s5(u, m_op, bst_op, cout_op, pw_op, tq):
    t, width = u.shape
    nn = tq // S5_CHUNK
    kdim = S5_CHUNK * LANES

    def tile_block(*shape):
        n = len(shape)
        return pl.BlockSpec((None,) + shape, lambda o, i: (o,) + (0,) * n)

    tok = pl.BlockSpec((tq, LANES), lambda o, i: (i, o))
    return pl.pallas_call(
        _s5_kernel,
        grid=(width // LANES, t // tq),
        in_specs=[tok, tile_block(kdim, kdim), tile_block(kdim, 2 * S5_HALF),
                  tile_block(2 * S5_HALF, kdim), tile_block(2, SUBLANES, S5_HALF)],
        out_specs=tok,
        out_shape=jax.ShapeDtypeStruct((t, width), F32),
        scratch_shapes=[pltpu.VMEM((nn, kdim), BF16),
                        pltpu.VMEM((nn // SUBLANES, SUBLANES, S5_HALF), F32),
                        pltpu.VMEM((nn // SUBLANES, SUBLANES, S5_HALF), F32),
                        pltpu.VMEM((2, S5_HALF), F32)],
        compiler_params=_params(("parallel", "arbitrary")),
        name="s5",
    )(u, m_op, bst_op, cout_op, pw_op)


def _s5_operators(lam_re, lam_im, log_dt, b_re, b_im, c_re, c_im):
    g, p, cg = b_re.shape
    lc, nt, tg = S5_CHUNK, S5_TILES, S5_TILE_GROUPS
    lam = lax.complex(jnp.minimum(lam_re, -1e-4), lam_im)
    log_lam_bar = lam * jnp.exp(log_dt)[:, None]
    lam_bar = jnp.exp(log_lam_bar)
    b_bar = ((lam_bar - 1.0) / lam)[..., None] * lax.complex(b_re, b_im)
    c_mat = lax.complex(c_re, c_im)

    def power(n):
        n = jnp.asarray(n, F32)
        return jnp.exp(n[:, None, None].astype(jnp.complex64) * log_lam_bar)

    lags = jnp.arange(lc)
    taps = jnp.real(jnp.einsum('gap,dgp,gpc->dgac', c_mat, power(lags), b_bar,
                               precision=lax.Precision.HIGHEST))
    taps = taps.reshape(lc, nt, tg, cg, cg).transpose(1, 2, 4, 0, 3).reshape(nt, tg * cg, lc * cg)
    taps = jnp.pad(taps, ((0, 0), (0, 0), ((lc - 1) * cg, 0)))
    kc = jnp.stack([taps[:, :, (lc - 1 - l) * cg:(2 * lc - 1 - l) * cg] for l in range(lc)], axis=1)
    kc = kc.reshape(nt, lc * tg * cg, lc * cg)

    bst = power(lc - 1 - lags)[:, :, None, :] * b_bar.transpose(0, 2, 1)[None]
    bst = bst.reshape(lc, nt, tg * cg, p).transpose(1, 0, 2, 3).reshape(nt, lc * tg * cg, p)
    bc = jnp.concatenate([jnp.real(bst), jnp.imag(bst)], axis=-1)

    pw_out = jnp.exp((lags + 1).astype(jnp.complex64)[None, None, :] * log_lam_bar[:, :, None])
    cout = pw_out[:, :, :, None] * c_mat.transpose(0, 2, 1)[:, :, None, :]
    cout = cout.reshape(nt, tg * p, lc * cg)
    cc = jnp.concatenate([jnp.real(cout), -jnp.imag(cout)], axis=1)

    def replicate(n_in, size):
        i = jnp.arange(n_in)[:, None]
        j = jnp.arange(n_in * tg)[None, :]
        return jnp.logical_and(i // size == j // (tg * size), i % size == j % size).astype(BF16)

    def spread(compact, size_in, row_size, col_size):
        full = jnp.einsum('trk,kc->trc', compact.astype(BF16), replicate(compact.shape[2], size_in),
                          preferred_element_type=F32)
        row_group = (jnp.arange(full.shape[1]) // row_size) % tg
        col_group = (jnp.arange(full.shape[2]) // col_size) % tg
        return jnp.where(row_group[:, None] == col_group[None, :], full, 0.0).astype(BF16)

    m_op = spread(kc, cg, cg, cg)
    bst_op = spread(bc, p, cg, p)
    cout_op = spread(cc, cg, p, cg)

    pw = power(lc * (1 + jnp.arange(SUBLANES)))
    pw = pw.reshape(SUBLANES, nt, tg * p).transpose(1, 0, 2)
    pw_op = jnp.stack([jnp.real(pw), jnp.imag(pw)], axis=1).astype(F32)
    return m_op, bst_op, cout_op, pw_op


def _merge_kernel(x_ref, nw_ref, yret_ref, ys_ref, u_ref, d_ref, wglu_ref, wm_ref, wa_ref, wb_ref,
                  o_ref):
    d_model = x_ref.shape[1]
    h = _rms(x_ref[...], nw_ref[...]).astype(BF16)
    y = jax.nn.gelu(ys_ref[...] + d_ref[...] * u_ref[...])
    y_ssm = (y * jax.nn.sigmoid(_dot(y.astype(BF16), wglu_ref[...]))).astype(BF16)
    y_ret = yret_ref[...]
    cw = 512
    for c in range(d_model // cw):
        cs = slice(c * cw, (c + 1) * cw)
        gs = slice(d_model + c * cw, d_model + (c + 1) * cw)
        g0 = jax.nn.sigmoid(_dot(h, wm_ref[:, cs]))
        g1 = jax.nn.sigmoid(_dot(h, wm_ref[:, gs]))
        o_ref[:, cs] = (g0 * _dot(y_ret, wa_ref[:, cs]) + g1 * _dot(y_ssm, wb_ref[:, cs])).astype(o_ref.dtype)


def _merge(x, norm_w, y_ret, y_s5, u, ssm_d, w_glu, w_merge, w_a, w_b, tm=512):
    t, d = x.shape
    width = y_ret.shape[1]
    tok = lambda w: pl.BlockSpec((tm, w), lambda i: (i, 0))
    return pl.pallas_call(
        _merge_kernel,
        grid=(t // tm,),
        in_specs=[tok(d), _resident((1, d)), tok(width), tok(width), tok(width), _resident((1, width)),
                  _resident(w_glu.shape), _resident(w_merge.shape), _resident(w_a.shape),
                  _resident(w_b.shape)],
        out_specs=tok(d),
        out_shape=jax.ShapeDtypeStruct((t, d), BF16),
        compiler_params=_params(("parallel",)),
        name="merge",
    )(x, norm_w, y_ret, y_s5, u, ssm_d, w_glu, w_merge, w_a, w_b)


def _first_argmax(vals, lane):
    top = jnp.max(vals, axis=-1, keepdims=True)
    idx = jnp.min(jnp.where(vals == top, lane, ROUTER_LANES), axis=-1, keepdims=True)
    return top, idx


def _bf16_bits(x):
    return lax.bitcast_convert_type(x.astype(BF16).astype(F32), U32)


def _packed_store(ref, value, lead=()):
    rows, d = value.shape
    for i in range(PACK):
        lo = lax.shift_right_logical(_bf16_bits(value[:, i * LANES:(i + 1) * LANES]), U32(16))
        hi = _bf16_bits(value[:, d // 2 + i * LANES:d // 2 + (i + 1) * LANES])
        ref[lead + (pl.ds(i, rows, stride=PACK), slice(None))] = jnp.bitwise_or(hi, lo)


def _packed_pieces(ref, i, rows, lead=()):
    word = ref[lead + (pl.ds(i, rows, stride=PACK), slice(None))]
    lo = lax.bitcast_convert_type(lax.shift_left(word, U32(16)), F32)
    hi = lax.bitcast_convert_type(jnp.bitwise_and(word, U32(0xFFFF0000)), F32)
    return lo, hi


def _outproj_kernel(m_ref, x_ref, wo_ref, nw_ref, wr_ref, br_ref, x1_ref, xp_ref, ri_ref, rw_ref, cnt_ref):
    x1 = x_ref[...] + _dot(m_ref[...], wo_ref[...])
    x1_ref[...] = x1
    _packed_store(xp_ref, x1)
    h2 = _rms(x1, nw_ref[...])
    h_hi = h2.astype(BF16)
    h_lo = (h2 - h_hi.astype(F32)).astype(BF16)
    part = _dot(h_hi, wr_ref[...])
    logits = (part[:, :ROUTER_LANES] + part[:, ROUTER_LANES:]
              + _dot(h_lo, wr_ref[:, :ROUTER_LANES]) + br_ref[...])

    lane = lax.broadcasted_iota(I32, logits.shape, 1)
    neg = -jnp.inf
    is_group = lane < N_GROUPS
    g_top, g_idx = _first_argmax(jnp.where(is_group, logits, neg), lane)
    g_w = 1.0 / jnp.sum(jnp.where(is_group, jnp.exp(logits - g_top), 0.0), axis=-1, keepdims=True)
    first = N_GROUPS + EXPERTS_PER_GROUP * g_idx
    e_logits = jnp.where(lane >= first, jnp.where(lane < first + EXPERTS_PER_GROUP, logits, neg), neg)
    t1, i1 = _first_argmax(e_logits, lane)
    t2, i2 = _first_argmax(jnp.where(lane == i1, neg, e_logits), lane)
    ratio = jnp.exp(t2 - t1)
    w1 = g_w / (1.0 + ratio)
    w2 = w1 * ratio
    e1 = i1 - N_GROUPS
    e2 = i2 - N_GROUPS
    ri_ref[...] = jnp.where(lane == 0, e1, jnp.where(lane == 1, e2, 0))
    rw_ref[...] = jnp.where(lane == 0, w1, jnp.where(lane == 1, w2, 0.0))

    @pl.when(pl.program_id(0) == 0)
    def _():
        cnt_ref[...] = jnp.zeros_like(cnt_ref)

    picked = jnp.where(lane == e1, 1.0, 0.0) + jnp.where(lane == e2, 1.0, 0.0)
    cnt_ref[...] += jnp.sum(picked, axis=0, keepdims=True)


def _outproj(merged, x, w_out, norm_w, w_router, b_router, tm=512):
    t, d = x.shape
    tok = lambda w: pl.BlockSpec((tm, w), lambda i: (i, 0))
    return pl.pallas_call(
        _outproj_kernel,
        grid=(t // tm,),
        in_specs=[tok(d), tok(d), _resident(w_out.shape), _resident((1, d)),
                  _resident(w_router.shape), _resident((1, ROUTER_LANES))],
        out_specs=[tok(d), pl.BlockSpec((tm * PACK, LANES), lambda i: (i, 0)), tok(ROUTER_LANES),
                   tok(ROUTER_LANES), pl.BlockSpec((1, ROUTER_LANES), lambda i: (0, 0))],
        out_shape=[jax.ShapeDtypeStruct((t, d), F32),
                   jax.ShapeDtypeStruct((t * PACK, LANES), U32),
                   jax.ShapeDtypeStruct((t, ROUTER_LANES), I32),
                   jax.ShapeDtypeStruct((t, ROUTER_LANES), F32),
                   jax.ShapeDtypeStruct((1, ROUTER_LANES), F32)],
        compiler_params=_params(("arbitrary",)),
        name="outproj",
    )(merged, x, w_out, norm_w, w_router, b_router)


def _moe_kernel(be_ref, nu_ref, tok_ref, dst_ref,
                x_hbm, nw_ref, wg_ref, wu_ref, wd_ref,
                out_hbm,
                xbuf, ybuf, zbuf, h_ref, wgu_bf, wd_bf, gsem, ssem):
    s = pl.program_id(0)
    n_used = nu_ref[0]
    rows, d_model = h_ref.shape
    f = wg_ref.shape[1]
    slot = lax.rem(s, 2)
    other = 1 - slot
    n_xbuf = xbuf.shape[0]
    xslot = lax.rem(s, n_xbuf)
    xslot_ahead = lax.rem(s + n_xbuf - 1, n_xbuf)

    def gather_row(blk, buf_slot, r):
        tok = pl.multiple_of(tok_ref[blk * rows + r], PACK)
        return pltpu.make_async_copy(x_hbm.at[pl.ds(tok, PACK)], xbuf.at[buf_slot, pl.ds(r * PACK, PACK)],
                                     gsem.at[buf_slot])

    def scatter_row(blk, buf_slot, r):
        dst = pl.multiple_of(dst_ref[(blk + 1) * rows + r], PACK)
        return pltpu.make_async_copy(ybuf.at[buf_slot, pl.ds(r * PACK, PACK)], out_hbm.at[pl.ds(dst, PACK)],
                                     ssem.at[buf_slot])

    def wait_slot(buf, sem, buf_slot):
        pltpu.make_async_copy(buf.at[buf_slot], buf.at[buf_slot], sem.at[buf_slot]).wait()

    @pl.when(s == 0)
    def _prologue():
        ybuf[...] = jnp.zeros_like(ybuf)
        zbuf[...] = jnp.zeros_like(zbuf)
        for blk in range(n_xbuf - 1):
            for r in range(rows):
                gather_row(blk, blk, r).start()

    @pl.when(jnp.logical_and(s < n_used,
                             jnp.logical_or(s == 0, be_ref[s] != be_ref[jnp.maximum(s - 1, 0)])))
    def _cast_weights():
        wgu_bf[:, :f] = wg_ref[...].astype(BF16)
        wgu_bf[:, f:] = wu_ref[...].astype(BF16)
        wd_bf[...] = wd_ref[...].astype(BF16)

    @pl.when(s < n_used)
    def _block():
        wait_slot(xbuf, gsem, xslot)
        for r in range(rows):
            scatter_row(s - 1, other, r).start()
        ss = jnp.zeros((rows, 1), F32)
        for i in range(PACK):
            lo, hi = _packed_pieces(xbuf, i, rows, (xslot,))
            ss = ss + jnp.sum(lo * lo + hi * hi, axis=-1, keepdims=True)
        inv = lax.rsqrt(ss * (1.0 / d_model) + EPS)
        for i in range(PACK):
            lo, hi = _packed_pieces(xbuf, i, rows, (xslot,))
            lo_cols = slice(i * LANES, (i + 1) * LANES)
            hi_cols = slice(d_model // 2 + i * LANES, d_model // 2 + (i + 1) * LANES)
            h_ref[:, lo_cols] = (lo * inv * nw_ref[:, lo_cols]).astype(BF16)
            h_ref[:, hi_cols] = (hi * inv * nw_ref[:, hi_cols]).astype(BF16)
        for r in range(rows):
            gather_row(s + n_xbuf - 1, xslot_ahead, r).start()
        gu = _dot(h_ref[...], wgu_bf[...])
        gate = gu[:, :f]
        act = (gate * jax.nn.sigmoid(gate) * gu[:, f:]).astype(BF16)
        y = _dot(act, wd_bf[...])

        @pl.when(s >= 1)
        def _free_slot():
            wait_slot(ybuf, ssem, slot)

        _packed_store(ybuf, y, (slot,))

    @pl.when(s >= n_used)
    def _tail():
        wait_slot(ybuf, ssem, slot)

        @pl.when(s == n_used)
        def _last_block():
            for ahead in range(n_xbuf - 1):
                wait_slot(xbuf, gsem, lax.rem(s + ahead, n_xbuf))
            for r in range(rows):
                scatter_row(s - 1, other, r).start()

        @pl.when(s > n_used)
        def _padding_block():
            dst = pl.multiple_of(dst_ref[s * rows], PACK)
            pltpu.make_async_copy(zbuf, out_hbm.at[pl.ds(dst, rows * PACK)], ssem.at[other]).start()

    @pl.when(s == pl.num_programs(0) - 1)
    def _drain():
        wait_slot(ybuf, ssem, other)


def _moe(block_expert, n_used, row_tok, row_dst, x1_packed, norm_w, w_gate, w_up, w_down):
    n_steps = block_expert.shape[0]
    _, d, f = w_gate.shape
    assert d == 2 * PACK * LANES

    def expert_block(shape):
        return pl.BlockSpec((None,) + shape, lambda b, be, *_: (be[b], 0, 0))

    grid_spec = pltpu.PrefetchScalarGridSpec(
        num_scalar_prefetch=4,
        grid=(n_steps,),
        in_specs=[pl.BlockSpec(memory_space=pl.ANY),
                  pl.BlockSpec((1, d), lambda b, *_: (0, 0)),
                  expert_block((d, f)), expert_block((d, f)), expert_block((f, d))],
        out_specs=pl.BlockSpec(memory_space=pl.ANY),
        scratch_shapes=[pltpu.VMEM((MOE_GATHER_SLOTS, MOE_ROWS * PACK, LANES), U32),
                        pltpu.VMEM((2, MOE_ROWS * PACK, LANES), U32),
                        pltpu.VMEM((MOE_ROWS * PACK, LANES), U32),
                        pltpu.VMEM((MOE_ROWS, d), BF16),
                        pltpu.VMEM((d, 2 * f), BF16),
                        pltpu.VMEM((f, d), BF16),
                        pltpu.SemaphoreType.DMA((MOE_GATHER_SLOTS,)),
                        pltpu.SemaphoreType.DMA((2,))],
    )
    return pl.pallas_call(
        _moe_kernel,
        grid_spec=grid_spec,
        out_shape=jax.ShapeDtypeStruct((row_dst.shape[0] * PACK, LANES), U32),
        compiler_params=pltpu.CompilerParams(dimension_semantics=("arbitrary",),
                                             vmem_limit_bytes=VMEM_LIMIT_BYTES,
                                             has_side_effects=True),
        name="moe",
    )(block_expert, n_used, row_tok * PACK, row_dst * PACK, x1_packed, norm_w, w_gate, w_up, w_down)


def _row_layout(expert, counts):
    t = expert.shape[0]
    n_assign = 2 * t
    n_blocks = n_assign // MOE_ROWS + N_EXPERTS
    flat_e = expert.reshape(-1)
    order = jnp.argsort(flat_e).astype(I32)
    start = jnp.cumsum(counts) - counts
    padded = (counts + MOE_ROWS - 1) // MOE_ROWS * MOE_ROWS
    pad_end = jnp.cumsum(padded)
    pad_start = pad_end - padded
    block_start = jnp.arange(n_blocks, dtype=I32) * MOE_ROWS
    n_rows = jnp.sum(padded)
    used = block_start < n_rows
    block_expert = jnp.minimum(jnp.sum(pad_end[None, :] <= block_start[:, None], axis=1, dtype=I32),
                               N_EXPERTS - 1)
    last_expert = jnp.max(jnp.where(used, block_expert, 0))
    block_expert = jnp.where(used, block_expert, last_expert)
    is_expert = block_expert[:, None] == jnp.arange(N_EXPERTS, dtype=I32)[None, :]

    def of_block(per_expert):
        return jnp.sum(jnp.where(is_expert, per_expert[None, :], 0), axis=1, dtype=I32)

    rank0 = block_start - of_block(pad_start)
    n_valid = jnp.where(used, jnp.clip(of_block(counts) - rank0, 0, MOE_ROWS), 0).astype(I32)
    within = jnp.arange(MOE_ROWS, dtype=I32)[None, :]
    valid = within < n_valid[:, None]
    src = order[jnp.clip(of_block(start)[:, None] + rank0[:, None] + within, 0, n_assign - 1)]
    tok = src // 2
    n_pad = MOE_ROWS - n_valid
    spare = n_assign + MOE_ROWS + (jnp.cumsum(n_pad) - n_pad)[:, None] + (within - n_valid[:, None])
    row_dst = jnp.where(valid, (src % 2) * t + tok, spare).astype(I32)
    row_dst = jnp.concatenate([n_assign + within, row_dst], axis=0).reshape(-1)
    row_tok = jnp.concatenate([jnp.where(valid, tok, 0), jnp.zeros((2, MOE_ROWS), I32)], axis=0).reshape(-1)
    block_expert = jnp.concatenate([block_expert, last_expert[None]])
    n_used = (n_rows // MOE_ROWS).astype(I32).reshape(1)
    return block_expert, n_used, row_tok, row_dst


def _final_kernel(x1_ref, r0_ref, r1_ref, rw_ref, p_ref, nple_ref, wg_ref, wp_ref, nf_ref, o_ref, x2_ref):
    tm, d = x2_ref.shape
    w0, w1 = rw_ref[:, 0:1], rw_ref[:, 1:2]
    for i in range(PACK):
        lo0, hi0 = _packed_pieces(r0_ref, i, tm)
        lo1, hi1 = _packed_pieces(r1_ref, i, tm)
        lo_cols = slice(i * LANES, (i + 1) * LANES)
        hi_cols = slice(d // 2 + i * LANES, d // 2 + (i + 1) * LANES)
        x2_ref[:, lo_cols] = x1_ref[:, lo_cols] + (w0 * lo0 + w1 * lo1)
        x2_ref[:, hi_cols] = x1_ref[:, hi_cols] + (w0 * hi0 + w1 * hi1)
    x2 = x2_ref[...]
    h3 = _rms(x2, nple_ref[...]).astype(BF16)
    gate = jax.nn.sigmoid(_dot(h3, wg_ref[...]))
    x3 = x2 + gate * _dot(p_ref[...].astype(BF16), wp_ref[...])
    o_ref[...] = _rms(x3, nf_ref[...])


def _final(x1, moe_packed, route_w, p, norm_ple, w_ple_gate, w_ple, norm_f, tm=512):
    t, d = x1.shape
    nt = t // tm
    tok = lambda w: pl.BlockSpec((tm, w), lambda i: (i, 0))
    packed = lambda first: pl.BlockSpec((tm * PACK, LANES), lambda i: (i + first, 0))
    return pl.pallas_call(
        _final_kernel,
        grid=(nt,),
        in_specs=[tok(d), packed(0), packed(nt), tok(ROUTER_LANES),
                  tok(p.shape[1]), _resident((1, d)), _resident(w_ple_gate.shape),
                  _resident(w_ple.shape), _resident((1, d))],
        out_specs=tok(d),
        out_shape=jax.ShapeDtypeStruct((t, d), F32),
        scratch_shapes=[pltpu.VMEM((tm, d), F32)],
        compiler_params=_params(("parallel",)),
        name="final",
    )(x1, moe_packed, moe_packed, route_w, p, norm_ple, w_ple_gate, w_ple, norm_f)


def _layer(x, p, norm_mix, w_in, ret_gn_w, lam_re, lam_im, log_dt, b_re, b_im, c_re, c_im, ssm_d,
           w_glu, w_branch_a, w_branch_b, w_merge, w_out, norm_ffn, w_rg, b_rg, w_re, b_re_router,
           w_exp_gate, w_exp_up, w_exp_down, norm_ple, w_ple_gate, w_ple, norm_f):
    t, d = x.shape
    row = lambda v: v.reshape(1, -1).astype(F32)

    q, k, v, g, u = _proj(x, row(norm_mix), w_in.astype(BF16))
    y_ret = _retention(q, k, v, g, row(ret_gn_w))

    ops = _s5_operators(lam_re, lam_im, log_dt, b_re, b_im, c_re, c_im)
    y_s5 = _s5(u, *ops, tq=min(t, 4096))

    merged = _merge(x, row(norm_mix), y_ret, y_s5, u, row(ssm_d), w_glu.astype(BF16),
                    w_merge.astype(BF16), w_branch_a.astype(BF16), w_branch_b.astype(BF16))

    pad = ROUTER_LANES - N_GROUPS - N_EXPERTS
    w_router = jnp.concatenate([w_rg, w_re, jnp.zeros((d, pad), F32)], axis=1).astype(F32)
    w_router_hi = w_router.astype(BF16)
    w_router_lo = (w_router - w_router_hi.astype(F32)).astype(BF16)
    b_router = jnp.concatenate([b_rg, b_re_router, jnp.zeros((pad,), F32)]).reshape(1, ROUTER_LANES).astype(F32)
    x1, x1_packed, route_i, route_w, counts = _outproj(
        merged, x, w_out.astype(BF16), row(norm_ffn),
        jnp.concatenate([w_router_hi, w_router_lo], axis=1), b_router)

    block_expert, n_used, row_tok, row_dst = _row_layout(route_i[:, :2], counts[0, :N_EXPERTS].astype(I32))
    moe_rows = _moe(block_expert, n_used, row_tok, row_dst, x1_packed, row(norm_ffn),
                    w_exp_gate, w_exp_up, w_exp_down)

    return _final(x1, moe_rows, route_w, p, row(norm_ple), w_ple_gate.astype(BF16), w_ple.astype(BF16),
                  row(norm_f))


def kernel(x, p, norm_mix, w_in, ret_gn_w, ssm_lam_re, ssm_lam_im, ssm_log_dt, ssm_b_re, ssm_b_im, ssm_c_re, ssm_c_im, ssm_d, w_glu, w_branch_a, w_branch_b, w_merge, w_out, norm_ffn, w_router_group, b_router_group, w_router_expert, b_router_expert, w_exp_gate, w_exp_up, w_exp_down, norm_ple, w_ple_gate, w_ple, norm_f):
    depth, bsz, seq, _ = p.shape
    assert depth == 1 and bsz == 1, "single layer, single sequence"
    out = _layer(x[0], p[0, 0], norm_mix[0], w_in[0], ret_gn_w[0], ssm_lam_re[0], ssm_lam_im[0],
                 ssm_log_dt[0], ssm_b_re[0], ssm_b_im[0], ssm_c_re[0], ssm_c_im[0], ssm_d[0], w_glu[0],
                 w_branch_a[0], w_branch_b[0], w_merge[0], w_out[0], norm_ffn[0], w_router_group[0],
                 b_router_group[0], w_router_expert[0], b_router_expert[0], w_exp_gate[0], w_exp_up[0],
                 w_exp_down[0], norm_ple[0], w_ple_gate[0], w_ple[0], norm_f)
    return out[None]
```

```python
import math

import jax
import jax.numpy as jnp
from jax import lax
from jax.experimental import pallas as pl
from jax.experimental.pallas import tpu as pltpu

F32 = jnp.float32
BF16 = jnp.bfloat16
I32 = jnp.int32
U32 = jnp.uint32

EPS = 1e-6
LANES = 128
SUBLANES = 8
CHUNK_LOG2 = 6
RET_HEADS = 8
HEAD_DIM = 128
ROPE_THETA = 10000.0
SSM_GROUPS = 64
SSM_GROUP_SIZE = 16
SSM_STATE = 64
S5_CHUNK = 16
S5_TILE_GROUPS = LANES // SSM_GROUP_SIZE
S5_TILES = SSM_GROUPS // S5_TILE_GROUPS
S5_HALF = S5_TILE_GROUPS * SSM_STATE
N_GROUPS = 4
EXPERTS_PER_GROUP = 8
N_EXPERTS = 32
ROUTER_LANES = LANES
MOE_ROWS = 256
MOE_GATHER_SLOTS = 3
PACK = 8

VMEM_LIMIT_BYTES = 56 * 1024 * 1024


def _params(sem):
    return pltpu.CompilerParams(dimension_semantics=sem, vmem_limit_bytes=VMEM_LIMIT_BYTES)


def _resident(shape):
    n = len(shape)
    return pl.BlockSpec(shape, lambda *_: (0,) * n, pipeline_mode=pl.Buffered(1))


def _rms(x, w):
    ms = jnp.mean(x * x, axis=-1, keepdims=True)
    return x * lax.rsqrt(ms + EPS) * w


def _dot(a, b):
    return jnp.dot(a, b, preferred_element_type=F32)


def _proj_kernel(x_ref, nw_ref, w_ref, q_ref, k_ref, v_ref, g_ref, u_ref):
    h = _rms(x_ref[...], nw_ref[...]).astype(BF16)
    for c, o_ref in enumerate((q_ref, k_ref, v_ref, g_ref, u_ref)):
        width = o_ref.shape[1]
        o_ref[...] = _dot(h, w_ref[:, c * width:(c + 1) * width])


def _proj(x, norm_w, w_in_bf16, tm=512):
    t, d = x.shape
    width = w_in_bf16.shape[1] // 5
    out = jax.ShapeDtypeStruct((t, width), F32)
    return pl.pallas_call(
        _proj_kernel,
        grid=(t // tm,),
        in_specs=[pl.BlockSpec((tm, d), lambda i: (i, 0)),
                  _resident((1, d)),
                  _resident(w_in_bf16.shape)],
        out_specs=[pl.BlockSpec((tm, width), lambda i: (i, 0))] * 5,
        out_shape=[out] * 5,
        compiler_params=_params(("parallel",)),
        name="proj",
    )(x, norm_w, w_in_bf16)


def _log_gamma(h):
    return math.log1p(-(2.0 ** (-5.0 - h)))


def _retention_kernel(q_ref, k_ref, v_ref, g_ref, rope_row_ref, rope_tile_ref, gn_ref, o_ref,
                      state_ref, decay_ref, qdec_ref, kdec_ref):
    rows = q_ref.shape[0]

    @pl.when(pl.program_id(0) == 0)
    def _init():
        ti = lax.broadcasted_iota(I32, (rows, rows), 0)
        si = lax.broadcasted_iota(I32, (rows, rows), 1)
        visible = lax.shift_right_logical(si, CHUNK_LOG2) <= lax.shift_right_logical(ti, CHUNK_LOG2)
        dist = jnp.abs(ti - si).astype(F32)
        pos = lax.broadcasted_iota(I32, (rows, HEAD_DIM), 0).astype(F32)
        for h in range(RET_HEADS):
            lg = _log_gamma(h)
            decay_ref[h] = jnp.where(visible, jnp.exp(lg * dist), 0.0)
            qdec_ref[h] = jnp.exp(lg * (pos + 1.0))
            kdec_ref[h] = jnp.exp(lg * (rows - 1.0 - pos))
        state_ref[...] = jnp.zeros_like(state_ref)

    cos_r, sin_r = rope_row_ref[0], rope_row_ref[1]
    cos_t, sin_t = rope_tile_ref[0], rope_tile_ref[1]
    cos = cos_r * cos_t - sin_r * sin_t
    sin = sin_r * cos_t + cos_r * sin_t
    first_half = lax.broadcasted_iota(I32, sin.shape, 1) < HEAD_DIM // 2
    sin = jnp.where(first_half, -sin, sin)
    scale = HEAD_DIM ** -0.5
    for h in range(RET_HEADS):
        sl = slice(h * HEAD_DIM, (h + 1) * HEAD_DIM)
        qh = q_ref[:, sl]
        kh = k_ref[:, sl]
        vb = v_ref[:, sl].astype(BF16)
        qr = qh * cos + pltpu.roll(qh, HEAD_DIM // 2, 1) * sin
        kr = (kh * cos + pltpu.roll(kh, HEAD_DIM // 2, 1) * sin) * scale
        scores = lax.dot_general(qr.astype(BF16), kr.astype(BF16), (((1,), (1,)), ((), ())),
                                 preferred_element_type=F32) * decay_ref[h]
        state = state_ref[h]
        y = _dot(scores.astype(BF16), vb)
        y = y + _dot((qr * qdec_ref[h]).astype(BF16), state.astype(BF16))
        kv = lax.dot_general((kr * kdec_ref[h]).astype(BF16), vb, (((0,), (0,)), ((), ())),
                             preferred_element_type=F32)
        state_ref[h] = math.exp(_log_gamma(h) * rows) * state + kv
        mu = jnp.mean(y, axis=-1, keepdims=True)
        yc = y - mu
        var = jnp.mean(yc * yc, axis=-1, keepdims=True)
        yn = yc * lax.rsqrt(var + EPS) * gn_ref[:, sl]
        gh = g_ref[:, sl]
        o_ref[:, sl] = (gh * jax.nn.sigmoid(gh) * yn).astype(o_ref.dtype)


def _retention(q, k, v, g, gn_w, rows=256):
    t, width = q.shape
    tok = pl.BlockSpec((rows, width), lambda i: (i, 0))
    half = HEAD_DIM // 2
    inv_freq = ROPE_THETA ** (-jnp.arange(half, dtype=F32) / half)

    def cos_sin(positions):
        ang = positions.astype(F32)[:, None] * inv_freq[None, :]
        ang = jnp.concatenate([ang, ang], axis=-1)
        return jnp.stack([jnp.cos(ang), jnp.sin(ang)])

    rope_row = cos_sin(jnp.arange(rows))
    rope_tile = cos_sin(jnp.arange(t // rows) * rows).transpose(1, 0, 2)[:, :, None, :]
    return pl.pallas_call(
        _retention_kernel,
        grid=(t // rows,),
        in_specs=[tok, tok, tok, tok, _resident((2, rows, HEAD_DIM)),
                  pl.BlockSpec((None, 2, 1, HEAD_DIM), lambda i: (i, 0, 0, 0)), _resident((1, width))],
        out_specs=tok,
        out_shape=jax.ShapeDtypeStruct((t, width), BF16),
        scratch_shapes=[pltpu.VMEM((RET_HEADS, HEAD_DIM, HEAD_DIM), F32),
                        pltpu.VMEM((RET_HEADS, rows, rows), F32),
                        pltpu.VMEM((RET_HEADS, rows, HEAD_DIM), F32),
                        pltpu.VMEM((RET_HEADS, rows, HEAD_DIM), F32)],
        compiler_params=_params(("arbitrary",)),
        name="retention",
    )(q, k, v, g, rope_row, rope_tile, gn_w)


def _cmul(ar, ai, br, bi):
    return ar * br - ai * bi, ar * bi + ai * br


def _s5_kernel(u_ref, m_ref, bst_ref, cout_ref, pw_ref, y_ref, a_ref, pr_ref, pi_ref, carry_ref):
    tq = u_ref.shape[0]
    nn = tq // S5_CHUNK
    nb = nn // SUBLANES
    half = S5_HALF

    @pl.when(pl.program_id(1) == 0)
    def _():
        carry_ref[...] = jnp.zeros_like(carry_ref)

    for l in range(S5_CHUNK):
        a_ref[:, l * LANES:(l + 1) * LANES] = u_ref[pl.ds(l, nn, stride=S5_CHUNK), :].astype(BF16)
    a = a_ref[...]
    kt = 2 * LANES
    yc = [_dot(a_ref[:, :(n + 1) * kt], m_ref[:(n + 1) * kt, n * kt:(n + 1) * kt])
          for n in range(S5_CHUNK // 2)]
    s_in = _dot(a, bst_ref[...])
    sr, si = s_in[:, :half], s_in[:, half:]

    row = lax.broadcasted_iota(I32, (nn, half), 0)
    zr = jnp.where(row == 0, carry_ref[0:1, :], pltpu.roll(sr, 1, 0))
    zi = jnp.where(row == 0, carry_ref[1:2, :], pltpu.roll(si, 1, 0))
    in_block = jnp.bitwise_and(row, SUBLANES - 1)
    for d in (1, 2, 4):
        mr, mi = _cmul(pw_ref[0, d - 1:d, :], pw_ref[1, d - 1:d, :],
                       pltpu.roll(zr, d, 0), pltpu.roll(zi, d, 0))
        keep = in_block >= d
        zr = zr + jnp.where(keep, mr, 0.0)
        zi = zi + jnp.where(keep, mi, 0.0)
    pr_ref[...] = zr.reshape(nb, SUBLANES, half)
    pi_ref[...] = zi.reshape(nb, SUBLANES, half)

    wr, wi = pw_ref[0], pw_ref[1]

    def block_step(r, c):
        fr, fi = _cmul(wr, wi, c[0], c[1])
        nr = pr_ref[r] + fr
        ni = pi_ref[r] + fi
        pr_ref[r] = nr
        pi_ref[r] = ni
        last = slice(SUBLANES - 1, SUBLANES)
        return (jnp.broadcast_to(nr[last, :], (SUBLANES, half)),
                jnp.broadcast_to(ni[last, :], (SUBLANES, half)))

    zero = jnp.zeros((SUBLANES, half), F32)
    lr, li = lax.fori_loop(0, nb, block_step, (zero, zero))
    er, ei = _cmul(wr[0:1, :], wi[0:1, :], lr[0:1, :], li[0:1, :])
    carry_ref[0:1, :] = er + sr[nn - 1:nn, :]
    carry_ref[1:2, :] = ei + si[nn - 1:nn, :]

    p = jnp.concatenate([pr_ref[...].reshape(nn, half), pi_ref[...].reshape(nn, half)], axis=-1)
    y_cross = _dot(p.astype(BF16), cout_ref[...])
    for l in range(S5_CHUNK):
        y_intra = yc[l // 2][:, (l % 2) * LANES:(l % 2 + 1) * LANES]
        y_ref[pl.ds(l, nn, stride=S5_CHUNK), :] = y_intra + y_cross[:, l * LANES:(l + 1) * LANES]


def _s5(u, m_op, bst_op, cout_op, pw_op, tq):
    t, width = u.shape
    nn = tq // S5_CHUNK
    kdim = S5_CHUNK * LANES

    def tile_block(*shape):
        n = len(shape)
        return pl.BlockSpec((None,) + shape, lambda o, i: (o,) + (0,) * n)

    tok = pl.BlockSpec((tq, LANES), lambda o, i: (i, o))
    return pl.pallas_call(
        _s5_kernel,
        grid=(width // LANES, t // tq),
        in_specs=[tok, tile_block(kdim, kdim), tile_block(kdim, 2 * S5_HALF),
                  tile_block(2 * S5_HALF, kdim), tile_block(2, SUBLANES, S5_HALF)],
        out_specs=tok,
        out_shape=jax.ShapeDtypeStruct((t, width), F32),
        scratch_shapes=[pltpu.VMEM((nn, kdim), BF16),
                        pltpu.VMEM((nn // SUBLANES, SUBLANES, S5_HALF), F32),
                        pltpu.VMEM((nn // SUBLANES, SUBLANES, S5_HALF), F32),
                        pltpu.VMEM((2, S5_HALF), F32)],
        compiler_params=_params(("parallel", "arbitrary")),
        name="s5",
    )(u, m_op, bst_op, cout_op, pw_op)


def _s5_operators(lam_re, lam_im, log_dt, b_re, b_im, c_re, c_im):
    g, p, cg = b_re.shape
    lc, nt, tg = S5_CHUNK, S5_TILES, S5_TILE_GROUPS
    lam = lax.complex(jnp.minimum(lam_re, -1e-4), lam_im)
    log_lam_bar = lam * jnp.exp(log_dt)[:, None]
    lam_bar = jnp.exp(log_lam_bar)
    b_bar = ((lam_bar - 1.0) / lam)[..., None] * lax.complex(b_re, b_im)
    c_mat = lax.complex(c_re, c_im)

    def power(n):
        n = jnp.asarray(n, F32)
        return jnp.exp(n[:, None, None].astype(jnp.complex64) * log_lam_bar)

    lags = jnp.arange(lc)
    taps = jnp.real(jnp.einsum('gap,dgp,gpc->dgac', c_mat, power(lags), b_bar,
                               precision=lax.Precision.HIGHEST))
    taps = taps.reshape(lc, nt, tg, cg, cg).transpose(1, 2, 4, 0, 3).reshape(nt, tg * cg, lc * cg)
    taps = jnp.pad(taps, ((0, 0), (0, 0), ((lc - 1) * cg, 0)))
    kc = jnp.stack([taps[:, :, (lc - 1 - l) * cg:(2 * lc - 1 - l) * cg] for l in range(lc)], axis=1)
    kc = kc.reshape(nt, lc * tg * cg, lc * cg)

    bst = power(lc - 1 - lags)[:, :, None, :] * b_bar.transpose(0, 2, 1)[None]
    bst = bst.reshape(lc, nt, tg * cg, p).transpose(1, 0, 2, 3).reshape(nt, lc * tg * cg, p)
    bc = jnp.concatenate([jnp.real(bst), jnp.imag(bst)], axis=-1)

    pw_out = jnp.exp((lags + 1).astype(jnp.complex64)[None, None, :] * log_lam_bar[:, :, None])
    cout = pw_out[:, :, :, None] * c_mat.transpose(0, 2, 1)[:, :, None, :]
    cout = cout.reshape(nt, tg * p, lc * cg)
    cc = jnp.concatenate([jnp.real(cout), -jnp.imag(cout)], axis=1)

    def replicate(n_in, size):
        i = jnp.arange(n_in)[:, None]
        j = jnp.arange(n_in * tg)[None, :]
        return jnp.logical_and(i // size == j // (tg * size), i % size == j % size).astype(BF16)

    def spread(compact, size_in, row_size, col_size):
        full = jnp.einsum('trk,kc->trc', compact.astype(BF16), replicate(compact.shape[2], size_in),
                          preferred_element_type=F32)
        row_group = (jnp.arange(full.shape[1]) // row_size) % tg
        col_group = (jnp.arange(full.shape[2]) // col_size) % tg
        return jnp.where(row_group[:, None] == col_group[None, :], full, 0.0).astype(BF16)

    m_op = spread(kc, cg, cg, cg)
    bst_op = spread(bc, p, cg, p)
    cout_op = spread(cc, cg, p, cg)

    pw = power(lc * (1 + jnp.arange(SUBLANES)))
    pw = pw.reshape(SUBLANES, nt, tg * p).transpose(1, 0, 2)
    pw_op = jnp.stack([jnp.real(pw), jnp.imag(pw)], axis=1).astype(F32)
    return m_op, bst_op, cout_op, pw_op


def _merge_kernel(x_ref, nw_ref, yret_ref, ys_ref, u_ref, d_ref, wglu_ref, wm_ref, wa_ref, wb_ref,
                  o_ref):
    d_model = x_ref.shape[1]
    h = _rms(x_ref[...], nw_ref[...]).astype(BF16)
    y = jax.nn.gelu(ys_ref[...] + d_ref[...] * u_ref[...])
    y_ssm = (y * jax.nn.sigmoid(_dot(y.astype(BF16), wglu_ref[...]))).astype(BF16)
    y_ret = yret_ref[...]
    cw = 512
    for c in range(d_model // cw):
        cs = slice(c * cw, (c + 1) * cw)
        gs = slice(d_model + c * cw, d_model + (c + 1) * cw)
        g0 = jax.nn.sigmoid(_dot(h, wm_ref[:, cs]))
        g1 = jax.nn.sigmoid(_dot(h, wm_ref[:, gs]))
        o_ref[:, cs] = (g0 * _dot(y_ret, wa_ref[:, cs]) + g1 * _dot(y_ssm, wb_ref[:, cs])).astype(o_ref.dtype)


def _merge(x, norm_w, y_ret, y_s5, u, ssm_d, w_glu, w_merge, w_a, w_b, tm=512):
    t, d = x.shape
    width = y_ret.shape[1]
    tok = lambda w: pl.BlockSpec((tm, w), lambda i: (i, 0))
    return pl.pallas_call(
        _merge_kernel,
        grid=(t // tm,),
        in_specs=[tok(d), _resident((1, d)), tok(width), tok(width), tok(width), _resident((1, width)),
                  _resident(w_glu.shape), _resident(w_merge.shape), _resident(w_a.shape),
                  _resident(w_b.shape)],
        out_specs=tok(d),
        out_shape=jax.ShapeDtypeStruct((t, d), BF16),
        compiler_params=_params(("parallel",)),
        name="merge",
    )(x, norm_w, y_ret, y_s5, u, ssm_d, w_glu, w_merge, w_a, w_b)


def _first_argmax(vals, lane):
    top = jnp.max(vals, axis=-1, keepdims=True)
    idx = jnp.min(jnp.where(vals == top, lane, ROUTER_LANES), axis=-1, keepdims=True)
    return top, idx


def _bf16_bits(x):
    return lax.bitcast_convert_type(x.astype(BF16).astype(F32), U32)


def _packed_store(ref, value, lead=()):
    rows, d = value.shape
    for i in range(PACK):
        lo = lax.shift_right_logical(_bf16_bits(value[:, i * LANES:(i + 1) * LANES]), U32(16))
        hi = _bf16_bits(value[:, d // 2 + i * LANES:d // 2 + (i + 1) * LANES])
        ref[lead + (pl.ds(i, rows, stride=PACK), slice(None))] = jnp.bitwise_or(hi, lo)


def _packed_pieces(ref, i, rows, lead=(), first_row=0):
    word = ref[lead + (pl.ds(first_row * PACK + i, rows, stride=PACK), slice(None))]
    lo = lax.bitcast_convert_type(lax.shift_left(word, U32(16)), F32)
    hi = lax.bitcast_convert_type(jnp.bitwise_and(word, U32(0xFFFF0000)), F32)
    return lo, hi


def _outproj_kernel(m_ref, x_ref, wo_ref, nw_ref, wr_ref, br_ref, x1_ref, xp_ref, ri_ref, rw_ref, cnt_ref):
    x1 = x_ref[...] + _dot(m_ref[...], wo_ref[...])
    x1_ref[...] = x1
    _packed_store(xp_ref, x1)
    h2 = _rms(x1, nw_ref[...])
    h_hi = h2.astype(BF16)
    h_lo = (h2 - h_hi.astype(F32)).astype(BF16)
    part = _dot(h_hi, wr_ref[...])
    logits = (part[:, :ROUTER_LANES] + part[:, ROUTER_LANES:]
              + _dot(h_lo, wr_ref[:, :ROUTER_LANES]) + br_ref[...])

    lane = lax.broadcasted_iota(I32, logits.shape, 1)
    neg = -jnp.inf
    is_group = lane < N_GROUPS
    g_top, g_idx = _first_argmax(jnp.where(is_group, logits, neg), lane)
    g_w = 1.0 / jnp.sum(jnp.where(is_group, jnp.exp(logits - g_top), 0.0), axis=-1, keepdims=True)
    first = N_GROUPS + EXPERTS_PER_GROUP * g_idx
    e_logits = jnp.where(lane >= first, jnp.where(lane < first + EXPERTS_PER_GROUP, logits, neg), neg)
    t1, i1 = _first_argmax(e_logits, lane)
    t2, i2 = _first_argmax(jnp.where(lane == i1, neg, e_logits), lane)
    ratio = jnp.exp(t2 - t1)
    w1 = g_w / (1.0 + ratio)
    w2 = w1 * ratio
    e1 = i1 - N_GROUPS
    e2 = i2 - N_GROUPS
    ri_ref[...] = jnp.where(lane == 0, e1, jnp.where(lane == 1, e2, 0))
    rw_ref[...] = jnp.where(lane == 0, w1, jnp.where(lane == 1, w2, 0.0))

    @pl.when(pl.program_id(0) == 0)
    def _():
        cnt_ref[...] = jnp.zeros_like(cnt_ref)

    picked = jnp.where(lane == e1, 1.0, 0.0) + jnp.where(lane == e2, 1.0, 0.0)
    cnt_ref[...] += jnp.sum(picked, axis=0, keepdims=True)


def _outproj(merged, x, w_out, norm_w, w_router, b_router, tm=512):
    t, d = x.shape
    tok = lambda w: pl.BlockSpec((tm, w), lambda i: (i, 0))
    return pl.pallas_call(
        _outproj_kernel,
        grid=(t // tm,),
        in_specs=[tok(d), tok(d), _resident(w_out.shape), _resident((1, d)),
                  _resident(w_router.shape), _resident((1, ROUTER_LANES))],
        out_specs=[tok(d), pl.BlockSpec((tm * PACK, LANES), lambda i: (i, 0)), tok(ROUTER_LANES),
                   tok(ROUTER_LANES), pl.BlockSpec((1, ROUTER_LANES), lambda i: (0, 0))],
        out_shape=[jax.ShapeDtypeStruct((t, d), F32),
                   jax.ShapeDtypeStruct((t * PACK, LANES), U32),
                   jax.ShapeDtypeStruct((t, ROUTER_LANES), I32),
                   jax.ShapeDtypeStruct((t, ROUTER_LANES), F32),
                   jax.ShapeDtypeStruct((1, ROUTER_LANES), F32)],
        compiler_params=_params(("arbitrary",)),
        name="outproj",
    )(merged, x, w_out, norm_w, w_router, b_router)


def _moe_kernel(be_ref, nu_ref, tok_ref,
                x_hbm, nw_ref, wg_ref, wu_ref, wd_ref,
                out_hbm,
                xbuf, ybuf, zbuf, h_ref, wgu_bf, wd_bf, gsem, ssem):
    s = pl.program_id(0)
    n_used = nu_ref[0]
    rows, d_model = h_ref.shape
    f = wg_ref.shape[1]
    slot = lax.rem(s, 2)
    other = 1 - slot
    n_xbuf = xbuf.shape[0]
    xslot = lax.rem(s, n_xbuf)
    xslot_ahead = lax.rem(s + n_xbuf - 1, n_xbuf)

    def gather_row(blk, buf_slot, r):
        tok = pl.multiple_of(tok_ref[blk * rows + r], PACK)
        return pltpu.make_async_copy(x_hbm.at[pl.ds(tok, PACK)], xbuf.at[buf_slot, pl.ds(r * PACK, PACK)],
                                     gsem.at[buf_slot])

    def block_out(blk):
        first = pl.multiple_of((blk + 1) * (rows * PACK), rows * PACK)
        return out_hbm.at[pl.ds(first, rows * PACK)]

    def wait_slot(buf, sem, buf_slot):
        pltpu.make_async_copy(buf.at[buf_slot], buf.at[buf_slot], sem.at[buf_slot]).wait()

    @pl.when(s == 0)
    def _prologue():
        ybuf[...] = jnp.zeros_like(ybuf)
        zbuf[...] = jnp.zeros_like(zbuf)
        for blk in range(n_xbuf - 1):
            for r in range(rows):
                gather_row(blk, blk, r).start()

    @pl.when(jnp.logical_and(s < n_used,
                             jnp.logical_or(s == 0, be_ref[s] != be_ref[jnp.maximum(s - 1, 0)])))
    def _cast_weights():
        wgu_bf[:, :f] = wg_ref[...].astype(BF16)
        wgu_bf[:, f:] = wu_ref[...].astype(BF16)
        wd_bf[...] = wd_ref[...].astype(BF16)

    @pl.when(s < n_used)
    def _block():
        wait_slot(xbuf, gsem, xslot)
        pltpu.make_async_copy(ybuf.at[other], block_out(s - 1), ssem.at[other]).start()
        ss = jnp.zeros((rows, 1), F32)
        for i in range(PACK):
            lo, hi = _packed_pieces(xbuf, i, rows, (xslot,))
            ss = ss + jnp.sum(lo * lo + hi * hi, axis=-1, keepdims=True)
        inv = lax.rsqrt(ss * (1.0 / d_model) + EPS)
        for i in range(PACK):
            lo, hi = _packed_pieces(xbuf, i, rows, (xslot,))
            lo_cols = slice(i * LANES, (i + 1) * LANES)
            hi_cols = slice(d_model // 2 + i * LANES, d_model // 2 + (i + 1) * LANES)
            h_ref[:, lo_cols] = (lo * inv * nw_ref[:, lo_cols]).astype(BF16)
            h_ref[:, hi_cols] = (hi * inv * nw_ref[:, hi_cols]).astype(BF16)
        for r in range(rows):
            gather_row(s + n_xbuf - 1, xslot_ahead, r).start()
        gu = _dot(h_ref[...], wgu_bf[...])
        gate = gu[:, :f]
        act = (gate * jax.nn.sigmoid(gate) * gu[:, f:]).astype(BF16)
        y = _dot(act, wd_bf[...])

        @pl.when(s >= 1)
        def _free_slot():
            wait_slot(ybuf, ssem, slot)

        _packed_store(ybuf, y, (slot,))

    @pl.when(s >= n_used)
    def _tail():
        wait_slot(ybuf, ssem, slot)

        @pl.when(s == n_used)
        def _last_block():
            for ahead in range(n_xbuf - 1):
                wait_slot(xbuf, gsem, lax.rem(s + ahead, n_xbuf))
            pltpu.make_async_copy(ybuf.at[other], block_out(s - 1), ssem.at[other]).start()

        @pl.when(s > n_used)
        def _padding_block():
            pltpu.make_async_copy(zbuf, block_out(s - 1), ssem.at[other]).start()

    @pl.when(s == pl.num_programs(0) - 1)
    def _drain():
        wait_slot(ybuf, ssem, other)


def _moe(block_expert, n_used, row_tok, x1_packed, norm_w, w_gate, w_up, w_down):
    n_steps = block_expert.shape[0]
    _, d, f = w_gate.shape
    assert d == 2 * PACK * LANES

    def expert_block(shape):
        return pl.BlockSpec((None,) + shape, lambda b, be, *_: (be[b], 0, 0))

    grid_spec = pltpu.PrefetchScalarGridSpec(
        num_scalar_prefetch=3,
        grid=(n_steps,),
        in_specs=[pl.BlockSpec(memory_space=pl.ANY),
                  pl.BlockSpec((1, d), lambda b, *_: (0, 0)),
                  expert_block((d, f)), expert_block((d, f)), expert_block((f, d))],
        out_specs=pl.BlockSpec(memory_space=pl.ANY),
        scratch_shapes=[pltpu.VMEM((MOE_GATHER_SLOTS, MOE_ROWS * PACK, LANES), U32),
                        pltpu.VMEM((2, MOE_ROWS * PACK, LANES), U32),
                        pltpu.VMEM((MOE_ROWS * PACK, LANES), U32),
                        pltpu.VMEM((MOE_ROWS, d), BF16),
                        pltpu.VMEM((d, 2 * f), BF16),
                        pltpu.VMEM((f, d), BF16),
                        pltpu.SemaphoreType.DMA((MOE_GATHER_SLOTS,)),
                        pltpu.SemaphoreType.DMA((2,))],
    )
    return pl.pallas_call(
        _moe_kernel,
        grid_spec=grid_spec,
        out_shape=jax.ShapeDtypeStruct((n_steps * MOE_ROWS * PACK, LANES), U32),
        compiler_params=pltpu.CompilerParams(dimension_semantics=("arbitrary",),
                                             vmem_limit_bytes=VMEM_LIMIT_BYTES,
                                             has_side_effects=True),
        name="moe",
    )(block_expert, n_used, row_tok * PACK, x1_packed, norm_w, w_gate, w_up, w_down)


def _row_layout(expert, counts):
    t = expert.shape[0]
    n_assign = 2 * t
    n_blocks = n_assign // MOE_ROWS + N_EXPERTS
    flat_e = expert.reshape(-1)
    order = jnp.argsort(flat_e).astype(I32)
    start = jnp.cumsum(counts) - counts
    padded = (counts + MOE_ROWS - 1) // MOE_ROWS * MOE_ROWS
    pad_end = jnp.cumsum(padded)
    pad_start = pad_end - padded
    block_start = jnp.arange(n_blocks, dtype=I32) * MOE_ROWS
    n_rows = jnp.sum(padded)
    used = block_start < n_rows
    block_expert = jnp.minimum(jnp.sum(pad_end[None, :] <= block_start[:, None], axis=1, dtype=I32),
                               N_EXPERTS - 1)
    last_expert = jnp.max(jnp.where(used, block_expert, 0))
    block_expert = jnp.where(used, block_expert, last_expert)
    is_expert = block_expert[:, None] == jnp.arange(N_EXPERTS, dtype=I32)[None, :]

    def of_block(per_expert):
        return jnp.sum(jnp.where(is_expert, per_expert[None, :], 0), axis=1, dtype=I32)

    rank0 = block_start - of_block(pad_start)
    n_valid = jnp.where(used, jnp.clip(of_block(counts) - rank0, 0, MOE_ROWS), 0).astype(I32)
    within = jnp.arange(MOE_ROWS, dtype=I32)[None, :]
    valid = within < n_valid[:, None]
    src = order[jnp.clip(of_block(start)[:, None] + rank0[:, None] + within, 0, n_assign - 1)]
    tok = src // 2
    row_tok = jnp.concatenate([jnp.where(valid, tok, 0), jnp.zeros((2, MOE_ROWS), I32)], axis=0).reshape(-1)
    block_expert = jnp.concatenate([block_expert, last_expert[None]])
    n_used = (n_rows // MOE_ROWS).astype(I32).reshape(1)
    rank = jnp.argsort(order).astype(I32)
    shift = jnp.sum(jnp.where(flat_e[:, None] == jnp.arange(N_EXPERTS, dtype=I32)[None, :],
                              (pad_start - start)[None, :], 0), axis=1, dtype=I32)
    out_row = (rank + shift + MOE_ROWS).reshape(t, 2)
    return block_expert, n_used, row_tok, out_row


def _final_kernel(pos_ref,
                  x1_ref, y_hbm, rw_ref, p_ref, nple_ref, wg_ref, wp_ref, nf_ref, o_ref,
                  x2_ref, rbuf, rsem):
    i = pl.program_id(0)
    nt = pl.num_programs(0)
    tm, d = x2_ref.shape
    slot = lax.rem(i, 2)
    other = 1 - slot

    def gather_row(tile, buf_slot, j):
        src = pl.multiple_of(pos_ref[tile * (2 * tm) + j], PACK)
        return pltpu.make_async_copy(y_hbm.at[pl.ds(src, PACK)], rbuf.at[buf_slot, pl.ds(j * PACK, PACK)],
                                     rsem.at[buf_slot])

    def wait_slot(buf_slot):
        pltpu.make_async_copy(rbuf.at[buf_slot], rbuf.at[buf_slot], rsem.at[buf_slot]).wait()

    @pl.when(i == 0)
    def _prologue():
        for j in range(2 * tm):
            gather_row(0, 0, j).start()

    wait_slot(slot)
    w0, w1 = rw_ref[:, 0:1], rw_ref[:, 1:2]
    for c in range(PACK):
        lo0, hi0 = _packed_pieces(rbuf, c, tm, (slot,))
        lo1, hi1 = _packed_pieces(rbuf, c, tm, (slot,), first_row=tm)
        lo_cols = slice(c * LANES, (c + 1) * LANES)
        hi_cols = slice(d // 2 + c * LANES, d // 2 + (c + 1) * LANES)
        x2_ref[:, lo_cols] = x1_ref[:, lo_cols] + (w0 * lo0 + w1 * lo1)
        x2_ref[:, hi_cols] = x1_ref[:, hi_cols] + (w0 * hi0 + w1 * hi1)
    next_tile = lax.rem(i + 1, nt)
    for j in range(2 * tm):
        gather_row(next_tile, other, j).start()
    x2 = x2_ref[...]
    h3 = _rms(x2, nple_ref[...]).astype(BF16)
    gate = jax.nn.sigmoid(_dot(h3, wg_ref[...]))
    x3 = x2 + gate * _dot(p_ref[...].astype(BF16), wp_ref[...])
    o_ref[...] = _rms(x3, nf_ref[...])

    @pl.when(i == nt - 1)
    def _drain():
        wait_slot(other)


def _final(x1, moe_packed, out_row, route_w, p, norm_ple, w_ple_gate, w_ple, norm_f, tm=512):
    t, d = x1.shape
    nt = t // tm
    tok = lambda w: pl.BlockSpec((tm, w), lambda i, *_: (i, 0))
    pos = (out_row * PACK).reshape(nt, tm, 2).transpose(0, 2, 1).reshape(-1)
    grid_spec = pltpu.PrefetchScalarGridSpec(
        num_scalar_prefetch=1,
        grid=(nt,),
        in_specs=[tok(d), pl.BlockSpec(memory_space=pl.ANY), tok(ROUTER_LANES),
                  tok(p.shape[1]), _resident((1, d)), _resident(w_ple_gate.shape),
                  _resident(w_ple.shape), _resident((1, d))],
        out_specs=tok(d),
        scratch_shapes=[pltpu.VMEM((tm, d), F32),
                        pltpu.VMEM((2, 2 * tm * PACK, LANES), U32),
                        pltpu.SemaphoreType.DMA((2,))],
    )
    return pl.pallas_call(
        _final_kernel,
        grid_spec=grid_spec,
        out_shape=jax.ShapeDtypeStruct((t, d), F32),
        compiler_params=_params(("arbitrary",)),
        name="final",
    )(pos, x1, moe_packed, route_w, p, norm_ple, w_ple_gate, w_ple, norm_f)


def _layer(x, p, norm_mix, w_in, ret_gn_w, lam_re, lam_im, log_dt, b_re, b_im, c_re, c_im, ssm_d,
           w_glu, w_branch_a, w_branch_b, w_merge, w_out, norm_ffn, w_rg, b_rg, w_re, b_re_router,
           w_exp_gate, w_exp_up, w_exp_down, norm_ple, w_ple_gate, w_ple, norm_f):
    t, d = x.shape
    row = lambda v: v.reshape(1, -1).astype(F32)

    q, k, v, g, u = _proj(x, row(norm_mix), w_in.astype(BF16))
    y_ret = _retention(q, k, v, g, row(ret_gn_w))

    ops = _s5_operators(lam_re, lam_im, log_dt, b_re, b_im, c_re, c_im)
    y_s5 = _s5(u, *ops, tq=min(t, 4096))

    merged = _merge(x, row(norm_mix), y_ret, y_s5, u, row(ssm_d), w_glu.astype(BF16),
                    w_merge.astype(BF16), w_branch_a.astype(BF16), w_branch_b.astype(BF16))

    pad = ROUTER_LANES - N_GROUPS - N_EXPERTS
    w_router = jnp.concatenate([w_rg, w_re, jnp.zeros((d, pad), F32)], axis=1).astype(F32)
    w_router_hi = w_router.astype(BF16)
    w_router_lo = (w_router - w_router_hi.astype(F32)).astype(BF16)
    b_router = jnp.concatenate([b_rg, b_re_router, jnp.zeros((pad,), F32)]).reshape(1, ROUTER_LANES).astype(F32)
    x1, x1_packed, route_i, route_w, counts = _outproj(
        merged, x, w_out.astype(BF16), row(norm_ffn),
        jnp.concatenate([w_router_hi, w_router_lo], axis=1), b_router)

    block_expert, n_used, row_tok, out_row = _row_layout(route_i[:, :2], counts[0, :N_EXPERTS].astype(I32))
    moe_rows = _moe(block_expert, n_used, row_tok, x1_packed, row(norm_ffn),
                    w_exp_gate, w_exp_up, w_exp_down)

    return _final(x1, moe_rows, out_row, route_w, p, row(norm_ple), w_ple_gate.astype(BF16),
                  w_ple.astype(BF16), row(norm_f))


def kernel(x, p, norm_mix, w_in, ret_gn_w, ssm_lam_re, ssm_lam_im, ssm_log_dt, ssm_b_re, ssm_b_im, ssm_c_re, ssm_c_im, ssm_d, w_glu, w_branch_a, w_branch_b, w_merge, w_out, norm_ffn, w_router_group, b_router_group, w_router_expert, b_router_expert, w_exp_gate, w_exp_up, w_exp_down, norm_ple, w_ple_gate, w_ple, norm_f):
    depth, bsz, seq, _ = p.shape
    assert depth == 1 and bsz == 1, "single layer, single sequence"
    out = _layer(x[0], p[0, 0], norm_mix[0], w_in[0], ret_gn_w[0], ssm_lam_re[0], ssm_lam_im[0],
                 ssm_log_dt[0], ssm_b_re[0], ssm_b_im[0], ssm_c_re[0], ssm_c_im[0], ssm_d[0], w_glu[0],
                 w_branch_a[0], w_branch_b[0], w_merge[0], w_out[0], norm_ffn[0], w_router_group[0],
                 b_router_group[0], w_router_expert[0], b_router_expert[0], w_exp_gate[0], w_exp_up[0],
                 w_exp_down[0], norm_ple[0], w_ple_gate[0], w_ple[0], norm_f)
    return out[None]
```

```python
import math

import jax
import jax.numpy as jnp
from jax import lax
from jax.experimental import pallas as pl
from jax.experimental.pallas import tpu as pltpu

F32 = jnp.float32
BF16 = jnp.bfloat16
I32 = jnp.int32
U32 = jnp.uint32

EPS = 1e-6
LANES = 128
SUBLANES = 8
CHUNK_LOG2 = 6
RET_HEADS = 8
HEAD_DIM = 128
ROPE_THETA = 10000.0
SSM_GROUPS = 64
SSM_GROUP_SIZE = 16
SSM_STATE = 64
S5_CHUNK = 16
S5_TILE_GROUPS = LANES // SSM_GROUP_SIZE
S5_TILES = SSM_GROUPS // S5_TILE_GROUPS
S5_HALF = S5_TILE_GROUPS * SSM_STATE
N_GROUPS = 4
EXPERTS_PER_GROUP = 8
N_EXPERTS = 32
ROUTER_LANES = LANES
MOE_ROWS = 512
MOE_GATHER_SLOTS = 3
PACK = 8

VMEM_LIMIT_BYTES = 56 * 1024 * 1024


def _params(sem):
    return pltpu.CompilerParams(dimension_semantics=sem, vmem_limit_bytes=VMEM_LIMIT_BYTES)


def _resident(shape):
    n = len(shape)
    return pl.BlockSpec(shape, lambda *_: (0,) * n, pipeline_mode=pl.Buffered(1))


def _rms(x, w):
    ms = jnp.mean(x * x, axis=-1, keepdims=True)
    return x * lax.rsqrt(ms + EPS) * w


def _dot(a, b):
    return jnp.dot(a, b, preferred_element_type=F32)


def _proj_kernel(x_ref, nw_ref, w_ref, q_ref, k_ref, v_ref, g_ref, u_ref):
    h = _rms(x_ref[...], nw_ref[...]).astype(BF16)
    for c, o_ref in enumerate((q_ref, k_ref, v_ref, g_ref, u_ref)):
        width = o_ref.shape[1]
        o_ref[...] = _dot(h, w_ref[:, c * width:(c + 1) * width])


def _proj(x, norm_w, w_in_bf16, tm=512):
    t, d = x.shape
    width = w_in_bf16.shape[1] // 5
    out = jax.ShapeDtypeStruct((t, width), F32)
    return pl.pallas_call(
        _proj_kernel,
        grid=(t // tm,),
        in_specs=[pl.BlockSpec((tm, d), lambda i: (i, 0)),
                  _resident((1, d)),
                  _resident(w_in_bf16.shape)],
        out_specs=[pl.BlockSpec((tm, width), lambda i: (i, 0))] * 5,
        out_shape=[out] * 5,
        compiler_params=_params(("parallel",)),
        name="proj",
    )(x, norm_w, w_in_bf16)


def _log_gamma(h):
    return math.log1p(-(2.0 ** (-5.0 - h)))


def _retention_kernel(q_ref, k_ref, v_ref, g_ref, rope_row_ref, rope_tile_ref, gn_ref, o_ref,
                      state_ref, decay_ref, qdec_ref, kdec_ref):
    rows = q_ref.shape[0]

    @pl.when(pl.program_id(0) == 0)
    def _init():
        ti = lax.broadcasted_iota(I32, (rows, rows), 0)
        si = lax.broadcasted_iota(I32, (rows, rows), 1)
        visible = lax.shift_right_logical(si, CHUNK_LOG2) <= lax.shift_right_logical(ti, CHUNK_LOG2)
        dist = jnp.abs(ti - si).astype(F32)
        pos = lax.broadcasted_iota(I32, (rows, HEAD_DIM), 0).astype(F32)
        for h in range(RET_HEADS):
            lg = _log_gamma(h)
            decay_ref[h] = jnp.where(visible, jnp.exp(lg * dist), 0.0)
            qdec_ref[h] = jnp.exp(lg * (pos + 1.0))
            kdec_ref[h] = jnp.exp(lg * (rows - 1.0 - pos))
        state_ref[...] = jnp.zeros_like(state_ref)

    cos_r, sin_r = rope_row_ref[0], rope_row_ref[1]
    cos_t, sin_t = rope_tile_ref[0], rope_tile_ref[1]
    cos = cos_r * cos_t - sin_r * sin_t
    sin = sin_r * cos_t + cos_r * sin_t
    first_half = lax.broadcasted_iota(I32, sin.shape, 1) < HEAD_DIM // 2
    sin = jnp.where(first_half, -sin, sin)
    scale = HEAD_DIM ** -0.5
    for h in range(RET_HEADS):
        sl = slice(h * HEAD_DIM, (h + 1) * HEAD_DIM)
        qh = q_ref[:, sl]
        kh = k_ref[:, sl]
        vb = v_ref[:, sl].astype(BF16)
        qr = qh * cos + pltpu.roll(qh, HEAD_DIM // 2, 1) * sin
        kr = (kh * cos + pltpu.roll(kh, HEAD_DIM // 2, 1) * sin) * scale
        scores = lax.dot_general(qr.astype(BF16), kr.astype(BF16), (((1,), (1,)), ((), ())),
                                 preferred_element_type=F32) * decay_ref[h]
        state = state_ref[h]
        y = _dot(scores.astype(BF16), vb)
        y = y + _dot((qr * qdec_ref[h]).astype(BF16), state.astype(BF16))
        kv = lax.dot_general((kr * kdec_ref[h]).astype(BF16), vb, (((0,), (0,)), ((), ())),
                             preferred_element_type=F32)
        state_ref[h] = math.exp(_log_gamma(h) * rows) * state + kv
        mu = jnp.mean(y, axis=-1, keepdims=True)
        yc = y - mu
        var = jnp.mean(yc * yc, axis=-1, keepdims=True)
        yn = yc * lax.rsqrt(var + EPS) * gn_ref[:, sl]
        gh = g_ref[:, sl]
        o_ref[:, sl] = (gh * jax.nn.sigmoid(gh) * yn).astype(o_ref.dtype)


def _retention(q, k, v, g, gn_w, rows=256):
    t, width = q.shape
    tok = pl.BlockSpec((rows, width), lambda i: (i, 0))
    half = HEAD_DIM // 2
    inv_freq = ROPE_THETA ** (-jnp.arange(half, dtype=F32) / half)

    def cos_sin(positions):
        ang = positions.astype(F32)[:, None] * inv_freq[None, :]
        ang = jnp.concatenate([ang, ang], axis=-1)
        return jnp.stack([jnp.cos(ang), jnp.sin(ang)])

    rope_row = cos_sin(jnp.arange(rows))
    rope_tile = cos_sin(jnp.arange(t // rows) * rows).transpose(1, 0, 2)[:, :, None, :]
    return pl.pallas_call(
        _retention_kernel,
        grid=(t // rows,),
        in_specs=[tok, tok, tok, tok, _resident((2, rows, HEAD_DIM)),
                  pl.BlockSpec((None, 2, 1, HEAD_DIM), lambda i: (i, 0, 0, 0)), _resident((1, width))],
        out_specs=tok,
        out_shape=jax.ShapeDtypeStruct((t, width), BF16),
        scratch_shapes=[pltpu.VMEM((RET_HEADS, HEAD_DIM, HEAD_DIM), F32),
                        pltpu.VMEM((RET_HEADS, rows, rows), F32),
                        pltpu.VMEM((RET_HEADS, rows, HEAD_DIM), F32),
                        pltpu.VMEM((RET_HEADS, rows, HEAD_DIM), F32)],
        compiler_params=_params(("arbitrary",)),
        name="retention",
    )(q, k, v, g, rope_row, rope_tile, gn_w)


def _cmul(ar, ai, br, bi):
    return ar * br - ai * bi, ar * bi + ai * br


def _s5_kernel(u_ref, m_ref, bst_ref, cout_ref, pw_ref, y_ref, a_ref, pr_ref, pi_ref, carry_ref):
    tq = u_ref.shape[0]
    nn = tq // S5_CHUNK
    nb = nn // SUBLANES
    half = S5_HALF

    @pl.when(pl.program_id(1) == 0)
    def _():
        carry_ref[...] = jnp.zeros_like(carry_ref)

    for l in range(S5_CHUNK):
        a_ref[:, l * LANES:(l + 1) * LANES] = u_ref[pl.ds(l, nn, stride=S5_CHUNK), :].astype(BF16)
    a = a_ref[...]
    kt = 2 * LANES
    yc = [_dot(a_ref[:, :(n + 1) * kt], m_ref[:(n + 1) * kt, n * kt:(n + 1) * kt])
          for n in range(S5_CHUNK // 2)]
    s_in = _dot(a, bst_ref[...])
    sr, si = s_in[:, :half], s_in[:, half:]

    row = lax.broadcasted_iota(I32, (nn, half), 0)
    zr = jnp.where(row == 0, carry_ref[0:1, :], pltpu.roll(sr, 1, 0))
    zi = jnp.where(row == 0, carry_ref[1:2, :], pltpu.roll(si, 1, 0))
    in_block = jnp.bitwise_and(row, SUBLANES - 1)
    for d in (1, 2, 4):
        mr, mi = _cmul(pw_ref[0, d - 1:d, :], pw_ref[1, d - 1:d, :],
                       pltpu.roll(zr, d, 0), pltpu.roll(zi, d, 0))
        keep = in_block >= d
        zr = zr + jnp.where(keep, mr, 0.0)
        zi = zi + jnp.where(keep, mi, 0.0)
    pr_ref[...] = zr.reshape(nb, SUBLANES, half)
    pi_ref[...] = zi.reshape(nb, SUBLANES, half)

    wr, wi = pw_ref[0], pw_ref[1]

    def block_step(r, c):
        fr, fi = _cmul(wr, wi, c[0], c[1])
        nr = pr_ref[r] + fr
        ni = pi_ref[r] + fi
        pr_ref[r] = nr
        pi_ref[r] = ni
        last = slice(SUBLANES - 1, SUBLANES)
        return (jnp.broadcast_to(nr[last, :], (SUBLANES, half)),
                jnp.broadcast_to(ni[last, :], (SUBLANES, half)))

    zero = jnp.zeros((SUBLANES, half), F32)
    lr, li = lax.fori_loop(0, nb, block_step, (zero, zero))
    er, ei = _cmul(wr[0:1, :], wi[0:1, :], lr[0:1, :], li[0:1, :])
    carry_ref[0:1, :] = er + sr[nn - 1:nn, :]
    carry_ref[1:2, :] = ei + si[nn - 1:nn, :]

    p = jnp.concatenate([pr_ref[...].reshape(nn, half), pi_ref[...].reshape(nn, half)], axis=-1)
    y_cross = _dot(p.astype(BF16), cout_ref[...])
    for l in range(S5_CHUNK):
        y_intra = yc[l // 2][:, (l % 2) * LANES:(l % 2 + 1) * LANES]
        y_ref[pl.ds(l, nn, stride=S5_CHUNK), :] = y_intra + y_cross[:, l * LANES:(l + 1) * LANES]


def _s5(u, m_op, bst_op, cout_op, pw_op, tq):
    t, width = u.shape
    nn = tq // S5_CHUNK
    kdim = S5_CHUNK * LANES

    def tile_block(*shape):
        n = len(shape)
        return pl.BlockSpec((None,) + shape, lambda o, i: (o,) + (0,) * n)

    tok = pl.BlockSpec((tq, LANES), lambda o, i: (i, o))
    return pl.pallas_call(
        _s5_kernel,
        grid=(width // LANES, t // tq),
        in_specs=[tok, tile_block(kdim, kdim), tile_block(kdim, 2 * S5_HALF),
                  tile_block(2 * S5_HALF, kdim), tile_block(2, SUBLANES, S5_HALF)],
        out_specs=tok,
        out_shape=jax.ShapeDtypeStruct((t, width), F32),
        scratch_shapes=[pltpu.VMEM((nn, kdim), BF16),
                        pltpu.VMEM((nn // SUBLANES, SUBLANES, S5_HALF), F32),
                        pltpu.VMEM((nn // SUBLANES, SUBLANES, S5_HALF), F32),
                        pltpu.VMEM((2, S5_HALF), F32)],
        compiler_params=_params(("parallel", "arbitrary")),
        name="s5",
    )(u, m_op, bst_op, cout_op, pw_op)


def _s5_operators(lam_re, lam_im, log_dt, b_re, b_im, c_re, c_im):
    g, p, cg = b_re.shape
    lc, nt, tg = S5_CHUNK, S5_TILES, S5_TILE_GROUPS
    lam = lax.complex(jnp.minimum(lam_re, -1e-4), lam_im)
    log_lam_bar = lam * jnp.exp(log_dt)[:, None]
    lam_bar = jnp.exp(log_lam_bar)
    b_bar = ((lam_bar - 1.0) / lam)[..., None] * lax.complex(b_re, b_im)
    c_mat = lax.complex(c_re, c_im)

    def power(n):
        n = jnp.asarray(n, F32)
        return jnp.exp(n[:, None, None].astype(jnp.complex64) * log_lam_bar)

    lags = jnp.arange(lc)
    taps = jnp.real(jnp.einsum('gap,dgp,gpc->dgac', c_mat, power(lags), b_bar,
                               precision=lax.Precision.HIGHEST))
    taps = taps.reshape(lc, nt, tg, cg, cg).transpose(1, 2, 4, 0, 3).reshape(nt, tg * cg, lc * cg)
    taps = jnp.pad(taps, ((0, 0), (0, 0), ((lc - 1) * cg, 0)))
    kc = jnp.stack([taps[:, :, (lc - 1 - l) * cg:(2 * lc - 1 - l) * cg] for l in range(lc)], axis=1)
    kc = kc.reshape(nt, lc * tg * cg, lc * cg)

    bst = power(lc - 1 - lags)[:, :, None, :] * b_bar.transpose(0, 2, 1)[None]
    bst = bst.reshape(lc, nt, tg * cg, p).transpose(1, 0, 2, 3).reshape(nt, lc * tg * cg, p)
    bc = jnp.concatenate([jnp.real(bst), jnp.imag(bst)], axis=-1)

    pw_out = jnp.exp((lags + 1).astype(jnp.complex64)[None, None, :] * log_lam_bar[:, :, None])
    cout = pw_out[:, :, :, None] * c_mat.transpose(0, 2, 1)[:, :, None, :]
    cout = cout.reshape(nt, tg * p, lc * cg)
    cc = jnp.concatenate([jnp.real(cout), -jnp.imag(cout)], axis=1)

    def replicate(n_in, size):
        i = jnp.arange(n_in)[:, None]
        j = jnp.arange(n_in * tg)[None, :]
        return jnp.logical_and(i // size == j // (tg * size), i % size == j % size).astype(BF16)

    def spread(compact, size_in, row_size, col_size):
        full = jnp.einsum('trk,kc->trc', compact.astype(BF16), replicate(compact.shape[2], size_in),
                          preferred_element_type=F32)
        row_group = (jnp.arange(full.shape[1]) // row_size) % tg
        col_group = (jnp.arange(full.shape[2]) // col_size) % tg
        return jnp.where(row_group[:, None] == col_group[None, :], full, 0.0).astype(BF16)

    m_op = spread(kc, cg, cg, cg)
    bst_op = spread(bc, p, cg, p)
    cout_op = spread(cc, cg, p, cg)

    pw = power(lc * (1 + jnp.arange(SUBLANES)))
    pw = pw.reshape(SUBLANES, nt, tg * p).transpose(1, 0, 2)
    pw_op = jnp.stack([jnp.real(pw), jnp.imag(pw)], axis=1).astype(F32)
    return m_op, bst_op, cout_op, pw_op


def _merge_kernel(x_ref, nw_ref, yret_ref, ys_ref, u_ref, d_ref, wglu_ref, wm_ref, wa_ref, wb_ref,
                  o_ref):
    d_model = x_ref.shape[1]
    h = _rms(x_ref[...], nw_ref[...]).astype(BF16)
    y = jax.nn.gelu(ys_ref[...] + d_ref[...] * u_ref[...])
    y_ssm = (y * jax.nn.sigmoid(_dot(y.astype(BF16), wglu_ref[...]))).astype(BF16)
    y_ret = yret_ref[...]
    cw = 512
    for c in range(d_model // cw):
        cs = slice(c * cw, (c + 1) * cw)
        gs = slice(d_model + c * cw, d_model + (c + 1) * cw)
        g0 = jax.nn.sigmoid(_dot(h, wm_ref[:, cs]))
        g1 = jax.nn.sigmoid(_dot(h, wm_ref[:, gs]))
        o_ref[:, cs] = (g0 * _dot(y_ret, wa_ref[:, cs]) + g1 * _dot(y_ssm, wb_ref[:, cs])).astype(o_ref.dtype)


def _merge(x, norm_w, y_ret, y_s5, u, ssm_d, w_glu, w_merge, w_a, w_b, tm=512):
    t, d = x.shape
    width = y_ret.shape[1]
    tok = lambda w: pl.BlockSpec((tm, w), lambda i: (i, 0))
    return pl.pallas_call(
        _merge_kernel,
        grid=(t // tm,),
        in_specs=[tok(d), _resident((1, d)), tok(width), tok(width), tok(width), _resident((1, width)),
                  _resident(w_glu.shape), _resident(w_merge.shape), _resident(w_a.shape),
                  _resident(w_b.shape)],
        out_specs=tok(d),
        out_shape=jax.ShapeDtypeStruct((t, d), BF16),
        compiler_params=_params(("parallel",)),
        name="merge",
    )(x, norm_w, y_ret, y_s5, u, ssm_d, w_glu, w_merge, w_a, w_b)


def _first_argmax(vals, lane):
    top = jnp.max(vals, axis=-1, keepdims=True)
    idx = jnp.min(jnp.where(vals == top, lane, ROUTER_LANES), axis=-1, keepdims=True)
    return top, idx


def _bf16_bits(x):
    return lax.bitcast_convert_type(x.astype(BF16).astype(F32), U32)


def _packed_store(ref, value, lead=()):
    rows, d = value.shape
    for i in range(PACK):
        lo = lax.shift_right_logical(_bf16_bits(value[:, i * LANES:(i + 1) * LANES]), U32(16))
        hi = _bf16_bits(value[:, d // 2 + i * LANES:d // 2 + (i + 1) * LANES])
        ref[lead + (pl.ds(i, rows, stride=PACK), slice(None))] = jnp.bitwise_or(hi, lo)


def _packed_pieces(ref, i, rows, lead=()):
    word = ref[lead + (pl.ds(i, rows, stride=PACK), slice(None))]
    lo = lax.bitcast_convert_type(lax.shift_left(word, U32(16)), F32)
    hi = lax.bitcast_convert_type(jnp.bitwise_and(word, U32(0xFFFF0000)), F32)
    return lo, hi


def _outproj_kernel(m_ref, x_ref, wo_ref, nw_ref, wr_ref, br_ref, x1_ref, xp_ref, ri_ref, rw_ref, cnt_ref):
    x1 = x_ref[...] + _dot(m_ref[...], wo_ref[...])
    x1_ref[...] = x1
    _packed_store(xp_ref, x1)
    h2 = _rms(x1, nw_ref[...])
    h_hi = h2.astype(BF16)
    h_lo = (h2 - h_hi.astype(F32)).astype(BF16)
    part = _dot(h_hi, wr_ref[...])
    logits = (part[:, :ROUTER_LANES] + part[:, ROUTER_LANES:]
              + _dot(h_lo, wr_ref[:, :ROUTER_LANES]) + br_ref[...])

    lane = lax.broadcasted_iota(I32, logits.shape, 1)
    neg = -jnp.inf
    is_group = lane < N_GROUPS
    g_top, g_idx = _first_argmax(jnp.where(is_group, logits, neg), lane)
    g_w = 1.0 / jnp.sum(jnp.where(is_group, jnp.exp(logits - g_top), 0.0), axis=-1, keepdims=True)
    first = N_GROUPS + EXPERTS_PER_GROUP * g_idx
    e_logits = jnp.where(lane >= first, jnp.where(lane < first + EXPERTS_PER_GROUP, logits, neg), neg)
    t1, i1 = _first_argmax(e_logits, lane)
    t2, i2 = _first_argmax(jnp.where(lane == i1, neg, e_logits), lane)
    ratio = jnp.exp(t2 - t1)
    w1 = g_w / (1.0 + ratio)
    w2 = w1 * ratio
    e1 = i1 - N_GROUPS
    e2 = i2 - N_GROUPS
    ri_ref[...] = jnp.where(lane == 0, e1, jnp.where(lane == 1, e2, 0))
    rw_ref[...] = jnp.where(lane == 0, w1, jnp.where(lane == 1, w2, 0.0))

    @pl.when(pl.program_id(0) == 0)
    def _():
        cnt_ref[...] = jnp.zeros_like(cnt_ref)

    picked = jnp.where(lane == e1, 1.0, 0.0) + jnp.where(lane == e2, 1.0, 0.0)
    cnt_ref[...] += jnp.sum(picked, axis=0, keepdims=True)


def _outproj(merged, x, w_out, norm_w, w_router, b_router, tm=512):
    t, d = x.shape
    tok = lambda w: pl.BlockSpec((tm, w), lambda i: (i, 0))
    return pl.pallas_call(
        _outproj_kernel,
        grid=(t // tm,),
        in_specs=[tok(d), tok(d), _resident(w_out.shape), _resident((1, d)),
                  _resident(w_router.shape), _resident((1, ROUTER_LANES))],
        out_specs=[tok(d), pl.BlockSpec((tm * PACK, LANES), lambda i: (i, 0)), tok(ROUTER_LANES),
                   tok(ROUTER_LANES), pl.BlockSpec((1, ROUTER_LANES), lambda i: (0, 0))],
        out_shape=[jax.ShapeDtypeStruct((t, d), F32),
                   jax.ShapeDtypeStruct((t * PACK, LANES), U32),
                   jax.ShapeDtypeStruct((t, ROUTER_LANES), I32),
                   jax.ShapeDtypeStruct((t, ROUTER_LANES), F32),
                   jax.ShapeDtypeStruct((1, ROUTER_LANES), F32)],
        compiler_params=_params(("arbitrary",)),
        name="outproj",
    )(merged, x, w_out, norm_w, w_router, b_router)


def _moe_kernel(be_ref, nu_ref, tok_ref, dst_ref,
                x_hbm, nw_ref, wg_ref, wu_ref, wd_ref,
                out_hbm,
                xbuf, ybuf, zbuf, h_ref, wgu_bf, wd_bf, gsem, ssem):
    s = pl.program_id(0)
    n_used = nu_ref[0]
    rows, d_model = h_ref.shape
    f = wg_ref.shape[1]
    slot = lax.rem(s, 2)
    other = 1 - slot
    n_xbuf = xbuf.shape[0]
    xslot = lax.rem(s, n_xbuf)
    xslot_ahead = lax.rem(s + n_xbuf - 1, n_xbuf)

    def gather_row(blk, buf_slot, r):
        tok = pl.multiple_of(tok_ref[blk * rows + r], PACK)
        return pltpu.make_async_copy(x_hbm.at[pl.ds(tok, PACK)], xbuf.at[buf_slot, pl.ds(r * PACK, PACK)],
                                     gsem.at[buf_slot])

    def scatter_row(blk, buf_slot, r):
        dst = pl.multiple_of(dst_ref[(blk + 1) * rows + r], PACK)
        return pltpu.make_async_copy(ybuf.at[buf_slot, pl.ds(r * PACK, PACK)], out_hbm.at[pl.ds(dst, PACK)],
                                     ssem.at[buf_slot])

    def wait_slot(buf, sem, buf_slot):
        pltpu.make_async_copy(buf.at[buf_slot], buf.at[buf_slot], sem.at[buf_slot]).wait()

    @pl.when(s == 0)
    def _prologue():
        ybuf[...] = jnp.zeros_like(ybuf)
        zbuf[...] = jnp.zeros_like(zbuf)
        for blk in range(n_xbuf - 1):
            for r in range(rows):
                gather_row(blk, blk, r).start()

    @pl.when(jnp.logical_and(s < n_used,
                             jnp.logical_or(s == 0, be_ref[s] != be_ref[jnp.maximum(s - 1, 0)])))
    def _cast_weights():
        wgu_bf[:, :f] = wg_ref[...].astype(BF16)
        wgu_bf[:, f:] = wu_ref[...].astype(BF16)
        wd_bf[...] = wd_ref[...].astype(BF16)

    @pl.when(s < n_used)
    def _block():
        wait_slot(xbuf, gsem, xslot)
        for r in range(rows):
            scatter_row(s - 1, other, r).start()
        ss = jnp.zeros((rows, 1), F32)
        for i in range(PACK):
            lo, hi = _packed_pieces(xbuf, i, rows, (xslot,))
            ss = ss + jnp.sum(lo * lo + hi * hi, axis=-1, keepdims=True)
        inv = lax.rsqrt(ss * (1.0 / d_model) + EPS)
        for i in range(PACK):
            lo, hi = _packed_pieces(xbuf, i, rows, (xslot,))
            lo_cols = slice(i * LANES, (i + 1) * LANES)
            hi_cols = slice(d_model // 2 + i * LANES, d_model // 2 + (i + 1) * LANES)
            h_ref[:, lo_cols] = (lo * inv * nw_ref[:, lo_cols]).astype(BF16)
            h_ref[:, hi_cols] = (hi * inv * nw_ref[:, hi_cols]).astype(BF16)
        for r in range(rows):
            gather_row(s + n_xbuf - 1, xslot_ahead, r).start()
        gu = _dot(h_ref[...], wgu_bf[...])
        gate = gu[:, :f]
        act = (gate * jax.nn.sigmoid(gate) * gu[:, f:]).astype(BF16)
        y = _dot(act, wd_bf[...])

        @pl.when(s >= 1)
        def _free_slot():
            wait_slot(ybuf, ssem, slot)

        _packed_store(ybuf, y, (slot,))

    @pl.when(s >= n_used)
    def _tail():
        wait_slot(ybuf, ssem, slot)

        @pl.when(s == n_used)
        def _last_block():
            for ahead in range(n_xbuf - 1):
                wait_slot(xbuf, gsem, lax.rem(s + ahead, n_xbuf))
            for r in range(rows):
                scatter_row(s - 1, other, r).start()

        @pl.when(s > n_used)
        def _padding_block():
            dst = pl.multiple_of(dst_ref[s * rows], PACK)
            pltpu.make_async_copy(zbuf, out_hbm.at[pl.ds(dst, rows * PACK)], ssem.at[other]).start()

    @pl.when(s == pl.num_programs(0) - 1)
    def _drain():
        wait_slot(ybuf, ssem, other)


def _moe(block_expert, n_used, row_tok, row_dst, x1_packed, norm_w, w_gate, w_up, w_down):
    n_steps = block_expert.shape[0]
    _, d, f = w_gate.shape
    assert d == 2 * PACK * LANES

    def expert_block(shape):
        return pl.BlockSpec((None,) + shape, lambda b, be, *_: (be[b], 0, 0))

    grid_spec = pltpu.PrefetchScalarGridSpec(
        num_scalar_prefetch=4,
        grid=(n_steps,),
        in_specs=[pl.BlockSpec(memory_space=pl.ANY),
                  pl.BlockSpec((1, d), lambda b, *_: (0, 0)),
                  expert_block((d, f)), expert_block((d, f)), expert_block((f, d))],
        out_specs=pl.BlockSpec(memory_space=pl.ANY),
        scratch_shapes=[pltpu.VMEM((MOE_GATHER_SLOTS, MOE_ROWS * PACK, LANES), U32),
                        pltpu.VMEM((2, MOE_ROWS * PACK, LANES), U32),
                        pltpu.VMEM((MOE_ROWS * PACK, LANES), U32),
                        pltpu.VMEM((MOE_ROWS, d), BF16),
                        pltpu.VMEM((d, 2 * f), BF16),
                        pltpu.VMEM((f, d), BF16),
                        pltpu.SemaphoreType.DMA((MOE_GATHER_SLOTS,)),
                        pltpu.SemaphoreType.DMA((2,))],
    )
    return pl.pallas_call(
        _moe_kernel,
        grid_spec=grid_spec,
        out_shape=jax.ShapeDtypeStruct((row_dst.shape[0] * PACK, LANES), U32),
        compiler_params=pltpu.CompilerParams(dimension_semantics=("arbitrary",),
                                             vmem_limit_bytes=VMEM_LIMIT_BYTES,
                                             has_side_effects=True),
        name="moe",
    )(block_expert, n_used, row_tok * PACK, row_dst * PACK, x1_packed, norm_w, w_gate, w_up, w_down)


def _row_layout(expert, counts):
    t = expert.shape[0]
    n_assign = 2 * t
    n_blocks = n_assign // MOE_ROWS + N_EXPERTS
    flat_e = expert.reshape(-1)
    order = jnp.argsort(flat_e).astype(I32)
    start = jnp.cumsum(counts) - counts
    padded = (counts + MOE_ROWS - 1) // MOE_ROWS * MOE_ROWS
    pad_end = jnp.cumsum(padded)
    pad_start = pad_end - padded
    block_start = jnp.arange(n_blocks, dtype=I32) * MOE_ROWS
    n_rows = jnp.sum(padded)
    used = block_start < n_rows
    block_expert = jnp.minimum(jnp.sum(pad_end[None, :] <= block_start[:, None], axis=1, dtype=I32),
                               N_EXPERTS - 1)
    last_expert = jnp.max(jnp.where(used, block_expert, 0))
    block_expert = jnp.where(used, block_expert, last_expert)
    is_expert = block_expert[:, None] == jnp.arange(N_EXPERTS, dtype=I32)[None, :]

    def of_block(per_expert):
        return jnp.sum(jnp.where(is_expert, per_expert[None, :], 0), axis=1, dtype=I32)

    rank0 = block_start - of_block(pad_start)
    n_valid = jnp.where(used, jnp.clip(of_block(counts) - rank0, 0, MOE_ROWS), 0).astype(I32)
    within = jnp.arange(MOE_ROWS, dtype=I32)[None, :]
    valid = within < n_valid[:, None]
    src = order[jnp.clip(of_block(start)[:, None] + rank0[:, None] + within, 0, n_assign - 1)]
    tok = src // 2
    n_pad = MOE_ROWS - n_valid
    spare = n_assign + MOE_ROWS + (jnp.cumsum(n_pad) - n_pad)[:, None] + (within - n_valid[:, None])
    row_dst = jnp.where(valid, (src % 2) * t + tok, spare).astype(I32)
    row_dst = jnp.concatenate([n_assign + within, row_dst], axis=0).reshape(-1)
    row_tok = jnp.concatenate([jnp.where(valid, tok, 0), jnp.zeros((2, MOE_ROWS), I32)], axis=0).reshape(-1)
    block_expert = jnp.concatenate([block_expert, last_expert[None]])
    n_used = (n_rows // MOE_ROWS).astype(I32).reshape(1)
    return block_expert, n_used, row_tok, row_dst


def _final_kernel(x1_ref, r0_ref, r1_ref, rw_ref, p_ref, nple_ref, wg_ref, wp_ref, nf_ref, o_ref, x2_ref):
    tm, d = x2_ref.shape
    w0, w1 = rw_ref[:, 0:1], rw_ref[:, 1:2]
    for i in range(PACK):
        lo0, hi0 = _packed_pieces(r0_ref, i, tm)
        lo1, hi1 = _packed_pieces(r1_ref, i, tm)
        lo_cols = slice(i * LANES, (i + 1) * LANES)
        hi_cols = slice(d // 2 + i * LANES, d // 2 + (i + 1) * LANES)
        x2_ref[:, lo_cols] = x1_ref[:, lo_cols] + (w0 * lo0 + w1 * lo1)
        x2_ref[:, hi_cols] = x1_ref[:, hi_cols] + (w0 * hi0 + w1 * hi1)
    x2 = x2_ref[...]
    h3 = _rms(x2, nple_ref[...]).astype(BF16)
    gate = jax.nn.sigmoid(_dot(h3, wg_ref[...]))
    x3 = x2 + gate * _dot(p_ref[...].astype(BF16), wp_ref[...])
    o_ref[...] = _rms(x3, nf_ref[...])


def _final(x1, moe_packed, route_w, p, norm_ple, w_ple_gate, w_ple, norm_f, tm=512):
    t, d = x1.shape
    nt = t // tm
    tok = lambda w: pl.BlockSpec((tm, w), lambda i: (i, 0))
    packed = lambda first: pl.BlockSpec((tm * PACK, LANES), lambda i: (i + first, 0))
    return pl.pallas_call(
        _final_kernel,
        grid=(nt,),
        in_specs=[tok(d), packed(0), packed(nt), tok(ROUTER_LANES),
                  tok(p.shape[1]), _resident((1, d)), _resident(w_ple_gate.shape),
                  _resident(w_ple.shape), _resident((1, d))],
        out_specs=tok(d),
        out_shape=jax.ShapeDtypeStruct((t, d), F32),
        scratch_shapes=[pltpu.VMEM((tm, d), F32)],
        compiler_params=_params(("parallel",)),
        name="final",
    )(x1, moe_packed, moe_packed, route_w, p, norm_ple, w_ple_gate, w_ple, norm_f)


def _layer(x, p, norm_mix, w_in, ret_gn_w, lam_re, lam_im, log_dt, b_re, b_im, c_re, c_im, ssm_d,
           w_glu, w_branch_a, w_branch_b, w_merge, w_out, norm_ffn, w_rg, b_rg, w_re, b_re_router,
           w_exp_gate, w_exp_up, w_exp_down, norm_ple, w_ple_gate, w_ple, norm_f):
    t, d = x.shape
    row = lambda v: v.reshape(1, -1).astype(F32)

    q, k, v, g, u = _proj(x, row(norm_mix), w_in.astype(BF16))
    y_ret = _retention(q, k, v, g, row(ret_gn_w))

    ops = _s5_operators(lam_re, lam_im, log_dt, b_re, b_im, c_re, c_im)
    y_s5 = _s5(u, *ops, tq=min(t, 4096))

    merged = _merge(x, row(norm_mix), y_ret, y_s5, u, row(ssm_d), w_glu.astype(BF16),
                    w_merge.astype(BF16), w_branch_a.astype(BF16), w_branch_b.astype(BF16))

    pad = ROUTER_LANES - N_GROUPS - N_EXPERTS
    w_router = jnp.concatenate([w_rg, w_re, jnp.zeros((d, pad), F32)], axis=1).astype(F32)
    w_router_hi = w_router.astype(BF16)
    w_router_lo = (w_router - w_router_hi.astype(F32)).astype(BF16)
    b_router = jnp.concatenate([b_rg, b_re_router, jnp.zeros((pad,), F32)]).reshape(1, ROUTER_LANES).astype(F32)
    x1, x1_packed, route_i, route_w, counts = _outproj(
        merged, x, w_out.astype(BF16), row(norm_ffn),
        jnp.concatenate([w_router_hi, w_router_lo], axis=1), b_router)

    block_expert, n_used, row_tok, row_dst = _row_layout(route_i[:, :2], counts[0, :N_EXPERTS].astype(I32))
    moe_rows = _moe(block_expert, n_used, row_tok, row_dst, x1_packed, row(norm_ffn),
                    w_exp_gate, w_exp_up, w_exp_down)

    return _final(x1, moe_rows, route_w, p, row(norm_ple), w_ple_gate.astype(BF16), w_ple.astype(BF16),
                  row(norm_f))


def kernel(x, p, norm_mix, w_in, ret_gn_w, ssm_lam_re, ssm_lam_im, ssm_log_dt, ssm_b_re, ssm_b_im, ssm_c_re, ssm_c_im, ssm_d, w_glu, w_branch_a, w_branch_b, w_merge, w_out, norm_ffn, w_router_group, b_router_group, w_router_expert, b_router_expert, w_exp_gate, w_exp_up, w_exp_down, norm_ple, w_ple_gate, w_ple, norm_f):
    depth, bsz, seq, _ = p.shape
    assert depth == 1 and bsz == 1, "single layer, single sequence"
    out = _layer(x[0], p[0, 0], norm_mix[0], w_in[0], ret_gn_w[0], ssm_lam_re[0], ssm_lam_im[0],
                 ssm_log_dt[0], ssm_b_re[0], ssm_b_im[0], ssm_c_re[0], ssm_c_im[0], ssm_d[0], w_glu[0],
                 w_branch_a[0], w_branch_b[0], w_merge[0], w_out[0], norm_ffn[0], w_router_group[0],
                 b_router_group[0], w_router_expert[0], b_router_expert[0], w_exp_gate[0], w_exp_up[0],
                 w_exp_down[0], norm_ple[0], w_ple_gate[0], w_ple[0], norm_f)
    return out[None]
```
